```python
import jax, jax.numpy as jnp
from jax import lax
import numpy as np

D_MODEL = 2048
BATCH = 8
SEQ = 2048
DEPTH = 1

GRID_W = 64
CTX_LEN = 256
N_HEADS = 16
HEAD_DIM = D_MODEL // N_HEADS
ATTN_WIDTH = N_HEADS * HEAD_DIM
F_GROUPS = 4
F_GROUP_DIM = D_MODEL // 8
F_WIDTH = F_GROUPS * F_GROUP_DIM
MAX_WIN_R = 8
WIN_C = 16
ROT_PER_AXIS = HEAD_DIM // 2
ROPE_BASE = 10000.0
EPS = 1e-6

OFF_ZF = F_WIDTH
OFF_Q = 2 * F_WIDTH
OFF_K = OFF_Q + ATTN_WIDTH
OFF_V = OFF_K + ATTN_WIDTH
OFF_ZA = OFF_V + ATTN_WIDTH
OFF_GF = OFF_ZA + ATTN_WIDTH
OFF_GA = OFF_GF + D_MODEL
IN_WIDTH = OFF_GA + D_MODEL
SPLIT_POINTS = (OFF_ZF, OFF_Q, OFF_K, OFF_V, OFF_ZA, OFF_GF, OFF_GA)

kernel_name = "hybrid_fourier_natten_dit_block"


def _rms(x):
    xf = x.astype(jnp.float32)
    return (xf * lax.rsqrt(jnp.mean(xf * xf, axis=-1, keepdims=True) + EPS)).astype(x.dtype)


def _qk_norm(x, gain):
    return _rms(x) * gain.astype(x.dtype)


def _heads(t):
    B, N, _ = t.shape
    return t.reshape(B, N, N_HEADS, HEAD_DIM).transpose(0, 2, 1, 3)


def _merge_heads(t):
    B, H, N, Dh = t.shape
    return t.transpose(0, 2, 1, 3).reshape(B, N, H * Dh)


def _axial_rope(n_tok):
    t = jnp.arange(n_tok)
    pos = jnp.stack([t // GRID_W, t % GRID_W], axis=-1).astype(jnp.float32)
    n_freq = ROT_PER_AXIS // 2
    inv_freq = ROPE_BASE ** (-jnp.arange(n_freq, dtype=jnp.float32) / n_freq)
    ang = pos[..., None] * inv_freq
    return jnp.cos(ang), jnp.sin(ang)


def _apply_rope(x, cos, sin):
    B, H, N, Dh = x.shape
    xr = x.reshape(B, H, N, 2, 2, ROT_PER_AXIS // 2)
    x0, x1 = xr[..., 0, :], xr[..., 1, :]
    cos = cos.astype(x.dtype)
    sin = sin.astype(x.dtype)
    out = jnp.stack([x0 * cos - x1 * sin, x1 * cos + x0 * sin], axis=-2)
    return out.reshape(B, H, N, Dh)


def _fourier_mix(u):
    B, N, _ = u.shape
    ug = u.reshape(B, N, F_GROUPS, F_GROUP_DIM).astype(jnp.float32)
    y = jnp.fft.fft2(ug, axes=(1, 3), norm="ortho").real
    return y.reshape(B, N, F_WIDTH).astype(u.dtype)


def _merge_branches(u_f, z_f, o, z_a, g_f, g_a, w_f_out, w_a_out, w_out):
    y_f = (_fourier_mix(u_f) * jax.nn.silu(z_f)) @ w_f_out
    y_a = (o * jax.nn.silu(z_a)) @ w_a_out
    y = jax.nn.sigmoid(g_f) * y_f + jax.nn.sigmoid(g_a) * y_a
    return y @ w_out


def _neighbourhood_attention(q, k, v, k_ctx, v_ctx, rpb):
    B, H, S, Dh = q.shape
    rows = S // GRID_W
    win_r = min(MAX_WIN_R, rows)
    scale = HEAD_DIM ** -0.5
    qg = q.reshape(B, H, rows, GRID_W, Dh)
    kg = k.reshape(B, H, rows, GRID_W, Dh)
    vg = v.reshape(B, H, rows, GRID_W, Dh)
    cols = jnp.arange(GRID_W)
    col_start = jnp.clip(cols - WIN_C // 2, 0, GRID_W - WIN_C)
    col_idx = col_start[:, None] + jnp.arange(WIN_C)[None, :]
    dc_idx = col_idx - cols[:, None] + (WIN_C - 1)

    def one_row(r):
        r_start = jnp.clip(r - win_r // 2, 0, rows - win_r)
        q_r = lax.dynamic_index_in_dim(qg, r, axis=2, keepdims=False)
        k_band = lax.dynamic_slice_in_dim(kg, r_start, win_r, axis=2)
        v_band = lax.dynamic_slice_in_dim(vg, r_start, win_r, axis=2)
        k_nb = k_band[:, :, :, col_idx, :]
        v_nb = v_band[:, :, :, col_idx, :]
        dr_idx = r_start + jnp.arange(win_r) - r + (MAX_WIN_R - 1)
        bias = rpb[:, dr_idx[:, None, None], dc_idx[None, :, :]]
        bias = bias.transpose(0, 2, 1, 3).reshape(H, GRID_W, win_r * WIN_C).astype(jnp.float32)
        s_loc = jnp.einsum('bhqd,bhiqjd->bhqij', q_r, k_nb).reshape(B, H, GRID_W, win_r * WIN_C)
        s_ctx = jnp.einsum('bhqd,bhld->bhql', q_r, k_ctx)
        logits = jnp.concatenate([s_loc.astype(jnp.float32) * scale + bias,
                                  s_ctx.astype(jnp.float32) * scale], axis=-1)
        p = jax.nn.softmax(logits, axis=-1).astype(v.dtype)
        n_loc = win_r * WIN_C
        p_loc = p[..., :n_loc].reshape(B, H, GRID_W, win_r, WIN_C)
        p_ctx = p[..., n_loc:]
        return (jnp.einsum('bhqij,bhiqjd->bhqd', p_loc, v_nb)
                + jnp.einsum('bhql,bhld->bhqd', p_ctx, v_ctx))

    out = lax.map(one_row, jnp.arange(rows))
    return out.transpose(1, 2, 0, 3, 4).reshape(B, H, S, Dh)


def _latent_mixer(h, k_ctx, v_ctx, w_in, q_gain, k_gain, rpb, w_f_out, w_a_out, w_out):
    S = h.shape[1]
    u_f, z_f, q, k, v, z_a, g_f, g_a = jnp.split(h @ w_in, SPLIT_POINTS, axis=-1)
    cos, sin = _axial_rope(S)
    q = _apply_rope(_qk_norm(_heads(q), q_gain), cos, sin)
    k = _apply_rope(_qk_norm(_heads(k), k_gain), cos, sin)
    o = _merge_heads(_neighbourhood_attention(q, k, _heads(v), k_ctx, v_ctx, rpb))
    return _merge_branches(u_f, z_f, o, z_a, g_f, g_a, w_f_out, w_a_out, w_out)


def _context_kv(h_ctx, w_in, k_gain):
    k, v = jnp.split(h_ctx @ w_in[:, OFF_K:OFF_ZA], [ATTN_WIDTH], axis=-1)
    return _qk_norm(_heads(k), k_gain), _heads(v)


def _context_mixer(h_ctx, w_in, q_gain, k_gain, w_f_out, w_a_out, w_out):
    u_f, z_f, q, k, v, z_a, g_f, g_a = jnp.split(h_ctx @ w_in, SPLIT_POINTS, axis=-1)
    q = _qk_norm(_heads(q), q_gain)
    k = _qk_norm(_heads(k), k_gain)
    v = _heads(v)
    logits = jnp.einsum('bhqd,bhkd->bhqk', q, k).astype(jnp.float32) * (HEAD_DIM ** -0.5)
    p = jax.nn.softmax(logits, axis=-1).astype(v.dtype)
    o = _merge_heads(jnp.einsum('bhqk,bhkd->bhqd', p, v))
    return _merge_branches(u_f, z_f, o, z_a, g_f, g_a, w_f_out, w_a_out, w_out), k, v


def setup_inputs(seed: int = 0) -> dict:
    key = jax.random.key(seed)
    ks = jax.random.split(key, 14)
    f32 = jnp.float32
    nrm = lambda k, shape, s: jax.random.normal(k, shape, f32) * s
    return {
        "x": nrm(ks[0], (BATCH, SEQ, D_MODEL), 1.0),
        "c": nrm(ks[1], (BATCH, D_MODEL), 1.0),
        "ctx": nrm(ks[2], (BATCH, CTX_LEN, D_MODEL), 1.0),
        "c_ctx": nrm(ks[3], (D_MODEL,), 1.0),
        "w_mod": nrm(ks[4], (DEPTH, D_MODEL, 3 * D_MODEL), D_MODEL ** -0.5),
        "b_mod": nrm(ks[5], (DEPTH, 3 * D_MODEL), 0.02),
        "w_in": nrm(ks[6], (DEPTH, D_MODEL, IN_WIDTH), D_MODEL ** -0.5),
        "q_gain": 1.0 + nrm(ks[7], (DEPTH, HEAD_DIM), 0.02),
        "k_gain": 1.0 + nrm(ks[8], (DEPTH, HEAD_DIM), 0.02),
        "rpb": nrm(ks[9], (DEPTH, N_HEADS, 2 * MAX_WIN_R - 1, 2 * WIN_C - 1), 0.1),
        "w_f_out": nrm(ks[10], (DEPTH, F_WIDTH, D_MODEL), F_WIDTH ** -0.5),
        "w_a_out": nrm(ks[11], (DEPTH, ATTN_WIDTH, D_MODEL), ATTN_WIDTH ** -0.5),
        "w_out": nrm(ks[12], (DEPTH, D_MODEL, D_MODEL), D_MODEL ** -0.5),
    }


def reference(x, c, ctx, c_ctx, w_mod, b_mod, w_in, q_gain, k_gain, rpb, w_f_out, w_a_out, w_out):
    silu_c = jax.nn.silu(c)
    silu_cc = jax.nn.silu(c_ctx)
    for l in range(DEPTH):
        mod_x = silu_c @ w_mod[l] + b_mod[l]
        mod_c = silu_cc @ w_mod[l] + b_mod[l]
        shift_x, scale_x, gate_x = jnp.split(mod_x[:, None, :], 3, axis=-1)
        shift_c, scale_c, gate_c = jnp.split(mod_c, 3, axis=-1)
        h_ctx = _rms(ctx) * (1.0 + scale_c) + shift_c
        h_x = _rms(x) * (1.0 + scale_x) + shift_x
        if l < DEPTH - 1:
            y_ctx, k_ctx, v_ctx = _context_mixer(h_ctx, w_in[l], q_gain[l], k_gain[l],
                                                 w_f_out[l], w_a_out[l], w_out[l])
            ctx_next = ctx + gate_c * y_ctx
        else:
            k_ctx, v_ctx = _context_kv(h_ctx, w_in[l], k_gain[l])
            ctx_next = ctx
        y_x = _latent_mixer(h_x, k_ctx, v_ctx, w_in[l], q_gain[l], k_gain[l], rpb[l],
                            w_f_out[l], w_a_out[l], w_out[l])
        x = x + gate_x * y_x
        ctx = ctx_next
    return x
```

```python
import functools

import numpy as np
import jax
import jax.numpy as jnp
from jax import lax
from jax.experimental import pallas as pl
from jax.experimental.pallas import tpu as pltpu

GRID_W = 64
F_GROUPS = 4
ROPE_BASE = 10000.0
EPS = 1e-6
NEG = -1e30
LANES = 128
VMEM_LIMIT = 56 * 1024 * 1024

BF16 = jnp.bfloat16
F32 = jnp.float32


def _nt_dot(a, b):
    return lax.dot_general(a, b, (((1,), (1,)), ((), ())), preferred_element_type=F32)


def _mod_kernel(c_ref, w_ref, b_ref, o_ref):
    a = jax.nn.silu(c_ref[...])
    o_ref[...] = jnp.dot(a, w_ref[...], preferred_element_type=F32,
                         precision=lax.Precision.HIGHEST) + b_ref[...]


def _mod(c_all, w_mod, b_mod, tn=512):
    m, d = c_all.shape
    n = w_mod.shape[1]
    return pl.pallas_call(
        _mod_kernel,
        grid=(n // tn,),
        in_specs=[pl.BlockSpec((m, d), lambda j: (0, 0)),
                  pl.BlockSpec((d, tn), lambda j: (0, j)),
                  pl.BlockSpec((1, tn), lambda j: (0, j))],
        out_specs=pl.BlockSpec((m, tn), lambda j: (0, j)),
        out_shape=jax.ShapeDtypeStruct((m, n), F32),
        name="mod",
    )(c_all, w_mod, b_mod)


def _rope_swap(x):
    lane = lax.broadcasted_iota(jnp.int32, x.shape, 1)
    first = (lane % 64) < 32
    return jnp.where(first, pltpu.roll(x, 96, 1), pltpu.roll(x, 32, 1))


def _proj_kernel(types, tm, rc, q_scale,
                 x_ref, shift_ref, scale_ref, w_ref, qg_ref, kg_ref, cos_ref, sin_ref,
                 o_ref, h_ref, acc_ref):
    j = pl.program_id(1)
    tn = w_ref.shape[1]
    n_chunks = tm // rc

    @pl.when(j == 0)
    def _():
        def norm_body(t, carry):
            r = pl.multiple_of(t * rc, rc)
            xs = x_ref[pl.ds(r, rc), :]
            ms = jnp.mean(xs * xs, axis=-1, keepdims=True)
            h = xs * lax.rsqrt(ms + EPS) * (1.0 + scale_ref[0]) + shift_ref[0]
            h_ref[pl.ds(r, rc), :] = h.astype(BF16)
            return carry
        lax.fori_loop(0, n_chunks, norm_body, 0)

    mm = 256
    for r in range(0, tm, mm):
        acc_ref[r:r + mm, :] = jnp.dot(h_ref[r:r + mm, :], w_ref[...], preferred_element_type=F32)

    def epilogue(kind):
        def body(t, carry):
            r = pl.multiple_of(t * rc, rc)
            a = acc_ref[pl.ds(r, rc), :]
            if kind == "raw":
                o_ref[pl.ds(r, rc), :] = a.astype(BF16)
            elif kind == "silu":
                o_ref[pl.ds(r, rc), :] = jax.nn.silu(a).astype(BF16)
            elif kind == "sig":
                o_ref[pl.ds(r, rc), :] = jax.nn.sigmoid(a).astype(BF16)
            else:
                gain = qg_ref[...] if kind == "q" else kg_ref[...]
                if kind != "kc":
                    cos = cos_ref[pl.ds(r, rc), :]
                    sin = sin_ref[pl.ds(r, rc), :]
                for hh in range(tn // LANES):
                    xh = a[:, hh * LANES:(hh + 1) * LANES]
                    ms = jnp.mean(xh * xh, axis=-1, keepdims=True)
                    xn = xh * lax.rsqrt(ms + EPS) * gain
                    if kind != "kc":
                        xn = xn * cos + _rope_swap(xn) * sin
                    if kind == "q":
                        xn = xn * q_scale
                    o_ref[pl.ds(r, rc), hh * LANES:(hh + 1) * LANES] = xn.astype(BF16)
            return carry
        lax.fori_loop(0, n_chunks, body, 0)

    for kind in sorted(set(types)):
        js = [jj for jj, t in enumerate(types) if t == kind]
        cond = functools.reduce(jnp.logical_or, [j == jj for jj in js])
        pl.when(cond)(functools.partial(epilogue, kind))


def _proj(x2d, mod3, w_bf, q_gain, k_gain, cos_t, sin_t, *, types, col0, rows_per_mod, mod_row0,
          q_scale, tm=1024, tn=1024, rc=64):
    m, d = x2d.shape
    seq = cos_t.shape[0]
    n_j = len(types)
    tiles_per_seq = seq // tm
    tiles_per_mod = rows_per_mod // tm
    kern = functools.partial(_proj_kernel, tuple(types), tm, rc, q_scale)
    return pl.pallas_call(
        kern,
        grid=(m // tm, n_j),
        in_specs=[
            pl.BlockSpec((tm, d), lambda i, j: (i, 0)),
            pl.BlockSpec((1, 1, d), lambda i, j: (mod_row0 + i // tiles_per_mod, 0, 0)),
            pl.BlockSpec((1, 1, d), lambda i, j: (mod_row0 + i // tiles_per_mod, 0, 1)),
            pl.BlockSpec((d, tn), lambda i, j: (0, col0 + j)),
            pl.BlockSpec((1, LANES), lambda i, j: (0, 0)),
            pl.BlockSpec((1, LANES), lambda i, j: (0, 0)),
            pl.BlockSpec((tm, LANES), lambda i, j: (i % tiles_per_seq, 0)),
            pl.BlockSpec((tm, LANES), lambda i, j: (i % tiles_per_seq, 0)),
        ],
        out_specs=pl.BlockSpec((tm, tn), lambda i, j: (i, j)),
        out_shape=jax.ShapeDtypeStruct((m, n_j * tn), BF16),
        scratch_shapes=[pltpu.VMEM((tm, d), BF16), pltpu.VMEM((tm, tn), F32)],
        compiler_params=pltpu.CompilerParams(
            dimension_semantics=("parallel", "arbitrary"), vmem_limit_bytes=VMEM_LIMIT),
        name="proj",
    )(x2d, mod3, mod3, w_bf, q_gain, k_gain, cos_t, sin_t)


Q_ROWS = 2
BAND = 10


def _band_start(r0, rows):
    return min(max(r0 - 4, 0), rows - BAND)


def _build_bias_tables(rpb_ref, bias_ref, rows, win_r, win_c):
    c_io = lax.broadcasted_iota(jnp.int32, (GRID_W, LANES), 0)
    l_io = lax.broadcasted_iota(jnp.int32, (GRID_W, LANES), 1)
    cs = jnp.clip(c_io - win_c // 2, 0, GRID_W - win_c)
    inwin = (l_io >= cs) & (l_io < cs + win_c) & (l_io < GRID_W)
    low = l_io < GRID_W
    neg = jnp.full((GRID_W, LANES), NEG, F32)
    toep = []
    for dr in range(2 * win_r - 1):
        row = jnp.broadcast_to(rpb_ref[0, dr:dr + 1, :], (GRID_W, LANES))
        t = pltpu.roll(row, LANES - (win_c - 1), 1, stride=1, stride_axis=0)
        toep.append(jnp.where(inwin, t, NEG))
    reps = [0, 2, 4, rows - 4, rows - 2]
    for tb, r0 in enumerate(reps):
        s0 = _band_start(r0, rows)
        assert (r0 - s0) == 2 * tb
        for rho in range(Q_ROWS):
            r = r0 + rho
            rs = min(max(r - win_r // 2, 0), rows - win_r)
            blocks = []
            for i in range(BAND):
                kr = s0 + i
                blocks.append(toep[kr - r + win_r - 1] if rs <= kr < rs + win_r else neg)
            for p in range(BAND // 2):
                tile = jnp.where(low, blocks[2 * p], pltpu.roll(blocks[2 * p + 1], GRID_W, 1))
                bias_ref[tb, rho * GRID_W:(rho + 1) * GRID_W, p * LANES:(p + 1) * LANES] = tile


def _attn_kernel(rows, win_r, win_c,
                 q_ref, k_ref, v_ref, z_ref, kc_ref, vc_ref, rpb_ref, o_ref, bias_ref):
    @pl.when(pl.program_id(1) == 0)
    def _():
        _build_bias_tables(rpb_ref, bias_ref, rows, win_r, win_c)

    nq = Q_ROWS * GRID_W
    nk = BAND * GRID_W

    def body(t, carry):
        r0 = Q_ROWS * t
        s0 = jnp.clip(r0 - 4, 0, rows - BAND)
        tb = lax.shift_right_logical(r0 - s0, 1)
        q0 = pl.multiple_of(r0 * GRID_W, nq)
        k0 = pl.multiple_of(s0 * GRID_W, GRID_W)
        qb = q_ref[pl.ds(q0, nq), :]
        kb = k_ref[pl.ds(k0, nk), :]
        vb = v_ref[pl.ds(k0, nk), :]
        s_loc = _nt_dot(qb, kb) + bias_ref[tb]
        s_ctx = _nt_dot(qb, kc_ref[...])
        m = jnp.maximum(jnp.max(s_loc, axis=-1, keepdims=True), jnp.max(s_ctx, axis=-1, keepdims=True))
        p_loc = jnp.exp(s_loc - m)
        p_ctx = jnp.exp(s_ctx - m)
        den = jnp.sum(p_loc, axis=-1, keepdims=True) + jnp.sum(p_ctx, axis=-1, keepdims=True)
        o = (jnp.dot(p_loc.astype(BF16), vb, preferred_element_type=F32)
             + jnp.dot(p_ctx.astype(BF16), vc_ref[...], preferred_element_type=F32))
        og = (o / den) * z_ref[pl.ds(q0, nq), :].astype(F32)
        o_ref[pl.ds(q0, nq), :] = og.astype(BF16)
        return carry

    lax.fori_loop(0, rows // Q_ROWS, body, 0)


def _attention(p_x, p_c, rpb_pad, *, batch, seq, ctx_len, heads, off_q, off_k, off_v, off_z, win_r, win_c):
    rows = seq // GRID_W
    cq, ck, cv, cz = (o // LANES for o in (off_q, off_k, off_v, off_z))
    kern = functools.partial(_attn_kernel, rows, win_r, win_c)
    return pl.pallas_call(
        kern,
        grid=(heads, batch),
        in_specs=[
            pl.BlockSpec((seq, LANES), lambda h, b: (b, cq + h)),
            pl.BlockSpec((seq, LANES), lambda h, b: (b, ck + h)),
            pl.BlockSpec((seq, LANES), lambda h, b: (b, cv + h)),
            pl.BlockSpec((seq, LANES), lambda h, b: (b, cz + h)),
            pl.BlockSpec((ctx_len, LANES), lambda h, b: (b, h)),
            pl.BlockSpec((ctx_len, LANES), lambda h, b: (b, heads + h)),
            pl.BlockSpec((1,) + rpb_pad.shape[1:], lambda h, b: (h, 0, 0)),
        ],
        out_specs=pl.BlockSpec((seq, LANES), lambda h, b: (b, h)),
        out_shape=jax.ShapeDtypeStruct((batch * seq, heads * LANES), BF16),
        scratch_shapes=[pltpu.VMEM((5, Q_ROWS * GRID_W, BAND * GRID_W), F32)],
        compiler_params=pltpu.CompilerParams(
            dimension_semantics=("arbitrary", "arbitrary"), vmem_limit_bytes=VMEM_LIMIT),
        name="attn",
    )(p_x, p_x, p_x, p_x, p_c, p_c, rpb_pad)


def _dft_mats(n):
    jk = (np.arange(n)[:, None] * np.arange(n)[None, :]) % n
    ang = 2.0 * np.pi * jk.astype(np.float64) / n
    return np.cos(ang) / np.sqrt(n), np.sin(ang) / np.sqrt(n)


def _fourier_kernel(gd, u_ref, zf_ref, csc_ref, csn_ref, o_ref, ab_ref):
    seq = u_ref.shape[0]

    @pl.when(pl.program_id(1) == 0)
    def _():
        rb = 512
        for g in range(u_ref.shape[1] // gd):
            for r in range(0, seq, rb):
                t = jnp.dot(u_ref[r:r + rb, g * gd:(g + 1) * gd], csc_ref[...], preferred_element_type=F32)
                ab_ref[r:r + rb, g * gd:(g + 1) * gd] = t[:, :gd].astype(BF16)
                ab_ref[seq + r:seq + r + rb, g * gd:(g + 1) * gd] = t[:, gd:].astype(BF16)

    y = jnp.dot(csn_ref[...], ab_ref[...], preferred_element_type=F32)
    o_ref[...] = (y * zf_ref[...].astype(F32)).astype(BF16)


def _fourier(p_x, csc, csn, *, batch, seq, fw, gd, tk=512):
    steps = seq // tk
    kern = functools.partial(_fourier_kernel, gd)
    return pl.pallas_call(
        kern,
        grid=(batch, steps),
        in_specs=[
            pl.BlockSpec((seq, fw), lambda b, k: (b, 0)),
            pl.BlockSpec((tk, fw), lambda b, k: (b * steps + k, 1)),
            pl.BlockSpec(csc.shape, lambda b, k: (0, 0)),
            pl.BlockSpec((tk, 2 * seq), lambda b, k: (k, 0)),
        ],
        out_specs=pl.BlockSpec((tk, fw), lambda b, k: (b * steps + k, 0)),
        out_shape=jax.ShapeDtypeStruct((batch * seq, fw), BF16),
        scratch_shapes=[pltpu.VMEM((2 * seq, fw), BF16)],
        compiler_params=pltpu.CompilerParams(
            dimension_semantics=("parallel", "arbitrary"), vmem_limit_bytes=VMEM_LIMIT),
        name="fourier",
    )(p_x, p_x, csc, csn)


def _merge_kernel(yg_ref, og_ref, sgf_ref, sga_ref, x_ref, gate_ref, wf_ref, wa_ref, wo_ref, o_ref):
    yf = jnp.dot(yg_ref[...], wf_ref[...], preferred_element_type=F32)
    ya = jnp.dot(og_ref[...], wa_ref[...], preferred_element_type=F32)
    y = sgf_ref[...].astype(F32) * yf + sga_ref[...].astype(F32) * ya
    yo = jnp.dot(y.astype(BF16), wo_ref[...], preferred_element_type=F32)
    o_ref[...] = x_ref[...] + gate_ref[0] * yo


def _merge(yg, og, p_x, x2d, mod3, wf, wa, wo, *, seq, off_gf, off_ga, tm=256):
    m, d = x2d.shape
    tiles_per_seq = seq // tm
    const = lambda shape: pl.BlockSpec(shape, lambda i: (0, 0), pipeline_mode=pl.Buffered(1))
    return pl.pallas_call(
        _merge_kernel,
        grid=(m // tm,),
        in_specs=[
            pl.BlockSpec((tm, yg.shape[1]), lambda i: (i, 0)),
            pl.BlockSpec((tm, d), lambda i: (i, 0)),
            pl.BlockSpec((tm, d), lambda i: (i, off_gf // d)),
            pl.BlockSpec((tm, d), lambda i: (i, off_ga // d)),
            pl.BlockSpec((tm, d), lambda i: (i, 0)),
            pl.BlockSpec((1, 1, d), lambda i: (i // tiles_per_seq, 0, 2)),
            const(wf.shape), const(wa.shape), const(wo.shape),
        ],
        out_specs=pl.BlockSpec((tm, d), lambda i: (i, 0)),
        out_shape=jax.ShapeDtypeStruct((m, d), F32),
        compiler_params=pltpu.CompilerParams(
            dimension_semantics=("parallel",), vmem_limit_bytes=VMEM_LIMIT),
        name="merge",
    )(yg, og, p_x, p_x, x2d, mod3, wf, wa, wo)


def _rope_tables(seq, head_dim):
    n_freq = head_dim // 4
    t = np.arange(seq)
    pos = np.stack([t // GRID_W, t % GRID_W], axis=-1).astype(np.float32)
    inv_freq = (np.float32(ROPE_BASE) ** (-np.arange(n_freq, dtype=np.float32) / np.float32(n_freq)))
    ang = pos[:, :, None] * inv_freq.astype(np.float32)
    ang = np.broadcast_to(ang[:, :, None, :], (seq, 2, 2, n_freq)).astype(np.float64)
    sign = np.array([-1.0, 1.0])[None, None, :, None]
    return (np.cos(ang).reshape(seq, head_dim).astype(np.float32),
            (np.sin(ang) * sign).reshape(seq, head_dim).astype(np.float32))


def kernel(x, c, ctx, c_ctx, w_mod, b_mod, w_in, q_gain, k_gain, rpb, w_f_out, w_a_out, w_out):
    batch, seq, d = x.shape
    ctx_len = ctx.shape[1]
    depth, heads, n_dr, n_dc = rpb.shape
    assert depth == 1 and w_mod.shape[0] == 1
    head_dim = q_gain.shape[1]
    assert head_dim == LANES and seq % GRID_W == 0
    win_r, win_c = (n_dr + 1) // 2, (n_dc + 1) // 2
    attn_w = heads * head_dim
    fw = w_f_out.shape[1]
    gd = fw // F_GROUPS
    off_zf, off_q = fw, 2 * fw
    off_k, off_v, off_za = off_q + attn_w, off_q + 2 * attn_w, off_q + 3 * attn_w
    off_gf = off_za + attn_w
    off_ga = off_gf + d
    assert w_in.shape[2] == off_ga + d and fw == 1024 and attn_w == d

    c_all = jnp.concatenate([c, c_ctx[None, :], jnp.zeros((16 - batch - 1, d), F32)], axis=0)
    mod = _mod(c_all, w_mod[0], b_mod)
    mod3 = mod.reshape(16, 1, 3 * d)

    cos_np, sin_np = _rope_tables(seq, head_dim)
    cos_t, sin_t = jnp.asarray(cos_np), jnp.asarray(sin_np)
    w_bf = w_in[0].astype(BF16)
    x2d = x.reshape(batch * seq, d)
    tn = 1024
    kinds = {0: "raw", off_zf: "silu", off_q: "q", off_k: "k", off_v: "raw", off_za: "silu",
             off_gf: "sig", off_ga: "sig"}
    starts = sorted(kinds)
    types = [kinds[max(s for s in starts if s <= jj * tn)] for jj in range(w_in.shape[2] // tn)]
    q_scale = float(head_dim) ** -0.5
    p_x = _proj(x2d, mod3, w_bf, q_gain, k_gain, cos_t, sin_t, types=types, col0=0,
                rows_per_mod=seq, mod_row0=0, q_scale=q_scale)
    ctx_types = ["kc"] * (attn_w // tn) + ["raw"] * (attn_w // tn)
    p_c = _proj(ctx.reshape(batch * ctx_len, d), mod3, w_bf, q_gain, k_gain, cos_t, sin_t,
                types=ctx_types, col0=off_k // tn, rows_per_mod=batch * ctx_len, mod_row0=batch,
                q_scale=q_scale)

    rpb_pad = jnp.pad(rpb[0], ((0, 0), (0, 16 - n_dr), (0, LANES - n_dc)))
    og = _attention(p_x, p_c, rpb_pad, batch=batch, seq=seq, ctx_len=ctx_len, heads=heads,
                    off_q=off_q, off_k=off_k, off_v=off_v, off_z=off_za, win_r=win_r, win_c=win_c)

    cc, sc = _dft_mats(gd)
    cn, sn = _dft_mats(seq)
    csc = jnp.asarray(np.concatenate([cc, sc], axis=1).astype(np.float32)).astype(BF16)
    csn = jnp.asarray(np.concatenate([cn, -sn], axis=1).astype(np.float32)).astype(BF16)
    yg = _fourier(p_x, csc, csn, batch=batch, seq=seq, fw=fw, gd=gd)

    out = _merge(yg, og, p_x, x2d, mod3, w_f_out[0].astype(BF16), w_a_out[0].astype(BF16),
                 w_out[0].astype(BF16), seq=seq, off_gf=off_gf, off_ga=off_ga)
    return out.reshape(batch, seq, d)
```

```python
import functools

import numpy as np
import jax
import jax.numpy as jnp
from jax import lax
from jax.experimental import pallas as pl
from jax.experimental.pallas import tpu as pltpu

GRID_W = 64
F_GROUPS = 4
ROPE_BASE = 10000.0
EPS = 1e-6
NEG = -1e30
LANES = 128
VMEM_LIMIT = 56 * 1024 * 1024

BF16 = jnp.bfloat16
F32 = jnp.float32


def _nt_dot(a, b):
    return lax.dot_general(a, b, (((1,), (1,)), ((), ())), preferred_element_type=F32)


def _mod_kernel(c_ref, w_ref, b_ref, o_ref):
    a = jax.nn.silu(c_ref[...])
    o_ref[...] = jnp.dot(a, w_ref[...], preferred_element_type=F32,
                         precision=lax.Precision.HIGHEST) + b_ref[...]


def _mod(c_all, w_mod, b_mod, tn=512):
    m, d = c_all.shape
    n = w_mod.shape[1]
    return pl.pallas_call(
        _mod_kernel,
        grid=(n // tn,),
        in_specs=[pl.BlockSpec((m, d), lambda j: (0, 0)),
                  pl.BlockSpec((d, tn), lambda j: (0, j)),
                  pl.BlockSpec((1, tn), lambda j: (0, j))],
        out_specs=pl.BlockSpec((m, tn), lambda j: (0, j)),
        out_shape=jax.ShapeDtypeStruct((m, n), F32),
        name="mod",
    )(c_all, w_mod, b_mod)


def _hnorm_kernel(rc, x_ref, shift_ref, scale_ref, o_ref):
    def body(t, carry):
        r = pl.multiple_of(t * rc, rc)
        xs = x_ref[pl.ds(r, rc), :]
        ms = jnp.mean(xs * xs, axis=-1, keepdims=True)
        h = xs * lax.rsqrt(ms + EPS) * (1.0 + scale_ref[0]) + shift_ref[0]
        o_ref[pl.ds(r, rc), :] = h.astype(BF16)
        return carry
    lax.fori_loop(0, x_ref.shape[0] // rc, body, 0)


def _hnorm(x2d, mod3, *, rows_per_mod, mod_row0, tm=512, rc=64):
    m, d = x2d.shape
    tiles_per_mod = rows_per_mod // tm
    return pl.pallas_call(
        functools.partial(_hnorm_kernel, rc),
        grid=(m // tm,),
        in_specs=[
            pl.BlockSpec((tm, d), lambda i: (i, 0)),
            pl.BlockSpec((1, 1, d), lambda i: (mod_row0 + i // tiles_per_mod, 0, 0)),
            pl.BlockSpec((1, 1, d), lambda i: (mod_row0 + i // tiles_per_mod, 0, 1)),
        ],
        out_specs=pl.BlockSpec((tm, d), lambda i: (i, 0)),
        out_shape=jax.ShapeDtypeStruct((m, d), BF16),
        compiler_params=pltpu.CompilerParams(dimension_semantics=("parallel",)),
        name="hnorm",
    )(x2d, mod3, mod3)


def _proj_kernel(kind, mm, q_scale, h_ref, w_ref, gain_ref, cos_ref, sin_ref, o_ref):
    tm = h_ref.shape[0]
    tn = w_ref.shape[1]
    for r in range(0, tm, mm):
        a = jnp.dot(h_ref[r:r + mm, :], w_ref[...], preferred_element_type=F32)
        if kind == "raw":
            o_ref[r:r + mm, :] = a.astype(BF16)
        elif kind == "silu":
            o_ref[r:r + mm, :] = jax.nn.silu(a).astype(BF16)
        elif kind == "sig":
            o_ref[r:r + mm, :] = jax.nn.sigmoid(a).astype(BF16)
        else:
            gain = gain_ref[...]
            for hh in range(tn // LANES):
                xh = a[:, hh * LANES:(hh + 1) * LANES]
                ms = jnp.mean(xh * xh, axis=-1, keepdims=True)
                xn = xh * lax.rsqrt(ms + EPS) * gain
                if kind != "kc":
                    xn = xn * cos_ref[r:r + mm, :] + pltpu.roll(xn, LANES // 2, 1) * sin_ref[r:r + mm, :]
                if kind == "q":
                    xn = xn * q_scale
                o_ref[r:r + mm, hh * LANES:(hh + 1) * LANES] = xn.astype(BF16)


def _proj(h, w_bf, gain, cos_t, sin_t, *, kind, q_scale, tm=1024, tn=1024, mm=256):
    m, d = h.shape
    n = w_bf.shape[1]
    tiles_per_seq = cos_t.shape[0] // tm
    kern = functools.partial(_proj_kernel, kind, mm, q_scale)
    return pl.pallas_call(
        kern,
        grid=(m // tm, n // tn),
        in_specs=[
            pl.BlockSpec((tm, d), lambda i, j: (i, 0)),
            pl.BlockSpec((d, tn), lambda i, j: (0, j)),
            pl.BlockSpec((1, LANES), lambda i, j: (0, 0)),
            pl.BlockSpec((tm, LANES), lambda i, j: (i % tiles_per_seq, 0)),
            pl.BlockSpec((tm, LANES), lambda i, j: (i % tiles_per_seq, 0)),
        ],
        out_specs=pl.BlockSpec((tm, tn), lambda i, j: (i, j)),
        out_shape=jax.ShapeDtypeStruct((m, n), BF16),
        compiler_params=pltpu.CompilerParams(
            dimension_semantics=("parallel", "arbitrary"), vmem_limit_bytes=VMEM_LIMIT),
        name="proj_" + kind,
    )(h, w_bf, gain, cos_t, sin_t)


Q_ROWS = 2
BAND = 10
AHEAD = 2


def _band_start(r0, rows):
    return min(max(r0 - 4, 0), rows - BAND)


def _build_bias_tables(rpb_ref, bias_ref, rows, win_r, win_c):
    c_io = lax.broadcasted_iota(jnp.int32, (GRID_W, LANES), 0)
    l_io = lax.broadcasted_iota(jnp.int32, (GRID_W, LANES), 1)
    cs = jnp.clip(c_io - win_c // 2, 0, GRID_W - win_c)
    inwin = (l_io >= cs) & (l_io < cs + win_c) & (l_io < GRID_W)
    low = l_io < GRID_W
    neg = jnp.full((GRID_W, LANES), NEG, F32)
    toep = []
    for dr in range(2 * win_r - 1):
        row = jnp.broadcast_to(rpb_ref[0, dr:dr + 1, :], (GRID_W, LANES))
        t = pltpu.roll(row, LANES - (win_c - 1), 1, stride=1, stride_axis=0)
        toep.append(jnp.where(inwin, t, NEG))
    reps = [0, 2, 4, rows - 4, rows - 2]
    for tb, r0 in enumerate(reps):
        s0 = _band_start(r0, rows)
        assert (r0 - s0) == 2 * tb
        for rho in range(Q_ROWS):
            r = r0 + rho
            rs = min(max(r - win_r // 2, 0), rows - win_r)
            blocks = []
            for i in range(BAND):
                kr = s0 + i
                blocks.append(toep[kr - r + win_r - 1] if rs <= kr < rs + win_r else neg)
            for p in range(BAND // 2):
                tile = jnp.where(low, blocks[2 * p], pltpu.roll(blocks[2 * p + 1], GRID_W, 1))
                bias_ref[tb, rho * GRID_W:(rho + 1) * GRID_W, p * LANES:(p + 1) * LANES] = tile


def _attn_kernel(rows, win_r, win_c,
                 q_ref, k_ref, v_ref, z_ref, kc_ref, vc_ref, rpb_ref, o_ref, bias_ref):
    @pl.when(pl.program_id(1) == 0)
    def _():
        _build_bias_tables(rpb_ref, bias_ref, rows, win_r, win_c)

    nq = Q_ROWS * GRID_W
    nk = BAND * GRID_W
    kc = kc_ref[...]
    vc = vc_ref[...]

    def scores(r0):
        s0 = _band_start(r0, rows)
        tb = (r0 - s0) // 2
        q0, k0 = r0 * GRID_W, s0 * GRID_W
        qb = q_ref[q0:q0 + nq, :]
        return _nt_dot(qb, k_ref[k0:k0 + nk, :]) + bias_ref[tb], _nt_dot(qb, kc)

    def finish(r0, s_loc, s_ctx):
        q0, k0 = r0 * GRID_W, _band_start(r0, rows) * GRID_W
        m = jnp.maximum(jnp.max(s_loc, axis=-1, keepdims=True), jnp.max(s_ctx, axis=-1, keepdims=True))
        p_loc = jnp.exp(s_loc - m)
        p_ctx = jnp.exp(s_ctx - m)
        den = jnp.sum(p_loc, axis=-1, keepdims=True) + jnp.sum(p_ctx, axis=-1, keepdims=True)
        o = (jnp.dot(p_loc.astype(BF16), v_ref[k0:k0 + nk, :], preferred_element_type=F32)
             + jnp.dot(p_ctx.astype(BF16), vc, preferred_element_type=F32))
        og = (o / den) * z_ref[q0:q0 + nq, :].astype(F32)
        o_ref[q0:q0 + nq, :] = og.astype(BF16)

    starts = list(range(0, rows, Q_ROWS))
    pending = [scores(r0) for r0 in starts[:AHEAD]]
    for i, r0 in enumerate(starts):
        if i + AHEAD < len(starts):
            pending.append(scores(starts[i + AHEAD]))
        finish(r0, *pending.pop(0))


def _attention(q, k, v, z, kc, vc, rpb_pad, *, batch, seq, ctx_len, heads, win_r, win_c):
    rows = seq // GRID_W
    kern = functools.partial(_attn_kernel, rows, win_r, win_c)
    tok = pl.BlockSpec((seq, LANES), lambda h, b: (b, h))
    ctx = pl.BlockSpec((ctx_len, LANES), lambda h, b: (b, h))
    return pl.pallas_call(
        kern,
        grid=(heads, batch),
        in_specs=[tok, tok, tok, tok, ctx, ctx,
                  pl.BlockSpec((1,) + rpb_pad.shape[1:], lambda h, b: (h, 0, 0))],
        out_specs=tok,
        out_shape=jax.ShapeDtypeStruct((batch * seq, heads * LANES), BF16),
        scratch_shapes=[pltpu.VMEM((5, Q_ROWS * GRID_W, BAND * GRID_W), F32)],
        compiler_params=pltpu.CompilerParams(
            dimension_semantics=("arbitrary", "arbitrary"), vmem_limit_bytes=VMEM_LIMIT),
        name="attn",
    )(q, k, v, z, kc, vc, rpb_pad)


def _dft_mats(n):
    jk = (np.arange(n)[:, None] * np.arange(n)[None, :]) % n
    ang = 2.0 * np.pi * jk.astype(np.float64) / n
    return np.cos(ang) / np.sqrt(n), np.sin(ang) / np.sqrt(n)


def _fourier_kernel(gd, u_ref, zf_ref, csc_ref, csn_ref, o_ref, ab_ref):
    seq = u_ref.shape[0]

    @pl.when(pl.program_id(1) == 0)
    def _():
        rb = 512
        for g in range(u_ref.shape[1] // gd):
            for r in range(0, seq, rb):
                t = jnp.dot(u_ref[r:r + rb, g * gd:(g + 1) * gd], csc_ref[...], preferred_element_type=F32)
                ab_ref[r:r + rb, g * gd:(g + 1) * gd] = t[:, :gd].astype(BF16)
                ab_ref[seq + r:seq + r + rb, g * gd:(g + 1) * gd] = t[:, gd:].astype(BF16)

    y = jnp.dot(csn_ref[...], ab_ref[...], preferred_element_type=F32)
    o_ref[...] = (y * zf_ref[...].astype(F32)).astype(BF16)


def _fourier(u, zf, csc, csn, *, batch, seq, gd, tk=512):
    fw = u.shape[1]
    steps = seq // tk
    kern = functools.partial(_fourier_kernel, gd)
    return pl.pallas_call(
        kern,
        grid=(batch, steps),
        in_specs=[
            pl.BlockSpec((seq, fw), lambda b, k: (b, 0)),
            pl.BlockSpec((tk, fw), lambda b, k: (b * steps + k, 0)),
            pl.BlockSpec(csc.shape, lambda b, k: (0, 0)),
            pl.BlockSpec((tk, 2 * seq), lambda b, k: (k, 0)),
        ],
        out_specs=pl.BlockSpec((tk, fw), lambda b, k: (b * steps + k, 0)),
        out_shape=jax.ShapeDtypeStruct((batch * seq, fw), BF16),
        scratch_shapes=[pltpu.VMEM((2 * seq, fw), BF16)],
        compiler_params=pltpu.CompilerParams(
            dimension_semantics=("parallel", "arbitrary"), vmem_limit_bytes=VMEM_LIMIT),
        name="fourier",
    )(u, zf, csc, csn)


def _merge_kernel(yg_ref, og_ref, sgf_ref, sga_ref, x_ref, gate_ref, wf_ref, wa_ref, wo_ref, o_ref):
    yf = jnp.dot(yg_ref[...], wf_ref[...], preferred_element_type=F32)
    ya = jnp.dot(og_ref[...], wa_ref[...], preferred_element_type=F32)
    y = sgf_ref[...].astype(F32) * yf + sga_ref[...].astype(F32) * ya
    yo = jnp.dot(y.astype(BF16), wo_ref[...], preferred_element_type=F32)
    o_ref[...] = x_ref[...] + gate_ref[0] * yo


def _merge(yg, og, g, x2d, mod3, wf, wa, wo, *, seq, tm=256):
    m, d = x2d.shape
    tiles_per_seq = seq // tm
    const = lambda shape: pl.BlockSpec(shape, lambda i: (0, 0), pipeline_mode=pl.Buffered(1))
    return pl.pallas_call(
        _merge_kernel,
        grid=(m // tm,),
        in_specs=[
            pl.BlockSpec((tm, yg.shape[1]), lambda i: (i, 0)),
            pl.BlockSpec((tm, d), lambda i: (i, 0)),
            pl.BlockSpec((tm, d), lambda i: (i, 0)),
            pl.BlockSpec((tm, d), lambda i: (i, 1)),
            pl.BlockSpec((tm, d), lambda i: (i, 0)),
            pl.BlockSpec((1, 1, d), lambda i: (i // tiles_per_seq, 0, 2)),
            const(wf.shape), const(wa.shape), const(wo.shape),
        ],
        out_specs=pl.BlockSpec((tm, d), lambda i: (i, 0)),
        out_shape=jax.ShapeDtypeStruct((m, d), F32),
        compiler_params=pltpu.CompilerParams(
            dimension_semantics=("parallel",), vmem_limit_bytes=VMEM_LIMIT),
        name="merge",
    )(yg, og, g, g, x2d, mod3, wf, wa, wo)


def _rope_tables(seq, head_dim):
    n_freq = head_dim // 4
    t = np.arange(seq)
    pos = np.stack([t // GRID_W, t % GRID_W], axis=-1).astype(np.float32)
    inv_freq = (np.float32(ROPE_BASE) ** (-np.arange(n_freq, dtype=np.float32) / np.float32(n_freq)))
    ang = (pos[:, :, None] * inv_freq.astype(np.float32)).astype(np.float64)
    ang = np.broadcast_to(ang[:, None, :, :], (seq, 2, 2, n_freq))
    sign = np.array([-1.0, 1.0])[None, :, None, None]
    return (np.cos(ang).reshape(seq, head_dim).astype(np.float32),
            (np.sin(ang) * sign).reshape(seq, head_dim).astype(np.float32))


def _rope_lane_order(a, n_freq):
    lead = a.shape[:-1]
    return a.reshape(lead + (-1, 2, 2, n_freq)).swapaxes(-3, -2).reshape(a.shape)


def kernel(x, c, ctx, c_ctx, w_mod, b_mod, w_in, q_gain, k_gain, rpb, w_f_out, w_a_out, w_out):
    batch, seq, d = x.shape
    ctx_len = ctx.shape[1]
    depth, heads, n_dr, n_dc = rpb.shape
    assert depth == 1 and w_mod.shape[0] == 1
    head_dim = q_gain.shape[1]
    assert head_dim == LANES and seq % GRID_W == 0
    win_r, win_c = (n_dr + 1) // 2, (n_dc + 1) // 2
    attn_w = heads * head_dim
    fw = w_f_out.shape[1]
    gd = fw // F_GROUPS
    off_zf, off_q = fw, 2 * fw
    off_k, off_v, off_za = off_q + attn_w, off_q + 2 * attn_w, off_q + 3 * attn_w
    off_gf = off_za + attn_w
    off_ga = off_gf + d
    assert w_in.shape[2] == off_ga + d

    c_all = jnp.concatenate([c, c_ctx[None, :], jnp.zeros((16 - batch - 1, d), F32)], axis=0)
    mod = _mod(c_all, w_mod[0], b_mod)
    mod3 = mod.reshape(16, 1, 3 * d)

    x2d = x.reshape(batch * seq, d)
    h_x = _hnorm(x2d, mod3, rows_per_mod=seq, mod_row0=0)
    h_c = _hnorm(ctx.reshape(batch * ctx_len, d), mod3, rows_per_mod=batch * ctx_len, mod_row0=batch)

    n_freq = head_dim // 4
    cos_np, sin_np = _rope_tables(seq, head_dim)
    cos_t, sin_t = jnp.asarray(cos_np), jnp.asarray(sin_np)
    w = w_in[0]
    col = lambda a, b: w[:, a:b].astype(BF16)
    rope_col = lambda a, b: _rope_lane_order(w[:, a:b], n_freq).astype(BF16)
    w_q, w_k = rope_col(off_q, off_k), rope_col(off_k, off_v)
    qg = _rope_lane_order(q_gain, n_freq)
    kg = _rope_lane_order(k_gain, n_freq)
    q_scale = float(head_dim) ** -0.5
    proj = functools.partial(_proj, cos_t=cos_t, sin_t=sin_t, q_scale=q_scale)

    u_f = proj(h_x, col(0, off_zf), qg, kind="raw")
    z_f = proj(h_x, col(off_zf, off_q), qg, kind="silu")
    q = proj(h_x, w_q, qg, kind="q")
    k = proj(h_x, w_k, kg, kind="k")
    v = proj(h_x, col(off_v, off_za), qg, kind="raw")
    z_a = proj(h_x, col(off_za, off_gf), qg, kind="silu")
    g = proj(h_x, col(off_gf, off_ga + d), qg, kind="sig")
    k_c = proj(h_c, w_k, kg, kind="kc")
    v_c = proj(h_c, col(off_v, off_za), qg, kind="raw")

    rpb_pad = jnp.pad(rpb[0], ((0, 0), (0, 16 - n_dr), (0, LANES - n_dc)))
    og = _attention(q, k, v, z_a, k_c, v_c, rpb_pad, batch=batch, seq=seq, ctx_len=ctx_len, heads=heads,
                    win_r=win_r, win_c=win_c)

    cc, sc = _dft_mats(gd)
    cn, sn = _dft_mats(seq)
    csc = jnp.asarray(np.concatenate([cc, sc], axis=1).astype(np.float32)).astype(BF16)
    csn = jnp.asarray(np.concatenate([cn, -sn], axis=1).astype(np.float32)).astype(BF16)
    yg = _fourier(u_f, z_f, csc, csn, batch=batch, seq=seq, gd=gd)

    out = _merge(yg, og, g, x2d, mod3, w_f_out[0].astype(BF16), w_a_out[0].astype(BF16),
                 w_out[0].astype(BF16), seq=seq)
    return out.reshape(batch, seq, d)
```

```python
import functools

import numpy as np
import jax
import jax.numpy as jnp
from jax import lax
from jax.experimental import pallas as pl
from jax.experimental.pallas import tpu as pltpu

GRID_W = 64
F_GROUPS = 4
ROPE_BASE = 10000.0
EPS = 1e-6
NEG = -1e30
LANES = 128
VMEM_LIMIT = 56 * 1024 * 1024

BF16 = jnp.bfloat16
F32 = jnp.float32


def _nt_dot(a, b):
    return lax.dot_general(a, b, (((1,), (1,)), ((), ())), preferred_element_type=F32)


def _mod_kernel(c_ref, w_ref, b_ref, o_ref):
    a = jax.nn.silu(c_ref[...])
    o_ref[...] = jnp.dot(a, w_ref[...], preferred_element_type=F32,
                         precision=lax.Precision.HIGHEST) + b_ref[...]


def _mod(c_all, w_mod, b_mod, tn=512):
    m, d = c_all.shape
    n = w_mod.shape[1]
    return pl.pallas_call(
        _mod_kernel,
        grid=(n // tn,),
        in_specs=[pl.BlockSpec((m, d), lambda j: (0, 0)),
                  pl.BlockSpec((d, tn), lambda j: (0, j)),
                  pl.BlockSpec((1, tn), lambda j: (0, j))],
        out_specs=pl.BlockSpec((m, tn), lambda j: (0, j)),
        out_shape=jax.ShapeDtypeStruct((m, n), F32),
        name="mod",
    )(c_all, w_mod, b_mod)


def _hnorm_kernel(rc, x_ref, shift_ref, scale_ref, o_ref):
    def body(t, carry):
        r = pl.multiple_of(t * rc, rc)
        xs = x_ref[pl.ds(r, rc), :]
        ms = jnp.mean(xs * xs, axis=-1, keepdims=True)
        h = xs * lax.rsqrt(ms + EPS) * (1.0 + scale_ref[0]) + shift_ref[0]
        o_ref[pl.ds(r, rc), :] = h.astype(BF16)
        return carry
    lax.fori_loop(0, x_ref.shape[0] // rc, body, 0)


def _hnorm(x2d, mod3, *, rows_per_mod, mod_row0, tm=512, rc=64):
    m, d = x2d.shape
    tiles_per_mod = rows_per_mod // tm
    return pl.pallas_call(
        functools.partial(_hnorm_kernel, rc),
        grid=(m // tm,),
        in_specs=[
            pl.BlockSpec((tm, d), lambda i: (i, 0)),
            pl.BlockSpec((1, 1, d), lambda i: (mod_row0 + i // tiles_per_mod, 0, 0)),
            pl.BlockSpec((1, 1, d), lambda i: (mod_row0 + i // tiles_per_mod, 0, 1)),
        ],
        out_specs=pl.BlockSpec((tm, d), lambda i: (i, 0)),
        out_shape=jax.ShapeDtypeStruct((m, d), BF16),
        compiler_params=pltpu.CompilerParams(dimension_semantics=("parallel",)),
        name="hnorm",
    )(x2d, mod3, mod3)


def _rope_lane_order_cols(w):
    n = w.shape[1]
    quarter = (lax.broadcasted_iota(jnp.int32, w.shape, 1) % LANES) // (LANES // 4)
    up = pltpu.roll(w, n - LANES // 4, 1)
    down = pltpu.roll(w, LANES // 4, 1)
    return jnp.where(quarter == 1, up, jnp.where(quarter == 2, down, w))


def _proj_kernel(kind, mm, rc, q_scale, h_ref, w_ref, gain_ref, cos_ref, sin_ref, o_ref, wb_ref):
    tm = h_ref.shape[0]
    tn = w_ref.shape[1]

    @pl.when(pl.program_id(1) == 0)
    def _():
        def cast_body(t, carry):
            r = pl.multiple_of(t * rc, rc)
            wt = w_ref[pl.ds(r, rc), :]
            if kind in ("q", "k", "kc"):
                wt = _rope_lane_order_cols(wt)
            wb_ref[pl.ds(r, rc), :] = wt.astype(BF16)
            return carry
        lax.fori_loop(0, w_ref.shape[0] // rc, cast_body, 0)

    for r in range(0, tm, mm):
        a = jnp.dot(h_ref[r:r + mm, :], wb_ref[...], preferred_element_type=F32)
        if kind == "raw":
            o_ref[r:r + mm, :] = a.astype(BF16)
        elif kind == "silu":
            o_ref[r:r + mm, :] = jax.nn.silu(a).astype(BF16)
        elif kind == "sig":
            o_ref[r:r + mm, :] = jax.nn.sigmoid(a).astype(BF16)
        else:
            gain = gain_ref[...]
            for hh in range(tn // LANES):
                xh = a[:, hh * LANES:(hh + 1) * LANES]
                ms = jnp.mean(xh * xh, axis=-1, keepdims=True)
                xn = xh * lax.rsqrt(ms + EPS) * gain
                if kind != "kc":
                    xn = xn * cos_ref[r:r + mm, :] + pltpu.roll(xn, LANES // 2, 1) * sin_ref[r:r + mm, :]
                if kind == "q":
                    xn = xn * q_scale
                o_ref[r:r + mm, hh * LANES:(hh + 1) * LANES] = xn.astype(BF16)


def _proj(h, w, gain, cos_t, sin_t, *, kind, col0, n, q_scale, tm=1024, tn=1024, mm=256, rc=64):
    m, d = h.shape
    tiles_per_seq = cos_t.shape[0] // tm
    j0 = col0 // tn
    kern = functools.partial(_proj_kernel, kind, mm, rc, q_scale)
    return pl.pallas_call(
        kern,
        grid=(n // tn, m // tm),
        in_specs=[
            pl.BlockSpec((tm, d), lambda j, i: (i, 0)),
            pl.BlockSpec((d, tn), lambda j, i: (0, j0 + j)),
            pl.BlockSpec((1, LANES), lambda j, i: (0, 0)),
            pl.BlockSpec((tm, LANES), lambda j, i: (i % tiles_per_seq, 0)),
            pl.BlockSpec((tm, LANES), lambda j, i: (i % tiles_per_seq, 0)),
        ],
        out_specs=pl.BlockSpec((tm, tn), lambda j, i: (i, j)),
        out_shape=jax.ShapeDtypeStruct((m, n), BF16),
        scratch_shapes=[pltpu.VMEM((d, tn), BF16)],
        compiler_params=pltpu.CompilerParams(
            dimension_semantics=("arbitrary", "arbitrary"), vmem_limit_bytes=VMEM_LIMIT),
        name="proj_" + kind,
    )(h, w, gain, cos_t, sin_t)


Q_ROWS = 2
BAND = 10
AHEAD = 2


def _band_start(r0, rows):
    return min(max(r0 - 4, 0), rows - BAND)


def _build_bias_tables(rpb_ref, bias_ref, rows, win_r, win_c):
    c_io = lax.broadcasted_iota(jnp.int32, (GRID_W, LANES), 0)
    l_io = lax.broadcasted_iota(jnp.int32, (GRID_W, LANES), 1)
    cs = jnp.clip(c_io - win_c // 2, 0, GRID_W - win_c)
    inwin = (l_io >= cs) & (l_io < cs + win_c) & (l_io < GRID_W)
    low = l_io < GRID_W
    neg = jnp.full((GRID_W, LANES), NEG, F32)
    toep = []
    for dr in range(2 * win_r - 1):
        row = jnp.broadcast_to(rpb_ref[0, dr:dr + 1, :], (GRID_W, LANES))
        t = pltpu.roll(row, LANES - (win_c - 1), 1, stride=1, stride_axis=0)
        toep.append(jnp.where(inwin, t, NEG))
    reps = [0, 2, 4, rows - 4, rows - 2]
    for tb, r0 in enumerate(reps):
        s0 = _band_start(r0, rows)
        assert (r0 - s0) == 2 * tb
        for rho in range(Q_ROWS):
            r = r0 + rho
            rs = min(max(r - win_r // 2, 0), rows - win_r)
            blocks = []
            for i in range(BAND):
                kr = s0 + i
                blocks.append(toep[kr - r + win_r - 1] if rs <= kr < rs + win_r else neg)
            for p in range(BAND // 2):
                tile = jnp.where(low, blocks[2 * p], pltpu.roll(blocks[2 * p + 1], GRID_W, 1))
                bias_ref[tb, rho * GRID_W:(rho + 1) * GRID_W, p * LANES:(p + 1) * LANES] = tile


def _attn_kernel(rows, win_r, win_c,
                 q_ref, k_ref, v_ref, z_ref, kc_ref, vc_ref, rpb_ref, o_ref, bias_ref):
    @pl.when(pl.program_id(1) == 0)
    def _():
        _build_bias_tables(rpb_ref, bias_ref, rows, win_r, win_c)

    nq = Q_ROWS * GRID_W
    nk = BAND * GRID_W
    kc = kc_ref[...]
    vc = vc_ref[...]

    def scores(r0):
        s0 = _band_start(r0, rows)
        tb = (r0 - s0) // 2
        q0, k0 = r0 * GRID_W, s0 * GRID_W
        qb = q_ref[q0:q0 + nq, :]
        return _nt_dot(qb, k_ref[k0:k0 + nk, :]) + bias_ref[tb], _nt_dot(qb, kc)

    def finish(r0, s_loc, s_ctx):
        q0, k0 = r0 * GRID_W, _band_start(r0, rows) * GRID_W
        m = jnp.maximum(jnp.max(s_loc, axis=-1, keepdims=True), jnp.max(s_ctx, axis=-1, keepdims=True))
        p_loc = jnp.exp(s_loc - m)
        p_ctx = jnp.exp(s_ctx - m)
        den = jnp.sum(p_loc, axis=-1, keepdims=True) + jnp.sum(p_ctx, axis=-1, keepdims=True)
        o = (jnp.dot(p_loc.astype(BF16), v_ref[k0:k0 + nk, :], preferred_element_type=F32)
             + jnp.dot(p_ctx.astype(BF16), vc, preferred_element_type=F32))
        og = (o / den) * z_ref[q0:q0 + nq, :].astype(F32)
        o_ref[q0:q0 + nq, :] = og.astype(BF16)

    starts = list(range(0, rows, Q_ROWS))
    pending = [scores(r0) for r0 in starts[:AHEAD]]
    for i, r0 in enumerate(starts):
        if i + AHEAD < len(starts):
            pending.append(scores(starts[i + AHEAD]))
        finish(r0, *pending.pop(0))


def _attention(q, k, v, z, kc, vc, rpb_pad, *, batch, seq, ctx_len, heads, win_r, win_c):
    rows = seq // GRID_W
    kern = functools.partial(_attn_kernel, rows, win_r, win_c)
    tok = pl.BlockSpec((seq, LANES), lambda h, b: (b, h))
    ctx = pl.BlockSpec((ctx_len, LANES), lambda h, b: (b, h))
    return pl.pallas_call(
        kern,
        grid=(heads, batch),
        in_specs=[tok, tok, tok, tok, ctx, ctx,
                  pl.BlockSpec((1,) + rpb_pad.shape[1:], lambda h, b: (h, 0, 0))],
        out_specs=tok,
        out_shape=jax.ShapeDtypeStruct((batch * seq, heads * LANES), BF16),
        scratch_shapes=[pltpu.VMEM((5, Q_ROWS * GRID_W, BAND * GRID_W), F32)],
        compiler_params=pltpu.CompilerParams(
            dimension_semantics=("arbitrary", "arbitrary"), vmem_limit_bytes=VMEM_LIMIT),
        name="attn",
    )(q, k, v, z, kc, vc, rpb_pad)


def _dft_mats(n):
    jk = (np.arange(n)[:, None] * np.arange(n)[None, :]) % n
    ang = 2.0 * np.pi * jk.astype(np.float64) / n
    return np.cos(ang) / np.sqrt(n), np.sin(ang) / np.sqrt(n)


def _fourier_kernel(gd, u_ref, zf_ref, csc_ref, csn_ref, o_ref, ab_ref):
    seq = u_ref.shape[0]

    @pl.when(pl.program_id(1) == 0)
    def _():
        rb = 512
        for g in range(u_ref.shape[1] // gd):
            for r in range(0, seq, rb):
                t = jnp.dot(u_ref[r:r + rb, g * gd:(g + 1) * gd], csc_ref[...], preferred_element_type=F32)
                ab_ref[r:r + rb, g * gd:(g + 1) * gd] = t[:, :gd].astype(BF16)
                ab_ref[seq + r:seq + r + rb, g * gd:(g + 1) * gd] = t[:, gd:].astype(BF16)

    y = jnp.dot(csn_ref[...], ab_ref[...], preferred_element_type=F32)
    o_ref[...] = (y * zf_ref[...].astype(F32)).astype(BF16)


def _fourier(u, zf, csc, csn, *, batch, seq, gd, tk=512):
    fw = u.shape[1]
    steps = seq // tk
    kern = functools.partial(_fourier_kernel, gd)
    return pl.pallas_call(
        kern,
        grid=(batch, steps),
        in_specs=[
            pl.BlockSpec((seq, fw), lambda b, k: (b, 0)),
            pl.BlockSpec((tk, fw), lambda b, k: (b * steps + k, 0)),
            pl.BlockSpec(csc.shape, lambda b, k: (0, 0)),
            pl.BlockSpec((tk, 2 * seq), lambda b, k: (k, 0)),
        ],
        out_specs=pl.BlockSpec((tk, fw), lambda b, k: (b * steps + k, 0)),
        out_shape=jax.ShapeDtypeStruct((batch * seq, fw), BF16),
        scratch_shapes=[pltpu.VMEM((2 * seq, fw), BF16)],
        compiler_params=pltpu.CompilerParams(
            dimension_semantics=("parallel", "arbitrary"), vmem_limit_bytes=VMEM_LIMIT),
        name="fourier",
    )(u, zf, csc, csn)


def _merge_kernel(yg_ref, og_ref, sgf_ref, sga_ref, x_ref, gate_ref, wf_ref, wa_ref, wo_ref, o_ref):
    yf = jnp.dot(yg_ref[...], wf_ref[...], preferred_element_type=F32)
    ya = jnp.dot(og_ref[...], wa_ref[...], preferred_element_type=F32)
    y = sgf_ref[...].astype(F32) * yf + sga_ref[...].astype(F32) * ya
    yo = jnp.dot(y.astype(BF16), wo_ref[...], preferred_element_type=F32)
    o_ref[...] = x_ref[...] + gate_ref[0] * yo


def _merge(yg, og, g, x2d, mod3, wf, wa, wo, *, seq, tm=256):
    m, d = x2d.shape
    tiles_per_seq = seq // tm
    const = lambda shape: pl.BlockSpec(shape, lambda i: (0, 0), pipeline_mode=pl.Buffered(1))
    return pl.pallas_call(
        _merge_kernel,
        grid=(m // tm,),
        in_specs=[
            pl.BlockSpec((tm, yg.shape[1]), lambda i: (i, 0)),
            pl.BlockSpec((tm, d), lambda i: (i, 0)),
            pl.BlockSpec((tm, d), lambda i: (i, 0)),
            pl.BlockSpec((tm, d), lambda i: (i, 1)),
            pl.BlockSpec((tm, d), lambda i: (i, 0)),
            pl.BlockSpec((1, 1, d), lambda i: (i // tiles_per_seq, 0, 2)),
            const(wf.shape), const(wa.shape), const(wo.shape),
        ],
        out_specs=pl.BlockSpec((tm, d), lambda i: (i, 0)),
        out_shape=jax.ShapeDtypeStruct((m, d), F32),
        compiler_params=pltpu.CompilerParams(
            dimension_semantics=("parallel",), vmem_limit_bytes=VMEM_LIMIT),
        name="merge",
    )(yg, og, g, g, x2d, mod3, wf, wa, wo)


def _rope_tables(seq, head_dim):
    n_freq = head_dim // 4
    t = np.arange(seq)
    pos = np.stack([t // GRID_W, t % GRID_W], axis=-1).astype(np.float32)
    inv_freq = (np.float32(ROPE_BASE) ** (-np.arange(n_freq, dtype=np.float32) / np.float32(n_freq)))
    ang = (pos[:, :, None] * inv_freq.astype(np.float32)).astype(np.float64)
    ang = np.broadcast_to(ang[:, None, :, :], (seq, 2, 2, n_freq))
    sign = np.array([-1.0, 1.0])[None, :, None, None]
    return (np.cos(ang).reshape(seq, head_dim).astype(np.float32),
            (np.sin(ang) * sign).reshape(seq, head_dim).astype(np.float32))


def _rope_lane_order(a, n_freq):
    lead = a.shape[:-1]
    return a.reshape(lead + (-1, 2, 2, n_freq)).swapaxes(-3, -2).reshape(a.shape)


def kernel(x, c, ctx, c_ctx, w_mod, b_mod, w_in, q_gain, k_gain, rpb, w_f_out, w_a_out, w_out):
    batch, seq, d = x.shape
    ctx_len = ctx.shape[1]
    depth, heads, n_dr, n_dc = rpb.shape
    assert depth == 1 and w_mod.shape[0] == 1
    head_dim = q_gain.shape[1]
    assert head_dim == LANES and seq % GRID_W == 0
    win_r, win_c = (n_dr + 1) // 2, (n_dc + 1) // 2
    attn_w = heads * head_dim
    fw = w_f_out.shape[1]
    gd = fw // F_GROUPS
    off_zf, off_q = fw, 2 * fw
    off_k, off_v, off_za = off_q + attn_w, off_q + 2 * attn_w, off_q + 3 * attn_w
    off_gf = off_za + attn_w
    off_ga = off_gf + d
    assert w_in.shape[2] == off_ga + d

    c_all = jnp.concatenate([c, c_ctx[None, :], jnp.zeros((16 - batch - 1, d), F32)], axis=0)
    mod = _mod(c_all, w_mod[0], b_mod)
    mod3 = mod.reshape(16, 1, 3 * d)

    x2d = x.reshape(batch * seq, d)
    h_x = _hnorm(x2d, mod3, rows_per_mod=seq, mod_row0=0)
    h_c = _hnorm(ctx.reshape(batch * ctx_len, d), mod3, rows_per_mod=batch * ctx_len, mod_row0=batch)

    n_freq = head_dim // 4
    cos_np, sin_np = _rope_tables(seq, head_dim)
    cos_t, sin_t = jnp.asarray(cos_np), jnp.asarray(sin_np)
    qg = _rope_lane_order(q_gain, n_freq)
    kg = _rope_lane_order(k_gain, n_freq)
    q_scale = float(head_dim) ** -0.5
    proj = functools.partial(_proj, w=w_in[0], cos_t=cos_t, sin_t=sin_t, q_scale=q_scale)

    u_f = proj(h_x, gain=qg, kind="raw", col0=0, n=fw)
    z_f = proj(h_x, gain=qg, kind="silu", col0=off_zf, n=fw)
    q = proj(h_x, gain=qg, kind="q", col0=off_q, n=attn_w)
    k = proj(h_x, gain=kg, kind="k", col0=off_k, n=attn_w)
    v = proj(h_x, gain=qg, kind="raw", col0=off_v, n=attn_w)
    z_a = proj(h_x, gain=qg, kind="silu", col0=off_za, n=attn_w)
    g = proj(h_x, gain=qg, kind="sig", col0=off_gf, n=2 * d)
    k_c = proj(h_c, gain=kg, kind="kc", col0=off_k, n=attn_w)
    v_c = proj(h_c, gain=qg, kind="raw", col0=off_v, n=attn_w)

    rpb_pad = jnp.pad(rpb[0], ((0, 0), (0, 16 - n_dr), (0, LANES - n_dc)))
    og = _attention(q, k, v, z_a, k_c, v_c, rpb_pad, batch=batch, seq=seq, ctx_len=ctx_len, heads=heads,
                    win_r=win_r, win_c=win_c)

    cc, sc = _dft_mats(gd)
    cn, sn = _dft_mats(seq)
    csc = jnp.asarray(np.concatenate([cc, sc], axis=1).astype(np.float32)).astype(BF16)
    csn = jnp.asarray(np.concatenate([cn, -sn], axis=1).astype(np.float32)).astype(BF16)
    yg = _fourier(u_f, z_f, csc, csn, batch=batch, seq=seq, gd=gd)

    out = _merge(yg, og, g, x2d, mod3, w_f_out[0].astype(BF16), w_a_out[0].astype(BF16),
                 w_out[0].astype(BF16), seq=seq)
    return out.reshape(batch, seq, d)
```

```python
import functools

import numpy as np
import jax
import jax.numpy as jnp
from jax import lax
from jax.experimental import pallas as pl
from jax.experimental.pallas import tpu as pltpu

GRID_W = 64
F_GROUPS = 4
ROPE_BASE = 10000.0
EPS = 1e-6
NEG = -1e30
LOG2E = 1.4426950408889634
LANES = 128
VMEM_LIMIT = 56 * 1024 * 1024

BF16 = jnp.bfloat16
F32 = jnp.float32


def _nt_dot(a, b):
    return lax.dot_general(a, b, (((1,), (1,)), ((), ())), preferred_element_type=F32)


def _mod_kernel(c_ref, w_ref, b_ref, o_ref):
    a = jax.nn.silu(c_ref[...])
    o_ref[...] = jnp.dot(a, w_ref[...], preferred_element_type=F32,
                         precision=lax.Precision.HIGHEST) + b_ref[...]


def _mod(c_all, w_mod, b_mod, tn=512):
    m, d = c_all.shape
    n = w_mod.shape[1]
    return pl.pallas_call(
        _mod_kernel,
        grid=(n // tn,),
        in_specs=[pl.BlockSpec((m, d), lambda j: (0, 0)),
                  pl.BlockSpec((d, tn), lambda j: (0, j)),
                  pl.BlockSpec((1, tn), lambda j: (0, j))],
        out_specs=pl.BlockSpec((m, tn), lambda j: (0, j)),
        out_shape=jax.ShapeDtypeStruct((m, n), F32),
        name="mod",
    )(c_all, w_mod, b_mod)


def _hnorm_kernel(rc, x_ref, shift_ref, scale_ref, o_ref):
    def body(t, carry):
        r = pl.multiple_of(t * rc, rc)
        xs = x_ref[pl.ds(r, rc), :]
        ms = jnp.mean(xs * xs, axis=-1, keepdims=True)
        h = xs * lax.rsqrt(ms + EPS) * (1.0 + scale_ref[0]) + shift_ref[0]
        o_ref[pl.ds(r, rc), :] = h.astype(BF16)
        return carry
    lax.fori_loop(0, x_ref.shape[0] // rc, body, 0)


def _hnorm(x2d, mod3, *, rows_per_mod, mod_row0, tm=512, rc=64):
    m, d = x2d.shape
    tiles_per_mod = rows_per_mod // tm
    return pl.pallas_call(
        functools.partial(_hnorm_kernel, rc),
        grid=(m // tm,),
        in_specs=[
            pl.BlockSpec((tm, d), lambda i: (i, 0)),
            pl.BlockSpec((1, 1, d), lambda i: (mod_row0 + i // tiles_per_mod, 0, 0)),
            pl.BlockSpec((1, 1, d), lambda i: (mod_row0 + i // tiles_per_mod, 0, 1)),
        ],
        out_specs=pl.BlockSpec((tm, d), lambda i: (i, 0)),
        out_shape=jax.ShapeDtypeStruct((m, d), BF16),
        compiler_params=pltpu.CompilerParams(dimension_semantics=("parallel",)),
        name="hnorm",
    )(x2d, mod3, mod3)


def _rope_lane_order_cols(w):
    n = w.shape[1]
    quarter = (lax.broadcasted_iota(jnp.int32, w.shape, 1) % LANES) // (LANES // 4)
    up = pltpu.roll(w, n - LANES // 4, 1)
    down = pltpu.roll(w, LANES // 4, 1)
    return jnp.where(quarter == 1, up, jnp.where(quarter == 2, down, w))


def _proj_kernel(kind, mm, rc, q_scale, h_ref, w_ref, gain_ref, cos_ref, sin_ref, o_ref, wb_ref):
    tm = h_ref.shape[0]
    tn = w_ref.shape[1]

    @pl.when(pl.program_id(1) == 0)
    def _():
        _cast_weight_tile(w_ref, wb_ref, rc, kind in ("q", "k", "kc"))

    for r in range(0, tm, mm):
        a = jnp.dot(h_ref[r:r + mm, :], wb_ref[...], preferred_element_type=F32)
        if kind == "raw":
            o_ref[r:r + mm, :] = a.astype(BF16)
        elif kind == "silu":
            o_ref[r:r + mm, :] = jax.nn.silu(a).astype(BF16)
        elif kind == "sig":
            o_ref[r:r + mm, :] = jax.nn.sigmoid(a).astype(BF16)
        else:
            gain = gain_ref[...]
            for hh in range(tn // LANES):
                xh = a[:, hh * LANES:(hh + 1) * LANES]
                ms = jnp.mean(xh * xh, axis=-1, keepdims=True)
                xn = xh * lax.rsqrt(ms + EPS) * gain
                if kind != "kc":
                    xn = xn * cos_ref[r:r + mm, :] + pltpu.roll(xn, LANES // 2, 1) * sin_ref[r:r + mm, :]
                if kind == "q":
                    xn = xn * q_scale
                o_ref[r:r + mm, hh * LANES:(hh + 1) * LANES] = xn.astype(BF16)


def _cast_weight_tile(w_ref, wb_ref, rc, reorder):
    def body(t, carry):
        r = pl.multiple_of(t * rc, rc)
        wt = w_ref[pl.ds(r, rc), :]
        if reorder:
            wt = _rope_lane_order_cols(wt)
        wb_ref[pl.ds(r, rc), :] = wt.astype(BF16)
        return carry
    lax.fori_loop(0, w_ref.shape[0] // rc, body, 0)


def _proj_x_kernel(mm, rc, x_ref, shift_ref, scale_ref, w_ref, o_ref, h_ref, wb_ref):
    @pl.when(pl.program_id(1) == 0)
    def _():
        _cast_weight_tile(w_ref, wb_ref, rc, False)

    for r in range(0, x_ref.shape[0], mm):
        xs = x_ref[r:r + mm, :]
        ms = jnp.mean(xs * xs, axis=-1, keepdims=True)
        h = (xs * lax.rsqrt(ms + EPS) * (1.0 + scale_ref[0]) + shift_ref[0]).astype(BF16)
        h_ref[r:r + mm, :] = h
        o_ref[r:r + mm, :] = jnp.dot(h, wb_ref[...], preferred_element_type=F32).astype(BF16)


def _proj_x(x2d, mod3, w, *, n, rows_per_mod, tm=1024, tn=1024, mm=256, rc=64):
    m, d = x2d.shape
    assert n == tn
    tiles_per_mod = rows_per_mod // tm
    return pl.pallas_call(
        functools.partial(_proj_x_kernel, mm, rc),
        grid=(n // tn, m // tm),
        in_specs=[
            pl.BlockSpec((tm, d), lambda j, i: (i, 0)),
            pl.BlockSpec((1, 1, d), lambda j, i: (i // tiles_per_mod, 0, 0)),
            pl.BlockSpec((1, 1, d), lambda j, i: (i // tiles_per_mod, 0, 1)),
            pl.BlockSpec((d, tn), lambda j, i: (0, j)),
        ],
        out_specs=[pl.BlockSpec((tm, tn), lambda j, i: (i, j)),
                   pl.BlockSpec((tm, d), lambda j, i: (i, 0))],
        out_shape=[jax.ShapeDtypeStruct((m, n), BF16), jax.ShapeDtypeStruct((m, d), BF16)],
        scratch_shapes=[pltpu.VMEM((d, tn), BF16)],
        compiler_params=pltpu.CompilerParams(
            dimension_semantics=("arbitrary", "arbitrary"), vmem_limit_bytes=VMEM_LIMIT),
        name="proj_x",
    )(x2d, mod3, mod3, w)


def _proj(h, w, gain, cos_t, sin_t, *, kind, col0, n, q_scale, tm=2048, tn=1024, mm=256, rc=64):
    m, d = h.shape
    tiles_per_seq = cos_t.shape[0] // tm
    j0 = col0 // tn
    kern = functools.partial(_proj_kernel, kind, mm, rc, q_scale)
    return pl.pallas_call(
        kern,
        grid=(n // tn, m // tm),
        in_specs=[
            pl.BlockSpec((tm, d), lambda j, i: (i, 0)),
            pl.BlockSpec((d, tn), lambda j, i: (0, j0 + j)),
            pl.BlockSpec((1, LANES), lambda j, i: (0, 0)),
            pl.BlockSpec((tm, LANES), lambda j, i: (i % tiles_per_seq, 0)),
            pl.BlockSpec((tm, LANES), lambda j, i: (i % tiles_per_seq, 0)),
        ],
        out_specs=pl.BlockSpec((tm, tn), lambda j, i: (i, j)),
        out_shape=jax.ShapeDtypeStruct((m, n), BF16),
        scratch_shapes=[pltpu.VMEM((d, tn), BF16)],
        compiler_params=pltpu.CompilerParams(
            dimension_semantics=("arbitrary", "arbitrary"), vmem_limit_bytes=VMEM_LIMIT),
        name="proj_" + kind,
    )(h, w, gain, cos_t, sin_t)


Q_ROWS = 2
BAND = 10
AHEAD = 2


def _band_start(r0, rows):
    return min(max(r0 - 4, 0), rows - BAND)


def _build_bias_tables(rpb_ref, bias_ref, rows, win_r, win_c):
    c_io = lax.broadcasted_iota(jnp.int32, (GRID_W, LANES), 0)
    l_io = lax.broadcasted_iota(jnp.int32, (GRID_W, LANES), 1)
    cs = jnp.clip(c_io - win_c // 2, 0, GRID_W - win_c)
    inwin = (l_io >= cs) & (l_io < cs + win_c) & (l_io < GRID_W)
    low = l_io < GRID_W
    neg = jnp.full((GRID_W, LANES), NEG, F32)
    toep = []
    for dr in range(2 * win_r - 1):
        row = jnp.broadcast_to(rpb_ref[0, dr:dr + 1, :], (GRID_W, LANES))
        t = pltpu.roll(row, LANES - (win_c - 1), 1, stride=1, stride_axis=0)
        toep.append(jnp.where(inwin, t * LOG2E, NEG))
    reps = [0, 2, 4, rows - 4, rows - 2]
    for tb, r0 in enumerate(reps):
        s0 = _band_start(r0, rows)
        assert (r0 - s0) == 2 * tb
        for rho in range(Q_ROWS):
            r = r0 + rho
            rs = min(max(r - win_r // 2, 0), rows - win_r)
            blocks = []
            for i in range(BAND):
                kr = s0 + i
                blocks.append(toep[kr - r + win_r - 1] if rs <= kr < rs + win_r else neg)
            for p in range(BAND // 2):
                tile = jnp.where(low, blocks[2 * p], pltpu.roll(blocks[2 * p + 1], GRID_W, 1))
                bias_ref[tb, rho * GRID_W:(rho + 1) * GRID_W, p * LANES:(p + 1) * LANES] = tile


def _attn_kernel(rows, win_r, win_c,
                 q_ref, k_ref, v_ref, z_ref, kc_ref, vc_ref, rpb_ref, o_ref, bias_ref, s_ref):
    @pl.when(pl.program_id(1) == 0)
    def _():
        _build_bias_tables(rpb_ref, bias_ref, rows, win_r, win_c)

    nq = Q_ROWS * GRID_W
    nk = BAND * GRID_W
    def scores(i, r0):
        s0 = _band_start(r0, rows)
        tb = (r0 - s0) // 2
        q0, k0 = r0 * GRID_W, s0 * GRID_W
        qb = q_ref[q0:q0 + nq, :]
        s_loc = _nt_dot(qb, k_ref[k0:k0 + nk, :]) + bias_ref[tb]
        s_ctx = _nt_dot(qb, kc_ref[...])
        slot = i % (AHEAD + 1)
        s_ref[slot, :, :nk] = s_loc
        s_ref[slot, :, nk:] = s_ctx
        return jnp.maximum(jnp.max(s_loc, axis=-1, keepdims=True), jnp.max(s_ctx, axis=-1, keepdims=True))

    def finish(i, r0, m):
        q0, k0 = r0 * GRID_W, _band_start(r0, rows) * GRID_W
        p = jnp.exp2(s_ref[i % (AHEAD + 1)] - m)
        den = jnp.sum(p, axis=-1, keepdims=True)
        pb = p.astype(BF16)
        o = (jnp.dot(pb[:, :nk], v_ref[k0:k0 + nk, :], preferred_element_type=F32)
             + jnp.dot(pb[:, nk:], vc_ref[...], preferred_element_type=F32))
        og = (o / den) * z_ref[q0:q0 + nq, :].astype(F32)
        o_ref[q0:q0 + nq, :] = og.astype(BF16)

    starts = list(range(0, rows, Q_ROWS))
    pending = [scores(i, r0) for i, r0 in enumerate(starts[:AHEAD])]
    for i, r0 in enumerate(starts):
        if i + AHEAD < len(starts):
            pending.append(scores(i + AHEAD, starts[i + AHEAD]))
        finish(i, r0, pending.pop(0))


def _attention(q, k, v, z, kc, vc, rpb_pad, *, batch, seq, ctx_len, heads, win_r, win_c):
    rows = seq // GRID_W
    kern = functools.partial(_attn_kernel, rows, win_r, win_c)
    tok = pl.BlockSpec((seq, LANES), lambda h, b: (b, h))
    ctx = pl.BlockSpec((ctx_len, LANES), lambda h, b: (b, h))
    return pl.pallas_call(
        kern,
        grid=(heads, batch),
        in_specs=[tok, tok, tok, tok, ctx, ctx,
                  pl.BlockSpec((1,) + rpb_pad.shape[1:], lambda h, b: (h, 0, 0))],
        out_specs=tok,
        out_shape=jax.ShapeDtypeStruct((batch * seq, heads * LANES), BF16),
        scratch_shapes=[pltpu.VMEM((5, Q_ROWS * GRID_W, BAND * GRID_W), F32),
                        pltpu.VMEM((AHEAD + 1, Q_ROWS * GRID_W, BAND * GRID_W + ctx_len), F32)],
        compiler_params=pltpu.CompilerParams(
            dimension_semantics=("arbitrary", "arbitrary"), vmem_limit_bytes=VMEM_LIMIT),
        name="attn",
    )(q, k, v, z, kc, vc, rpb_pad)


def _dft_mats(n):
    jk = (np.arange(n)[:, None] * np.arange(n)[None, :]) % n
    ang = 2.0 * np.pi * jk.astype(np.float64) / n
    return np.cos(ang) / np.sqrt(n), np.sin(ang) / np.sqrt(n)


def _fourier_kernel(gd, u_ref, zf_ref, csc_ref, csn_ref, o_ref, ab_ref):
    seq = u_ref.shape[0]

    @pl.when(pl.program_id(1) == 0)
    def _():
        rb = 512
        for g in range(u_ref.shape[1] // gd):
            for r in range(0, seq, rb):
                t = jnp.dot(u_ref[r:r + rb, g * gd:(g + 1) * gd], csc_ref[...], preferred_element_type=F32)
                ab_ref[r:r + rb, g * gd:(g + 1) * gd] = t[:, :gd].astype(BF16)
                ab_ref[seq + r:seq + r + rb, g * gd:(g + 1) * gd] = t[:, gd:].astype(BF16)

    y = jnp.dot(csn_ref[...], ab_ref[...], preferred_element_type=F32)
    o_ref[...] = (y * zf_ref[...].astype(F32)).astype(BF16)


def _fourier(u, zf, csc, csn, *, batch, seq, gd, tk=512):
    fw = u.shape[1]
    steps = seq // tk
    kern = functools.partial(_fourier_kernel, gd)
    return pl.pallas_call(
        kern,
        grid=(batch, steps),
        in_specs=[
            pl.BlockSpec((seq, fw), lambda b, k: (b, 0)),
            pl.BlockSpec((tk, fw), lambda b, k: (b * steps + k, 0)),
            pl.BlockSpec(csc.shape, lambda b, k: (0, 0)),
            pl.BlockSpec((tk, 2 * seq), lambda b, k: (k, 0)),
        ],
        out_specs=pl.BlockSpec((tk, fw), lambda b, k: (b * steps + k, 0)),
        out_shape=jax.ShapeDtypeStruct((batch * seq, fw), BF16),
        scratch_shapes=[pltpu.VMEM((2 * seq, fw), BF16)],
        compiler_params=pltpu.CompilerParams(
            dimension_semantics=("parallel", "arbitrary"), vmem_limit_bytes=VMEM_LIMIT),
        name="fourier",
    )(u, zf, csc, csn)


def _merge_kernel(yg_ref, og_ref, sgf_ref, sga_ref, x_ref, gate_ref, wf_ref, wa_ref, wo_ref, o_ref):
    yf = jnp.dot(yg_ref[...], wf_ref[...], preferred_element_type=F32)
    ya = jnp.dot(og_ref[...], wa_ref[...], preferred_element_type=F32)
    y = sgf_ref[...].astype(F32) * yf + sga_ref[...].astype(F32) * ya
    yo = jnp.dot(y.astype(BF16), wo_ref[...], preferred_element_type=F32)
    o_ref[...] = x_ref[...] + gate_ref[0] * yo


def _merge(yg, og, g, x2d, mod3, wf, wa, wo, *, seq, tm=256):
    m, d = x2d.shape
    tiles_per_seq = seq // tm
    const = lambda shape: pl.BlockSpec(shape, lambda i: (0, 0), pipeline_mode=pl.Buffered(1))
    return pl.pallas_call(
        _merge_kernel,
        grid=(m // tm,),
        in_specs=[
            pl.BlockSpec((tm, yg.shape[1]), lambda i: (i, 0)),
            pl.BlockSpec((tm, d), lambda i: (i, 0)),
            pl.BlockSpec((tm, d), lambda i: (i, 0)),
            pl.BlockSpec((tm, d), lambda i: (i, 1)),
            pl.BlockSpec((tm, d), lambda i: (i, 0)),
            pl.BlockSpec((1, 1, d), lambda i: (i // tiles_per_seq, 0, 2)),
            const(wf.shape), const(wa.shape), const(wo.shape),
        ],
        out_specs=pl.BlockSpec((tm, d), lambda i: (i, 0)),
        out_shape=jax.ShapeDtypeStruct((m, d), F32),
        compiler_params=pltpu.CompilerParams(
            dimension_semantics=("parallel",), vmem_limit_bytes=VMEM_LIMIT),
        name="merge",
    )(yg, og, g, g, x2d, mod3, wf, wa, wo)


def _rope_tables(seq, head_dim):
    n_freq = head_dim // 4
    t = np.arange(seq)
    pos = np.stack([t // GRID_W, t % GRID_W], axis=-1).astype(np.float32)
    inv_freq = (np.float32(ROPE_BASE) ** (-np.arange(n_freq, dtype=np.float32) / np.float32(n_freq)))
    ang = (pos[:, :, None] * inv_freq.astype(np.float32)).astype(np.float64)
    ang = np.broadcast_to(ang[:, None, :, :], (seq, 2, 2, n_freq))
    sign = np.array([-1.0, 1.0])[None, :, None, None]
    return (np.cos(ang).reshape(seq, head_dim).astype(np.float32),
            (np.sin(ang) * sign).reshape(seq, head_dim).astype(np.float32))


def _rope_lane_order(a, n_freq):
    lead = a.shape[:-1]
    return a.reshape(lead + (-1, 2, 2, n_freq)).swapaxes(-3, -2).reshape(a.shape)


def kernel(x, c, ctx, c_ctx, w_mod, b_mod, w_in, q_gain, k_gain, rpb, w_f_out, w_a_out, w_out):
    batch, seq, d = x.shape
    ctx_len = ctx.shape[1]
    depth, heads, n_dr, n_dc = rpb.shape
    assert depth == 1 and w_mod.shape[0] == 1
    head_dim = q_gain.shape[1]
    assert head_dim == LANES and seq % GRID_W == 0
    win_r, win_c = (n_dr + 1) // 2, (n_dc + 1) // 2
    attn_w = heads * head_dim
    fw = w_f_out.shape[1]
    gd = fw // F_GROUPS
    off_zf, off_q = fw, 2 * fw
    off_k, off_v, off_za = off_q + attn_w, off_q + 2 * attn_w, off_q + 3 * attn_w
    off_gf = off_za + attn_w
    off_ga = off_gf + d
    assert w_in.shape[2] == off_ga + d

    c_all = jnp.concatenate([c, c_ctx[None, :], jnp.zeros((16 - batch - 1, d), F32)], axis=0)
    mod = _mod(c_all, w_mod[0], b_mod)
    mod3 = mod.reshape(16, 1, 3 * d)

    x2d = x.reshape(batch * seq, d)
    u_f, h_x = _proj_x(x2d, mod3, w_in[0], n=fw, rows_per_mod=seq)
    h_c =_hnorm(ctx.reshape(batch * ctx_len, d), mod3, rows_per_mod=batch * ctx_len, mod_row0=batch)

    n_freq = head_dim // 4
    cos_np, sin_np = _rope_tables(seq, head_dim)
    cos_t, sin_t = jnp.asarray(cos_np), jnp.asarray(sin_np)
    qg = _rope_lane_order(q_gain, n_freq)
    kg = _rope_lane_order(k_gain, n_freq)
    q_scale = float(head_dim) ** -0.5 * LOG2E
    proj =functools.partial(_proj, w=w_in[0], cos_t=cos_t, sin_t=sin_t, q_scale=q_scale)

    z_f = proj(h_x, gain=qg, kind="silu", col0=off_zf, n=fw)
    q = proj(h_x, gain=qg, kind="q", col0=off_q, n=attn_w)
    k = proj(h_x, gain=kg, kind="k", col0=off_k, n=attn_w)
    v = proj(h_x, gain=qg, kind="raw", col0=off_v, n=attn_w)
    z_a = proj(h_x, gain=qg, kind="silu", col0=off_za, n=attn_w)
    g = proj(h_x, gain=qg, kind="sig", col0=off_gf, n=2 * d)
    k_c = proj(h_c, gain=kg, kind="kc", col0=off_k, n=attn_w)
    v_c = proj(h_c, gain=qg, kind="raw", col0=off_v, n=attn_w)

    rpb_pad = jnp.pad(rpb[0], ((0, 0), (0, 16 - n_dr), (0, LANES - n_dc)))
    og = _attention(q, k, v, z_a, k_c, v_c, rpb_pad, batch=batch, seq=seq, ctx_len=ctx_len, heads=heads,
                    win_r=win_r, win_c=win_c)

    cc, sc = _dft_mats(gd)
    cn, sn = _dft_mats(seq)
    csc = jnp.asarray(np.concatenate([cc, sc], axis=1).astype(np.float32)).astype(BF16)
    csn = jnp.asarray(np.concatenate([cn, -sn], axis=1).astype(np.float32)).astype(BF16)
    yg = _fourier(u_f, z_f, csc, csn, batch=batch, seq=seq, gd=gd)

    out = _merge(yg, og, g, x2d, mod3, w_f_out[0].astype(BF16), w_a_out[0].astype(BF16),
                 w_out[0].astype(BF16), seq=seq)
    return out.reshape(batch, seq, d)
```

```python
import functools

import numpy as np
import jax
import jax.numpy as jnp
from jax import lax
from jax.experimental import pallas as pl
from jax.experimental.pallas import tpu as pltpu

GRID_W = 64
F_GROUPS = 4
ROPE_BASE = 10000.0
EPS = 1e-6
NEG = -1e30
LOG2E = 1.4426950408889634
LANES = 128
VMEM_LIMIT = 56 * 1024 * 1024

BF16 = jnp.bfloat16
F32 = jnp.float32


def _nt_dot(a, b):
    return lax.dot_general(a, b, (((1,), (1,)), ((), ())), preferred_element_type=F32)


def _mod_kernel(c_ref, w_ref, b_ref, o_ref):
    a = jax.nn.silu(c_ref[...]).astype(BF16)
    o_ref[...] = jnp.dot(a, w_ref[...].astype(BF16), preferred_element_type=F32) + b_ref[...]


def _mod(c_all, w_mod, b_mod, tn=512):
    m, d = c_all.shape
    n = w_mod.shape[1]
    return pl.pallas_call(
        _mod_kernel,
        grid=(n // tn,),
        in_specs=[pl.BlockSpec((m, d), lambda j: (0, 0)),
                  pl.BlockSpec((d, tn), lambda j: (0, j)),
                  pl.BlockSpec((1, tn), lambda j: (0, j))],
        out_specs=pl.BlockSpec((m, tn), lambda j: (0, j)),
        out_shape=jax.ShapeDtypeStruct((m, n), F32),
        name="mod",
    )(c_all, w_mod, b_mod)


def _hnorm_kernel(rc, x_ref, shift_ref, scale_ref, o_ref):
    def body(t, carry):
        r = pl.multiple_of(t * rc, rc)
        xs = x_ref[pl.ds(r, rc), :]
        ms = jnp.mean(xs * xs, axis=-1, keepdims=True)
        h = xs * lax.rsqrt(ms + EPS) * (1.0 + scale_ref[0]) + shift_ref[0]
        o_ref[pl.ds(r, rc), :] = h.astype(BF16)
        return carry
    lax.fori_loop(0, x_ref.shape[0] // rc, body, 0)


def _hnorm(x2d, mod3, *, rows_per_mod, mod_row0, tm=512, rc=64):
    m, d = x2d.shape
    tiles_per_mod = rows_per_mod // tm
    return pl.pallas_call(
        functools.partial(_hnorm_kernel, rc),
        grid=(m // tm,),
        in_specs=[
            pl.BlockSpec((tm, d), lambda i: (i, 0)),
            pl.BlockSpec((1, 1, d), lambda i: (mod_row0 + i // tiles_per_mod, 0, 0)),
            pl.BlockSpec((1, 1, d), lambda i: (mod_row0 + i // tiles_per_mod, 0, 1)),
        ],
        out_specs=pl.BlockSpec((tm, d), lambda i: (i, 0)),
        out_shape=jax.ShapeDtypeStruct((m, d), BF16),
        compiler_params=pltpu.CompilerParams(dimension_semantics=("parallel",)),
        name="hnorm",
    )(x2d, mod3, mod3)


def _rope_lane_order_cols(w):
    n = w.shape[1]
    quarter = (lax.broadcasted_iota(jnp.int32, w.shape, 1) % LANES) // (LANES // 4)
    up = pltpu.roll(w, n - LANES // 4, 1)
    down = pltpu.roll(w, LANES // 4, 1)
    return jnp.where(quarter == 1, up, jnp.where(quarter == 2, down, w))


def _proj_kernel(kind, mm, rc, q_scale, h_ref, w_ref, gain_ref, cos_ref, sin_ref, o_ref, wb_ref):
    tm = h_ref.shape[0]
    tn = w_ref.shape[1]

    @pl.when(pl.program_id(1) == 0)
    def _():
        _cast_weight_tile(w_ref, wb_ref, rc, kind in ("q", "k", "kc"))

    for r in range(0, tm, mm):
        a = jnp.dot(h_ref[r:r + mm, :], wb_ref[...], preferred_element_type=F32)
        if kind == "raw":
            o_ref[r:r + mm, :] = a.astype(BF16)
        elif kind == "silu":
            o_ref[r:r + mm, :] = jax.nn.silu(a).astype(BF16)
        elif kind == "sig":
            o_ref[r:r + mm, :] = jax.nn.sigmoid(a).astype(BF16)
        else:
            gain = gain_ref[...]
            for hh in range(tn // LANES):
                xh = a[:, hh * LANES:(hh + 1) * LANES]
                ms = jnp.mean(xh * xh, axis=-1, keepdims=True)
                xn = xh * lax.rsqrt(ms + EPS) * gain
                if kind != "kc":
                    xn = xn * cos_ref[r:r + mm, :] + pltpu.roll(xn, LANES // 2, 1) * sin_ref[r:r + mm, :]
                if kind == "q":
                    xn = xn * q_scale
                o_ref[r:r + mm, hh * LANES:(hh + 1) * LANES] = xn.astype(BF16)


def _cast_weight_tile(w_ref, wb_ref, rc, reorder):
    def body(t, carry):
        r = pl.multiple_of(t * rc, rc)
        wt = w_ref[pl.ds(r, rc), :]
        if reorder:
            wt = _rope_lane_order_cols(wt)
        wb_ref[pl.ds(r, rc), :] = wt.astype(BF16)
        return carry
    lax.fori_loop(0, w_ref.shape[0] // rc, body, 0)


def _proj_x_kernel(mm, rc, x_ref, shift_ref, scale_ref, w_ref, o_ref, h_ref, wb_ref):
    @pl.when(pl.program_id(1) == 0)
    def _():
        _cast_weight_tile(w_ref, wb_ref, rc, False)

    for r in range(0, x_ref.shape[0], mm):
        xs = x_ref[r:r + mm, :]
        ms = jnp.mean(xs * xs, axis=-1, keepdims=True)
        h = (xs * lax.rsqrt(ms + EPS) * (1.0 + scale_ref[0]) + shift_ref[0]).astype(BF16)
        h_ref[r:r + mm, :] = h
        o_ref[r:r + mm, :] = jnp.dot(h, wb_ref[...], preferred_element_type=F32).astype(BF16)


def _proj_x(x2d, mod3, w, *, n, rows_per_mod, tm=1024, tn=1024, mm=256, rc=64):
    m, d = x2d.shape
    assert n == tn
    tiles_per_mod = rows_per_mod // tm
    return pl.pallas_call(
        functools.partial(_proj_x_kernel, mm, rc),
        grid=(n // tn, m // tm),
        in_specs=[
            pl.BlockSpec((tm, d), lambda j, i: (i, 0)),
            pl.BlockSpec((1, 1, d), lambda j, i: (i // tiles_per_mod, 0, 0)),
            pl.BlockSpec((1, 1, d), lambda j, i: (i // tiles_per_mod, 0, 1)),
            pl.BlockSpec((d, tn), lambda j, i: (0, j)),
        ],
        out_specs=[pl.BlockSpec((tm, tn), lambda j, i: (i, j)),
                   pl.BlockSpec((tm, d), lambda j, i: (i, 0))],
        out_shape=[jax.ShapeDtypeStruct((m, n), BF16), jax.ShapeDtypeStruct((m, d), BF16)],
        scratch_shapes=[pltpu.VMEM((d, tn), BF16)],
        compiler_params=pltpu.CompilerParams(
            dimension_semantics=("arbitrary", "arbitrary"), vmem_limit_bytes=VMEM_LIMIT),
        name="proj_x",
    )(x2d, mod3, mod3, w)


def _proj(h, w, gain, cos_t, sin_t, *, kind, col0, n, q_scale, tm=2048, tn=1024, rc=64):
    m, d = h.shape
    mm = 512 if kind == "raw" else 256
    tiles_per_seq = cos_t.shape[0] // tm
    j0 = col0 // tn
    kern = functools.partial(_proj_kernel, kind, mm, rc, q_scale)
    return pl.pallas_call(
        kern,
        grid=(n // tn, m // tm),
        in_specs=[
            pl.BlockSpec((tm, d), lambda j, i: (i, 0)),
            pl.BlockSpec((d, tn), lambda j, i: (0, j0 + j)),
            pl.BlockSpec((1, LANES), lambda j, i: (0, 0)),
            pl.BlockSpec((tm, LANES), lambda j, i: (i % tiles_per_seq, 0)),
            pl.BlockSpec((tm, LANES), lambda j, i: (i % tiles_per_seq, 0)),
        ],
        out_specs=pl.BlockSpec((tm, tn), lambda j, i: (i, j)),
        out_shape=jax.ShapeDtypeStruct((m, n), BF16),
        scratch_shapes=[pltpu.VMEM((d, tn), BF16)],
        compiler_params=pltpu.CompilerParams(
            dimension_semantics=("arbitrary", "arbitrary"), vmem_limit_bytes=VMEM_LIMIT),
        name="proj_" + kind,
    )(h, w, gain, cos_t, sin_t)


Q_ROWS = 2
BAND = 10
GROUP = 1
AHEAD = 2
N_SLOTS = (AHEAD + 1) * GROUP


def _band_start(r0, rows):
    return min(max(r0 - 4, 0), rows - BAND)


def _build_bias_tables(rpb_ref, bias_ref, rows, win_r, win_c):
    c_io = lax.broadcasted_iota(jnp.int32, (GRID_W, LANES), 0)
    l_io = lax.broadcasted_iota(jnp.int32, (GRID_W, LANES), 1)
    cs = jnp.clip(c_io - win_c // 2, 0, GRID_W - win_c)
    inwin = (l_io >= cs) & (l_io < cs + win_c) & (l_io < GRID_W)
    low = l_io < GRID_W
    neg = jnp.full((GRID_W, LANES), NEG, F32)
    toep = []
    for dr in range(2 * win_r - 1):
        row = jnp.broadcast_to(rpb_ref[0, dr:dr + 1, :], (GRID_W, LANES))
        t = pltpu.roll(row, LANES - (win_c - 1), 1, stride=1, stride_axis=0)
        toep.append(jnp.where(inwin, t * LOG2E, NEG))
    reps = [0, 2, 4, rows - 4, rows - 2]
    for tb, r0 in enumerate(reps):
        s0 = _band_start(r0, rows)
        assert (r0 - s0) == 2 * tb
        for rho in range(Q_ROWS):
            r = r0 + rho
            rs = min(max(r - win_r // 2, 0), rows - win_r)
            blocks = []
            for i in range(BAND):
                kr = s0 + i
                blocks.append(toep[kr - r + win_r - 1] if rs <= kr < rs + win_r else neg)
            for p in range(BAND // 2):
                tile = jnp.where(low, blocks[2 * p], pltpu.roll(blocks[2 * p + 1], GRID_W, 1))
                bias_ref[tb, rho * GRID_W:(rho + 1) * GRID_W, p * LANES:(p + 1) * LANES] = tile


def _attn_kernel(rows, win_r, win_c,
                 q_ref, k_ref, v_ref, z_ref, kc_ref, vc_ref, rpb_ref, o_ref, bias_ref, s_ref):
    @pl.when(pl.program_id(1) == 0)
    def _():
        _build_bias_tables(rpb_ref, bias_ref, rows, win_r, win_c)

    nq = Q_ROWS * GRID_W
    nk = BAND * GRID_W
    def scores(i, r0):
        s0 = _band_start(r0, rows)
        tb = (r0 - s0) // 2
        q0, k0 = r0 * GRID_W, s0 * GRID_W
        qb = q_ref[q0:q0 + nq, :]
        s_loc = _nt_dot(qb, k_ref[k0:k0 + nk, :]) + bias_ref[tb]
        s_ctx = _nt_dot(qb, kc_ref[...])
        slot = i % N_SLOTS
        s_ref[slot, :, :nk] = s_loc
        s_ref[slot, :, nk:] = s_ctx
        return jnp.maximum(jnp.max(s_loc, axis=-1, keepdims=True), jnp.max(s_ctx, axis=-1, keepdims=True))

    def finish(i, r0, m):
        q0, k0 = r0 * GRID_W, _band_start(r0, rows) * GRID_W
        p = jnp.exp2(s_ref[i % N_SLOTS] - m)
        den = jnp.sum(p, axis=-1, keepdims=True)
        v_all = jnp.concatenate([v_ref[k0:k0 + nk, :], vc_ref[...]], axis=0)
        o = jnp.dot(p.astype(BF16), v_all, preferred_element_type=F32)
        og = (o / den) * z_ref[q0:q0 + nq, :].astype(F32)
        o_ref[q0:q0 + nq, :] = og.astype(BF16)

    starts = list(range(0, rows, Q_ROWS))
    groups = [starts[g:g + GROUP] for g in range(0, len(starts), GROUP)]

    def issue(g):
        return [scores(g * GROUP + u, r0) for u, r0 in enumerate(groups[g])]

    pending = [issue(g) for g in range(AHEAD)]
    for g, grp in enumerate(groups):
        if g + AHEAD < len(groups):
            pending.append(issue(g + AHEAD))
        for u, (r0, m) in enumerate(zip(grp, pending.pop(0))):
            finish(g * GROUP + u, r0, m)


def _attention(q, k, v, z, kc, vc, rpb_pad, *, batch, seq, ctx_len, heads, win_r, win_c):
    rows = seq // GRID_W
    kern = functools.partial(_attn_kernel, rows, win_r, win_c)
    tok = pl.BlockSpec((seq, LANES), lambda h, b: (b, h))
    ctx = pl.BlockSpec((ctx_len, LANES), lambda h, b: (b, h))
    return pl.pallas_call(
        kern,
        grid=(heads, batch),
        in_specs=[tok, tok, tok, tok, ctx, ctx,
                  pl.BlockSpec((1,) + rpb_pad.shape[1:], lambda h, b: (h, 0, 0))],
        out_specs=tok,
        out_shape=jax.ShapeDtypeStruct((batch * seq, heads * LANES), BF16),
        scratch_shapes=[pltpu.VMEM((5, Q_ROWS * GRID_W, BAND * GRID_W), F32),
                        pltpu.VMEM((N_SLOTS, Q_ROWS * GRID_W, BAND * GRID_W + ctx_len), F32)],
        compiler_params=pltpu.CompilerParams(
            dimension_semantics=("arbitrary", "arbitrary"), vmem_limit_bytes=VMEM_LIMIT),
        name="attn",
    )(q, k, v, z, kc, vc, rpb_pad)


def _dft_mats(n):
    jk = (np.arange(n)[:, None] * np.arange(n)[None, :]) % n
    ang = 2.0 * np.pi * jk.astype(np.float64) / n
    return np.cos(ang) / np.sqrt(n), np.sin(ang) / np.sqrt(n)


def _fourier_kernel(gd, u_ref, zf_ref, csc_ref, ch_ref, sh_ref, nyq_ref, flip_ref, o_ref, a_ref, b_ref, e_ref):
    seq = u_ref.shape[0]
    half = seq // 2
    tk = o_ref.shape[0]
    n_first = half // tk
    s = pl.program_id(1)
    zf = zf_ref[...].astype(F32)

    @pl.when(s == 0)
    def _():
        rb = 512
        for g in range(u_ref.shape[1] // gd):
            for r in range(0, seq, rb):
                t = jnp.dot(u_ref[r:r + rb, g * gd:(g + 1) * gd], csc_ref[...], preferred_element_type=F32)
                a_ref[r:r + rb, g * gd:(g + 1) * gd] = t[:, :gd].astype(BF16)
                b_ref[r:r + rb, g * gd:(g + 1) * gd] = t[:, gd:].astype(BF16)
        e_ref[half:, :] = jnp.zeros((tk, e_ref.shape[1]), BF16)
        e_ref[half:half + nyq_ref.shape[0], :] = jnp.dot(
            nyq_ref[...], a_ref[...], preferred_element_type=F32).astype(BF16)

    @pl.when(s < n_first)
    def _():
        p = jnp.dot(ch_ref[...], a_ref[...], preferred_element_type=F32)
        q = jnp.dot(sh_ref[...], b_ref[...], preferred_element_type=F32)
        o_ref[...] = ((p - q) * zf).astype(BF16)
        e_ref[pl.ds(pl.multiple_of(s * tk, tk), tk), :] = (p + q).astype(BF16)

    @pl.when(s >= n_first)
    def _():
        base = pl.multiple_of(half - (s - n_first + 1) * tk, tk)
        y = jnp.dot(flip_ref[...], e_ref[pl.ds(base, 2 * tk), :], preferred_element_type=F32)
        o_ref[...] = (y * zf).astype(BF16)


def _fourier(u, zf, csc, ch, sh, nyq, flip, *, batch, seq, gd, tk=256):
    fw = u.shape[1]
    half = seq // 2
    steps = seq // tk
    n_first = half // tk
    kern = functools.partial(_fourier_kernel, gd)
    half_rows = pl.BlockSpec((tk, seq), lambda b, k: (jnp.minimum(k, n_first - 1), 0))
    whole = lambda a: pl.BlockSpec(a.shape, lambda b, k: (0, 0))
    return pl.pallas_call(
        kern,
        grid=(batch, steps),
        in_specs=[
            pl.BlockSpec((seq, fw), lambda b, k: (b, 0)),
            pl.BlockSpec((tk, fw), lambda b, k: (b * steps + k, 0)),
            whole(csc), half_rows, half_rows, whole(nyq), whole(flip),
        ],
        out_specs=pl.BlockSpec((tk, fw), lambda b, k: (b * steps + k, 0)),
        out_shape=jax.ShapeDtypeStruct((batch * seq, fw), BF16),
        scratch_shapes=[pltpu.VMEM((seq, fw), BF16), pltpu.VMEM((seq, fw), BF16),
                        pltpu.VMEM((half + tk, fw), BF16)],
        compiler_params=pltpu.CompilerParams(
            dimension_semantics=("arbitrary", "arbitrary"), vmem_limit_bytes=VMEM_LIMIT),
        name="fourier",
    )(u, zf, csc, ch, sh, nyq, flip)


def _merge_kernel(yg_ref, og_ref, sgf_ref, sga_ref, x_ref, gate_ref, wf_ref, wa_ref, wo_ref, o_ref):
    yf = jnp.dot(yg_ref[...], wf_ref[...], preferred_element_type=F32)
    ya = jnp.dot(og_ref[...], wa_ref[...], preferred_element_type=F32)
    y = sgf_ref[...].astype(F32) * yf + sga_ref[...].astype(F32) * ya
    yo = jnp.dot(y.astype(BF16), wo_ref[...], preferred_element_type=F32)
    o_ref[...] = x_ref[...] + gate_ref[0] * yo


def _merge(yg, og, g, x2d, mod3, wf, wa, wo, *, seq, tm=256):
    m, d = x2d.shape
    tiles_per_seq = seq // tm
    const = lambda shape: pl.BlockSpec(shape, lambda i: (0, 0), pipeline_mode=pl.Buffered(1))
    return pl.pallas_call(
        _merge_kernel,
        grid=(m // tm,),
        in_specs=[
            pl.BlockSpec((tm, yg.shape[1]), lambda i: (i, 0)),
            pl.BlockSpec((tm, d), lambda i: (i, 0)),
            pl.BlockSpec((tm, d), lambda i: (i, 0)),
            pl.BlockSpec((tm, d), lambda i: (i, 1)),
            pl.BlockSpec((tm, d), lambda i: (i, 0)),
            pl.BlockSpec((1, 1, d), lambda i: (i // tiles_per_seq, 0, 2)),
            const(wf.shape), const(wa.shape), const(wo.shape),
        ],
        out_specs=pl.BlockSpec((tm, d), lambda i: (i, 0)),
        out_shape=jax.ShapeDtypeStruct((m, d), F32),
        compiler_params=pltpu.CompilerParams(
            dimension_semantics=("parallel",), vmem_limit_bytes=VMEM_LIMIT),
        name="merge",
    )(yg, og, g, g, x2d, mod3, wf, wa, wo)


def _rope_tables(seq, head_dim):
    n_freq = head_dim // 4
    t = np.arange(seq)
    pos = np.stack([t // GRID_W, t % GRID_W], axis=-1).astype(np.float32)
    inv_freq = (np.float32(ROPE_BASE) ** (-np.arange(n_freq, dtype=np.float32) / np.float32(n_freq)))
    ang = (pos[:, :, None] * inv_freq.astype(np.float32)).astype(np.float64)
    ang = np.broadcast_to(ang[:, None, :, :], (seq, 2, 2, n_freq))
    sign = np.array([-1.0, 1.0])[None, :, None, None]
    return (np.cos(ang).reshape(seq, head_dim).astype(np.float32),
            (np.sin(ang) * sign).reshape(seq, head_dim).astype(np.float32))


def _rope_lane_order(a, n_freq):
    lead = a.shape[:-1]
    return a.reshape(lead + (-1, 2, 2, n_freq)).swapaxes(-3, -2).reshape(a.shape)


def kernel(x, c, ctx, c_ctx, w_mod, b_mod, w_in, q_gain, k_gain, rpb, w_f_out, w_a_out, w_out):
    batch, seq, d = x.shape
    ctx_len = ctx.shape[1]
    depth, heads, n_dr, n_dc = rpb.shape
    assert depth == 1 and w_mod.shape[0] == 1
    head_dim = q_gain.shape[1]
    assert head_dim == LANES and seq % GRID_W == 0
    win_r, win_c = (n_dr + 1) // 2, (n_dc + 1) // 2
    attn_w = heads * head_dim
    fw = w_f_out.shape[1]
    gd = fw // F_GROUPS
    off_zf, off_q = fw, 2 * fw
    off_k, off_v, off_za = off_q + attn_w, off_q + 2 * attn_w, off_q + 3 * attn_w
    off_gf = off_za + attn_w
    off_ga = off_gf + d
    assert w_in.shape[2] == off_ga + d

    c_all = jnp.concatenate([c, c_ctx[None, :], jnp.zeros((16 - batch - 1, d), F32)], axis=0)
    mod = _mod(c_all, w_mod[0], b_mod)
    mod3 = mod.reshape(16, 1, 3 * d)

    x2d = x.reshape(batch * seq, d)
    u_f, h_x = _proj_x(x2d, mod3, w_in[0], n=fw, rows_per_mod=seq)
    h_c =_hnorm(ctx.reshape(batch * ctx_len, d), mod3, rows_per_mod=batch * ctx_len, mod_row0=batch)

    n_freq = head_dim // 4
    cos_np, sin_np = _rope_tables(seq, head_dim)
    cos_t, sin_t = jnp.asarray(cos_np), jnp.asarray(sin_np)
    qg = _rope_lane_order(q_gain, n_freq)
    kg = _rope_lane_order(k_gain, n_freq)
    q_scale = float(head_dim) ** -0.5 * LOG2E
    proj =functools.partial(_proj, w=w_in[0], cos_t=cos_t, sin_t=sin_t, q_scale=q_scale)

    z_f = proj(h_x, gain=qg, kind="silu", col0=off_zf, n=fw)
    q = proj(h_x, gain=qg, kind="q", col0=off_q, n=attn_w)
    k = proj(h_x, gain=kg, kind="k", col0=off_k, n=attn_w)
    v = proj(h_x, gain=qg, kind="raw", col0=off_v, n=attn_w)
    z_a = proj(h_x, gain=qg, kind="silu", col0=off_za, n=attn_w)
    g = proj(h_x, gain=qg, kind="sig", col0=off_gf, n=2 * d)
    k_c = proj(h_c, gain=kg, kind="kc", col0=off_k, n=attn_w)
    v_c = proj(h_c, gain=qg, kind="raw", col0=off_v, n=attn_w)

    rpb_pad = jnp.pad(rpb[0], ((0, 0), (0, 16 - n_dr), (0, LANES - n_dc)))
    og = _attention(q, k, v, z_a, k_c, v_c, rpb_pad, batch=batch, seq=seq, ctx_len=ctx_len, heads=heads,
                    win_r=win_r, win_c=win_c)

    cc, sc = _dft_mats(gd)
    cn, sn = _dft_mats(seq)
    const = lambda a: jnp.asarray(a.astype(np.float32)).astype(BF16)
    half, tk = seq // 2, 256
    nyq = np.zeros((16, seq))
    nyq[0] = cn[half]
    flip = np.zeros((tk, 2 * tk))
    flip[np.arange(tk), tk - np.arange(tk)] = 1.0
    yg = _fourier(u_f, z_f, const(np.concatenate([cc, sc], axis=1)), const(cn[:half]), const(sn[:half]),
                  const(nyq), const(flip), batch=batch, seq=seq, gd=gd, tk=tk)

    out = _merge(yg, og, g, x2d, mod3, w_f_out[0].astype(BF16), w_a_out[0].astype(BF16),
                 w_out[0].astype(BF16), seq=seq)
    return out.reshape(batch, seq, d)
```

```python
import functools

import numpy as np
import jax
import jax.numpy as jnp
from jax import lax
from jax.experimental import pallas as pl
from jax.experimental.pallas import tpu as pltpu

GRID_W = 64
F_GROUPS = 4
ROPE_BASE = 10000.0
EPS = 1e-6
NEG = -1e30
LOG2E = 1.4426950408889634
LANES = 128
VMEM_LIMIT = 56 * 1024 * 1024

BF16 = jnp.bfloat16
F32 = jnp.float32


def _nt_dot(a, b):
    return lax.dot_general(a, b, (((1,), (1,)), ((), ())), preferred_element_type=F32)


def _mod_kernel(c_ref, w_ref, b_ref, o_ref):
    a = jax.nn.silu(c_ref[...]).astype(BF16)
    o_ref[...] = jnp.dot(a, w_ref[...].astype(BF16), preferred_element_type=F32) + b_ref[...]


def _mod(c_all, w_mod, b_mod, tn=512):
    m, d = c_all.shape
    n = w_mod.shape[1]
    return pl.pallas_call(
        _mod_kernel,
        grid=(n // tn,),
        in_specs=[pl.BlockSpec((m, d), lambda j: (0, 0)),
                  pl.BlockSpec((d, tn), lambda j: (0, j)),
                  pl.BlockSpec((1, tn), lambda j: (0, j))],
        out_specs=pl.BlockSpec((m, tn), lambda j: (0, j)),
        out_shape=jax.ShapeDtypeStruct((m, n), F32),
        name="mod",
    )(c_all, w_mod, b_mod)


def _hnorm_kernel(rc, x_ref, shift_ref, scale_ref, o_ref):
    def body(t, carry):
        r = pl.multiple_of(t * rc, rc)
        xs = x_ref[pl.ds(r, rc), :]
        ms = jnp.mean(xs * xs, axis=-1, keepdims=True)
        h = xs * lax.rsqrt(ms + EPS) * (1.0 + scale_ref[0]) + shift_ref[0]
        o_ref[pl.ds(r, rc), :] = h.astype(BF16)
        return carry
    lax.fori_loop(0, x_ref.shape[0] // rc, body, 0)


def _hnorm(x2d, mod3, *, rows_per_mod, mod_row0, tm=512, rc=64):
    m, d = x2d.shape
    tiles_per_mod = rows_per_mod // tm
    return pl.pallas_call(
        functools.partial(_hnorm_kernel, rc),
        grid=(m // tm,),
        in_specs=[
            pl.BlockSpec((tm, d), lambda i: (i, 0)),
            pl.BlockSpec((1, 1, d), lambda i: (mod_row0 + i // tiles_per_mod, 0, 0)),
            pl.BlockSpec((1, 1, d), lambda i: (mod_row0 + i // tiles_per_mod, 0, 1)),
        ],
        out_specs=pl.BlockSpec((tm, d), lambda i: (i, 0)),
        out_shape=jax.ShapeDtypeStruct((m, d), BF16),
        compiler_params=pltpu.CompilerParams(dimension_semantics=("parallel",)),
        name="hnorm",
    )(x2d, mod3, mod3)


def _rope_lane_order_cols(w):
    n = w.shape[1]
    quarter = (lax.broadcasted_iota(jnp.int32, w.shape, 1) % LANES) // (LANES // 4)
    up = pltpu.roll(w, n - LANES // 4, 1)
    down = pltpu.roll(w, LANES // 4, 1)
    return jnp.where(quarter == 1, up, jnp.where(quarter == 2, down, w))


def _proj_kernel(kind, mm, rc, q_scale, h_ref, w_ref, gain_ref, cos_ref, sin_ref, o_ref, wb_ref):
    tm = h_ref.shape[0]
    tn = w_ref.shape[1]

    @pl.when(pl.program_id(1) == 0)
    def _():
        _cast_weight_tile(w_ref, wb_ref, rc, kind in ("q", "k", "kc"))

    for r in range(0, tm, mm):
        a = jnp.dot(h_ref[r:r + mm, :], wb_ref[...], preferred_element_type=F32)
        if kind == "raw":
            o_ref[r:r + mm, :] = a.astype(BF16)
        elif kind == "silu":
            o_ref[r:r + mm, :] = jax.nn.silu(a).astype(BF16)
        elif kind == "sig":
            o_ref[r:r + mm, :] = jax.nn.sigmoid(a).astype(BF16)
        else:
            gain = gain_ref[...]
            for hh in range(tn // LANES):
                xh = a[:, hh * LANES:(hh + 1) * LANES]
                ms = jnp.mean(xh * xh, axis=-1, keepdims=True)
                xn = xh * lax.rsqrt(ms + EPS) * gain
                if kind != "kc":
                    xn = xn * cos_ref[r:r + mm, :] + pltpu.roll(xn, LANES // 2, 1) * sin_ref[r:r + mm, :]
                if kind == "q":
                    xn = xn * q_scale
                o_ref[r:r + mm, hh * LANES:(hh + 1) * LANES] = xn.astype(BF16)


def _cast_weight_tile(w_ref, wb_ref, rc, reorder):
    def body(t, carry):
        r = pl.multiple_of(t * rc, rc)
        wt = w_ref[pl.ds(r, rc), :]
        if reorder:
            wt = _rope_lane_order_cols(wt)
        wb_ref[pl.ds(r, rc), :] = wt.astype(BF16)
        return carry
    lax.fori_loop(0, w_ref.shape[0] // rc, body, 0)


def _proj_x_kernel(mm, rc, x_ref, shift_ref, scale_ref, w_ref, o_ref, h_ref, wb_ref):
    @pl.when(pl.program_id(1) == 0)
    def _():
        _cast_weight_tile(w_ref, wb_ref, rc, False)

    for r in range(0, x_ref.shape[0], mm):
        xs = x_ref[r:r + mm, :]
        ms = jnp.mean(xs * xs, axis=-1, keepdims=True)
        h = (xs * lax.rsqrt(ms + EPS) * (1.0 + scale_ref[0]) + shift_ref[0]).astype(BF16)
        h_ref[r:r + mm, :] = h
        o_ref[r:r + mm, :] = jnp.dot(h, wb_ref[...], preferred_element_type=F32).astype(BF16)


def _proj_x(x2d, mod3, w, *, n, rows_per_mod, tm=1024, tn=1024, mm=256, rc=64):
    m, d = x2d.shape
    assert n == tn
    tiles_per_mod = rows_per_mod // tm
    return pl.pallas_call(
        functools.partial(_proj_x_kernel, mm, rc),
        grid=(n // tn, m // tm),
        in_specs=[
            pl.BlockSpec((tm, d), lambda j, i: (i, 0)),
            pl.BlockSpec((1, 1, d), lambda j, i: (i // tiles_per_mod, 0, 0)),
            pl.BlockSpec((1, 1, d), lambda j, i: (i // tiles_per_mod, 0, 1)),
            pl.BlockSpec((d, tn), lambda j, i: (0, j)),
        ],
        out_specs=[pl.BlockSpec((tm, tn), lambda j, i: (i, j)),
                   pl.BlockSpec((tm, d), lambda j, i: (i, 0))],
        out_shape=[jax.ShapeDtypeStruct((m, n), BF16), jax.ShapeDtypeStruct((m, d), BF16)],
        scratch_shapes=[pltpu.VMEM((d, tn), BF16)],
        compiler_params=pltpu.CompilerParams(
            dimension_semantics=("arbitrary", "arbitrary"), vmem_limit_bytes=VMEM_LIMIT),
        name="proj_x",
    )(x2d, mod3, mod3, w)


def _proj(h, w, gain, cos_t, sin_t, *, kind, col0, n, q_scale, tm=2048, tn=1024, rc=64):
    m, d = h.shape
    mm = 512 if kind == "raw" else 256
    tiles_per_seq = cos_t.shape[0] // tm
    j0 = col0 // tn
    kern = functools.partial(_proj_kernel, kind, mm, rc, q_scale)
    return pl.pallas_call(
        kern,
        grid=(n // tn, m // tm),
        in_specs=[
            pl.BlockSpec((tm, d), lambda j, i: (i, 0)),
            pl.BlockSpec((d, tn), lambda j, i: (0, j0 + j)),
            pl.BlockSpec((1, LANES), lambda j, i: (0, 0)),
            pl.BlockSpec((tm, LANES), lambda j, i: (i % tiles_per_seq, 0)),
            pl.BlockSpec((tm, LANES), lambda j, i: (i % tiles_per_seq, 0)),
        ],
        out_specs=pl.BlockSpec((tm, tn), lambda j, i: (i, j)),
        out_shape=jax.ShapeDtypeStruct((m, n), BF16),
        scratch_shapes=[pltpu.VMEM((d, tn), BF16)],
        compiler_params=pltpu.CompilerParams(
            dimension_semantics=("arbitrary", "arbitrary"), vmem_limit_bytes=VMEM_LIMIT),
        name="proj_" + kind,
    )(h, w, gain, cos_t, sin_t)


Q_ROWS = 2
BAND = 10
GROUP = 1
AHEAD = 2
N_SLOTS = (AHEAD + 1) * GROUP


def _band_start(r0, rows):
    return min(max(r0 - 4, 0), rows - BAND)


def _build_bias_tables(rpb_ref, bias_ref, rows, win_r, win_c):
    c_io = lax.broadcasted_iota(jnp.int32, (GRID_W, LANES), 0)
    l_io = lax.broadcasted_iota(jnp.int32, (GRID_W, LANES), 1)
    cs = jnp.clip(c_io - win_c // 2, 0, GRID_W - win_c)
    inwin = (l_io >= cs) & (l_io < cs + win_c) & (l_io < GRID_W)
    low = l_io < GRID_W
    neg = jnp.full((GRID_W, LANES), NEG, F32)
    toep = []
    for dr in range(2 * win_r - 1):
        row = jnp.broadcast_to(rpb_ref[0, dr:dr + 1, :], (GRID_W, LANES))
        t = pltpu.roll(row, LANES - (win_c - 1), 1, stride=1, stride_axis=0)
        toep.append(jnp.where(inwin, t * LOG2E, NEG))
    reps = [0, 2, 4, rows - 4, rows - 2]
    for tb, r0 in enumerate(reps):
        s0 = _band_start(r0, rows)
        assert (r0 - s0) == 2 * tb
        for rho in range(Q_ROWS):
            r = r0 + rho
            rs = min(max(r - win_r // 2, 0), rows - win_r)
            blocks = []
            for i in range(BAND):
                kr = s0 + i
                blocks.append(toep[kr - r + win_r - 1] if rs <= kr < rs + win_r else neg)
            for p in range(BAND // 2):
                tile = jnp.where(low, blocks[2 * p], pltpu.roll(blocks[2 * p + 1], GRID_W, 1))
                bias_ref[tb, rho * GRID_W:(rho + 1) * GRID_W, p * LANES:(p + 1) * LANES] = tile


def _attn_kernel(rows, win_r, win_c,
                 q_ref, k_ref, v_ref, z_ref, kc_ref, vc_ref, rpb_ref, o_ref, bias_ref, s_ref):
    @pl.when(pl.program_id(1) == 0)
    def _():
        _build_bias_tables(rpb_ref, bias_ref, rows, win_r, win_c)

    nq = Q_ROWS * GRID_W
    nk = BAND * GRID_W
    def scores(i, r0):
        s0 = _band_start(r0, rows)
        tb = (r0 - s0) // 2
        q0, k0 = r0 * GRID_W, s0 * GRID_W
        qb = q_ref[q0:q0 + nq, :]
        s_loc = _nt_dot(qb, k_ref[k0:k0 + nk, :]) + bias_ref[tb]
        s_ctx = _nt_dot(qb, kc_ref[...])
        slot = i % N_SLOTS
        s_ref[slot, :, :nk] = s_loc
        s_ref[slot, :, nk:] = s_ctx
        return jnp.maximum(jnp.max(s_loc, axis=-1, keepdims=True), jnp.max(s_ctx, axis=-1, keepdims=True))

    def finish(i, r0, m):
        q0, k0 = r0 * GRID_W, _band_start(r0, rows) * GRID_W
        p = jnp.exp2(s_ref[i % N_SLOTS] - m)
        den = jnp.sum(p, axis=-1, keepdims=True)
        pb = p.astype(BF16)
        o = (jnp.dot(pb[:, :nk], v_ref[k0:k0 + nk, :], preferred_element_type=F32)
             + jnp.dot(pb[:, nk:], vc_ref[...], preferred_element_type=F32))
        og = (o / den) * z_ref[q0:q0 + nq, :].astype(F32)
        o_ref[q0:q0 + nq, :] = og.astype(BF16)

    starts = list(range(0, rows, Q_ROWS))
    groups = [starts[g:g + GROUP] for g in range(0, len(starts), GROUP)]

    def issue(g):
        return [scores(g * GROUP + u, r0) for u, r0 in enumerate(groups[g])]

    pending = [issue(g) for g in range(AHEAD)]
    for g, grp in enumerate(groups):
        if g + AHEAD < len(groups):
            pending.append(issue(g + AHEAD))
        for u, (r0, m) in enumerate(zip(grp, pending.pop(0))):
            finish(g * GROUP + u, r0, m)


def _attention(q, k, v, z, kc, vc, rpb_pad, *, batch, seq, ctx_len, heads, win_r, win_c):
    rows = seq // GRID_W
    kern = functools.partial(_attn_kernel, rows, win_r, win_c)
    tok = pl.BlockSpec((seq, LANES), lambda h, b: (b, h))
    ctx = pl.BlockSpec((ctx_len, LANES), lambda h, b: (b, h))
    return pl.pallas_call(
        kern,
        grid=(heads, batch),
        in_specs=[tok, tok, tok, tok, ctx, ctx,
                  pl.BlockSpec((1,) + rpb_pad.shape[1:], lambda h, b: (h, 0, 0))],
        out_specs=tok,
        out_shape=jax.ShapeDtypeStruct((batch * seq, heads * LANES), BF16),
        scratch_shapes=[pltpu.VMEM((5, Q_ROWS * GRID_W, BAND * GRID_W), F32),
                        pltpu.VMEM((N_SLOTS, Q_ROWS * GRID_W, BAND * GRID_W + ctx_len), F32)],
        compiler_params=pltpu.CompilerParams(
            dimension_semantics=("arbitrary", "arbitrary"), vmem_limit_bytes=VMEM_LIMIT),
        name="attn",
    )(q, k, v, z, kc, vc, rpb_pad)


def _dft_mats(n):
    jk = (np.arange(n)[:, None] * np.arange(n)[None, :]) % n
    ang = 2.0 * np.pi * jk.astype(np.float64) / n
    return np.cos(ang) / np.sqrt(n), np.sin(ang) / np.sqrt(n)


def _fourier_kernel(gd, u_ref, zf_ref, csc_ref, ch_ref, sh_ref, nyq_ref, flip_ref, o_ref, a_ref, b_ref, e_ref):
    seq = u_ref.shape[0]
    half = seq // 2
    tk = ch_ref.shape[0]
    n_first = half // tk
    s = pl.program_id(1)

    @pl.when(s == 0)
    def _():
        rb = 512
        for g in range(u_ref.shape[1] // gd):
            for r in range(0, seq, rb):
                t = jnp.dot(u_ref[r:r + rb, g * gd:(g + 1) * gd], csc_ref[...], preferred_element_type=F32)
                a_ref[r:r + rb, g * gd:(g + 1) * gd] = t[:, :gd].astype(BF16)
                b_ref[r:r + rb, g * gd:(g + 1) * gd] = t[:, gd:].astype(BF16)
        e_ref[half:, :] = jnp.zeros((tk, e_ref.shape[1]), BF16)
        e_ref[half:half + nyq_ref.shape[0], :] = jnp.dot(
            nyq_ref[...], a_ref[...], preferred_element_type=F32).astype(BF16)

    rows = pl.ds(pl.multiple_of(s * tk, tk), tk)
    p = jnp.dot(ch_ref[...], a_ref[...], preferred_element_type=F32)
    q = jnp.dot(sh_ref[...], b_ref[...], preferred_element_type=F32)
    o_ref[rows, :] = ((p - q) * zf_ref[rows, :].astype(F32)).astype(BF16)
    e_ref[rows, :] = (p + q).astype(BF16)

    @pl.when(s == n_first - 1)
    def _():
        for t in range(n_first):
            base = half - (t + 1) * tk
            y = jnp.dot(flip_ref[...], e_ref[base:base + 2 * tk, :], preferred_element_type=F32)
            out = slice(half + t * tk, half + (t + 1) * tk)
            o_ref[out, :] = (y * zf_ref[out, :].astype(F32)).astype(BF16)


def _fourier(u, zf, csc, ch, sh, nyq, flip, *, batch, seq, gd, tk=256):
    fw = u.shape[1]
    half = seq // 2
    n_first = half // tk
    kern = functools.partial(_fourier_kernel, gd)
    per_batch = pl.BlockSpec((seq, fw), lambda b, k: (b, 0))
    half_rows = pl.BlockSpec((tk, seq), lambda b, k: (k, 0))
    whole = lambda a: pl.BlockSpec(a.shape, lambda b, k: (0, 0))
    return pl.pallas_call(
        kern,
        grid=(batch, n_first),
        in_specs=[per_batch, per_batch, whole(csc), half_rows, half_rows, whole(nyq), whole(flip)],
        out_specs=per_batch,
        out_shape=jax.ShapeDtypeStruct((batch * seq, fw), BF16),
        scratch_shapes=[pltpu.VMEM((seq, fw), BF16), pltpu.VMEM((seq, fw), BF16),
                        pltpu.VMEM((half + tk, fw), BF16)],
        compiler_params=pltpu.CompilerParams(
            dimension_semantics=("arbitrary", "arbitrary"), vmem_limit_bytes=VMEM_LIMIT),
        name="fourier",
    )(u, zf, csc, ch, sh, nyq, flip)


def _merge_kernel(yg_ref, og_ref, sgf_ref, sga_ref, x_ref, gate_ref, wf_ref, wa_ref, wo_ref, o_ref):
    yf = jnp.dot(yg_ref[...], wf_ref[...], preferred_element_type=F32)
    ya = jnp.dot(og_ref[...], wa_ref[...], preferred_element_type=F32)
    y = sgf_ref[...].astype(F32) * yf + sga_ref[...].astype(F32) * ya
    yo = jnp.dot(y.astype(BF16), wo_ref[...], preferred_element_type=F32)
    o_ref[...] = x_ref[...] + gate_ref[0] * yo


def _merge(yg, og, g, x2d, mod3, wf, wa, wo, *, seq, tm=256):
    m, d = x2d.shape
    tiles_per_seq = seq // tm
    const = lambda shape: pl.BlockSpec(shape, lambda i: (0, 0), pipeline_mode=pl.Buffered(1))
    return pl.pallas_call(
        _merge_kernel,
        grid=(m // tm,),
        in_specs=[
            pl.BlockSpec((tm, yg.shape[1]), lambda i: (i, 0)),
            pl.BlockSpec((tm, d), lambda i: (i, 0)),
            pl.BlockSpec((tm, d), lambda i: (i, 0)),
            pl.BlockSpec((tm, d), lambda i: (i, 1)),
            pl.BlockSpec((tm, d), lambda i: (i, 0)),
            pl.BlockSpec((1, 1, d), lambda i: (i // tiles_per_seq, 0, 2)),
            const(wf.shape), const(wa.shape), const(wo.shape),
        ],
        out_specs=pl.BlockSpec((tm, d), lambda i: (i, 0)),
        out_shape=jax.ShapeDtypeStruct((m, d), F32),
        compiler_params=pltpu.CompilerParams(
            dimension_semantics=("parallel",), vmem_limit_bytes=VMEM_LIMIT),
        name="merge",
    )(yg, og, g, g, x2d, mod3, wf, wa, wo)


def _rope_tables(seq, head_dim):
    n_freq = head_dim // 4
    t = np.arange(seq)
    pos = np.stack([t // GRID_W, t % GRID_W], axis=-1).astype(np.float32)
    inv_freq = (np.float32(ROPE_BASE) ** (-np.arange(n_freq, dtype=np.float32) / np.float32(n_freq)))
    ang = (pos[:, :, None] * inv_freq.astype(np.float32)).astype(np.float64)
    ang = np.broadcast_to(ang[:, None, :, :], (seq, 2, 2, n_freq))
    sign = np.array([-1.0, 1.0])[None, :, None, None]
    return (np.cos(ang).reshape(seq, head_dim).astype(np.float32),
            (np.sin(ang) * sign).reshape(seq, head_dim).astype(np.float32))


def _rope_lane_order(a, n_freq):
    lead = a.shape[:-1]
    return a.reshape(lead + (-1, 2, 2, n_freq)).swapaxes(-3, -2).reshape(a.shape)


def kernel(x, c, ctx, c_ctx, w_mod, b_mod, w_in, q_gain, k_gain, rpb, w_f_out, w_a_out, w_out):
    batch, seq, d = x.shape
    ctx_len = ctx.shape[1]
    depth, heads, n_dr, n_dc = rpb.shape
    assert depth == 1 and w_mod.shape[0] == 1
    head_dim = q_gain.shape[1]
    assert head_dim == LANES and seq % GRID_W == 0
    win_r, win_c = (n_dr + 1) // 2, (n_dc + 1) // 2
    attn_w = heads * head_dim
    fw = w_f_out.shape[1]
    gd = fw // F_GROUPS
    off_zf, off_q = fw, 2 * fw
    off_k, off_v, off_za = off_q + attn_w, off_q + 2 * attn_w, off_q + 3 * attn_w
    off_gf = off_za + attn_w
    off_ga = off_gf + d
    assert w_in.shape[2] == off_ga + d

    c_all = jnp.concatenate([c, c_ctx[None, :], jnp.zeros((16 - batch - 1, d), F32)], axis=0)
    mod = _mod(c_all, w_mod[0], b_mod)
    mod3 = mod.reshape(16, 1, 3 * d)

    x2d = x.reshape(batch * seq, d)
    u_f, h_x = _proj_x(x2d, mod3, w_in[0], n=fw, rows_per_mod=seq)
    h_c =_hnorm(ctx.reshape(batch * ctx_len, d), mod3, rows_per_mod=batch * ctx_len, mod_row0=batch)

    n_freq = head_dim // 4
    cos_np, sin_np = _rope_tables(seq, head_dim)
    cos_t, sin_t = jnp.asarray(cos_np), jnp.asarray(sin_np)
    qg = _rope_lane_order(q_gain, n_freq)
    kg = _rope_lane_order(k_gain, n_freq)
    q_scale = float(head_dim) ** -0.5 * LOG2E
    proj =functools.partial(_proj, w=w_in[0], cos_t=cos_t, sin_t=sin_t, q_scale=q_scale)

    z_f = proj(h_x, gain=qg, kind="silu", col0=off_zf, n=fw)
    q = proj(h_x, gain=qg, kind="q", col0=off_q, n=attn_w)
    k = proj(h_x, gain=kg, kind="k", col0=off_k, n=attn_w)
    v = proj(h_x, gain=qg, kind="raw", col0=off_v, n=attn_w)
    z_a = proj(h_x, gain=qg, kind="silu", col0=off_za, n=attn_w)
    g = proj(h_x, gain=qg, kind="sig", col0=off_gf, n=2 * d)
    k_c = proj(h_c, gain=kg, kind="kc", col0=off_k, n=attn_w)
    v_c = proj(h_c, gain=qg, kind="raw", col0=off_v, n=attn_w)

    rpb_pad = jnp.pad(rpb[0], ((0, 0), (0, 16 - n_dr), (0, LANES - n_dc)))
    og = _attention(q, k, v, z_a, k_c, v_c, rpb_pad, batch=batch, seq=seq, ctx_len=ctx_len, heads=heads,
                    win_r=win_r, win_c=win_c)

    cc, sc = _dft_mats(gd)
    cn, sn = _dft_mats(seq)
    const = lambda a: jnp.asarray(a.astype(np.float32)).astype(BF16)
    half, tk = seq // 2, 256
    nyq = np.zeros((16, seq))
    nyq[0] = cn[half]
    flip = np.zeros((tk, 2 * tk))
    flip[np.arange(tk), tk - np.arange(tk)] = 1.0
    yg = _fourier(u_f, z_f, const(np.concatenate([cc, sc], axis=1)), const(cn[:half]), const(sn[:half]),
                  const(nyq), const(flip), batch=batch, seq=seq, gd=gd, tk=tk)

    out = _merge(yg, og, g, x2d, mod3, w_f_out[0].astype(BF16), w_a_out[0].astype(BF16),
                 w_out[0].astype(BF16), seq=seq)
    return out.reshape(batch, seq, d)
```

```python
import functools

import numpy as np
import jax
import jax.numpy as jnp
from jax import lax
from jax.experimental import pallas as pl
from jax.experimental.pallas import tpu as pltpu

GRID_W = 64
F_GROUPS = 4
ROPE_BASE = 10000.0
EPS = 1e-6
NEG = -1e30
LOG2E = 1.4426950408889634
LANES = 128
VMEM_LIMIT = 56 * 1024 * 1024

BF16 = jnp.bfloat16
F32 = jnp.float32


def _nt_dot(a, b):
    return lax.dot_general(a, b, (((1,), (1,)), ((), ())), preferred_element_type=F32)


def _mod_kernel(c_ref, w_ref, b_ref, o_ref):
    a = jax.nn.silu(c_ref[...]).astype(BF16)
    o_ref[...] = jnp.dot(a, w_ref[...].astype(BF16), preferred_element_type=F32) + b_ref[...]


def _mod(c_all, w_mod, b_mod, tn=512):
    m, d = c_all.shape
    n = w_mod.shape[1]
    return pl.pallas_call(
        _mod_kernel,
        grid=(n // tn,),
        in_specs=[pl.BlockSpec((m, d), lambda j: (0, 0)),
                  pl.BlockSpec((d, tn), lambda j: (0, j)),
                  pl.BlockSpec((1, tn), lambda j: (0, j))],
        out_specs=pl.BlockSpec((m, tn), lambda j: (0, j)),
        out_shape=jax.ShapeDtypeStruct((m, n), F32),
        name="mod",
    )(c_all, w_mod, b_mod)


def _hnorm_kernel(rc, x_ref, shift_ref, scale_ref, o_ref):
    def body(t, carry):
        r = pl.multiple_of(t * rc, rc)
        xs = x_ref[pl.ds(r, rc), :]
        ms = jnp.mean(xs * xs, axis=-1, keepdims=True)
        h = xs * lax.rsqrt(ms + EPS) * (1.0 + scale_ref[0]) + shift_ref[0]
        o_ref[pl.ds(r, rc), :] = h.astype(BF16)
        return carry
    lax.fori_loop(0, x_ref.shape[0] // rc, body, 0)


def _hnorm(x2d, mod3, *, rows_per_mod, mod_row0, tm=512, rc=64):
    m, d = x2d.shape
    tiles_per_mod = rows_per_mod // tm
    return pl.pallas_call(
        functools.partial(_hnorm_kernel, rc),
        grid=(m // tm,),
        in_specs=[
            pl.BlockSpec((tm, d), lambda i: (i, 0)),
            pl.BlockSpec((1, 1, d), lambda i: (mod_row0 + i // tiles_per_mod, 0, 0)),
            pl.BlockSpec((1, 1, d), lambda i: (mod_row0 + i // tiles_per_mod, 0, 1)),
        ],
        out_specs=pl.BlockSpec((tm, d), lambda i: (i, 0)),
        out_shape=jax.ShapeDtypeStruct((m, d), BF16),
        compiler_params=pltpu.CompilerParams(dimension_semantics=("parallel",)),
        name="hnorm",
    )(x2d, mod3, mod3)


def _rope_lane_order_cols(w):
    n = w.shape[1]
    quarter = (lax.broadcasted_iota(jnp.int32, w.shape, 1) % LANES) // (LANES // 4)
    up = pltpu.roll(w, n - LANES // 4, 1)
    down = pltpu.roll(w, LANES // 4, 1)
    return jnp.where(quarter == 1, up, jnp.where(quarter == 2, down, w))


def _proj_kernel(kind, mm, rc, q_scale, h_ref, w_ref, gain_ref, cos_ref, sin_ref, o_ref, wb_ref):
    tm = h_ref.shape[0]
    tn = w_ref.shape[1]

    @pl.when(pl.program_id(1) == 0)
    def _():
        _cast_weight_tile(w_ref, wb_ref, rc, kind in ("q", "k", "kc"))

    for r in range(0, tm, mm):
        a = jnp.dot(h_ref[r:r + mm, :], wb_ref[...], preferred_element_type=F32)
        if kind == "raw":
            o_ref[r:r + mm, :] = a.astype(BF16)
        elif kind == "silu":
            o_ref[r:r + mm, :] = jax.nn.silu(a).astype(BF16)
        elif kind == "sig":
            o_ref[r:r + mm, :] = jax.nn.sigmoid(a).astype(BF16)
        else:
            gain = gain_ref[...]
            for hh in range(tn // LANES):
                xh = a[:, hh * LANES:(hh + 1) * LANES]
                ms = jnp.mean(xh * xh, axis=-1, keepdims=True)
                xn = xh * lax.rsqrt(ms + EPS) * gain
                if kind != "kc":
                    xn = xn * cos_ref[r:r + mm, :] + pltpu.roll(xn, LANES // 2, 1) * sin_ref[r:r + mm, :]
                if kind == "q":
                    xn = xn * q_scale
                o_ref[r:r + mm, hh * LANES:(hh + 1) * LANES] = xn.astype(BF16)


def _cast_weight_tile(w_ref, wb_ref, rc, reorder):
    def body(t, carry):
        r = pl.multiple_of(t * rc, rc)
        wt = w_ref[pl.ds(r, rc), :]
        if reorder:
            wt = _rope_lane_order_cols(wt)
        wb_ref[pl.ds(r, rc), :] = wt.astype(BF16)
        return carry
    lax.fori_loop(0, w_ref.shape[0] // rc, body, 0)


def _proj_x_kernel(mm, rc, x_ref, shift_ref, scale_ref, w_ref, o_ref, h_ref, wb_ref):
    @pl.when(pl.program_id(1) == 0)
    def _():
        _cast_weight_tile(w_ref, wb_ref, rc, False)

    for r in range(0, x_ref.shape[0], mm):
        xs = x_ref[r:r + mm, :]
        ms = jnp.mean(xs * xs, axis=-1, keepdims=True)
        h = (xs * lax.rsqrt(ms + EPS) * (1.0 + scale_ref[0]) + shift_ref[0]).astype(BF16)
        h_ref[r:r + mm, :] = h
        o_ref[r:r + mm, :] = jnp.dot(h, wb_ref[...], preferred_element_type=F32).astype(BF16)


def _proj_x(x2d, mod3, w, *, n, rows_per_mod, tm=1024, tn=1024, mm=256, rc=64):
    m, d = x2d.shape
    assert n == tn
    tiles_per_mod = rows_per_mod // tm
    return pl.pallas_call(
        functools.partial(_proj_x_kernel, mm, rc),
        grid=(n // tn, m // tm),
        in_specs=[
            pl.BlockSpec((tm, d), lambda j, i: (i, 0)),
            pl.BlockSpec((1, 1, d), lambda j, i: (i // tiles_per_mod, 0, 0)),
            pl.BlockSpec((1, 1, d), lambda j, i: (i // tiles_per_mod, 0, 1)),
            pl.BlockSpec((d, tn), lambda j, i: (0, j)),
        ],
        out_specs=[pl.BlockSpec((tm, tn), lambda j, i: (i, j)),
                   pl.BlockSpec((tm, d), lambda j, i: (i, 0))],
        out_shape=[jax.ShapeDtypeStruct((m, n), BF16), jax.ShapeDtypeStruct((m, d), BF16)],
        scratch_shapes=[pltpu.VMEM((d, tn), BF16)],
        compiler_params=pltpu.CompilerParams(
            dimension_semantics=("arbitrary", "arbitrary"), vmem_limit_bytes=VMEM_LIMIT),
        name="proj_x",
    )(x2d, mod3, mod3, w)


def _proj(h, w, gain, cos_t, sin_t, *, kind, col0, n, q_scale, tm=2048, tn=1024, rc=64):
    m, d = h.shape
    mm = 512 if kind == "raw" else 256
    tiles_per_seq = cos_t.shape[0] // tm
    j0 = col0 // tn
    kern = functools.partial(_proj_kernel, kind, mm, rc, q_scale)
    return pl.pallas_call(
        kern,
        grid=(n // tn, m // tm),
        in_specs=[
            pl.BlockSpec((tm, d), lambda j, i: (i, 0)),
            pl.BlockSpec((d, tn), lambda j, i: (0, j0 + j)),
            pl.BlockSpec((1, LANES), lambda j, i: (0, 0)),
            pl.BlockSpec((tm, LANES), lambda j, i: (i % tiles_per_seq, 0)),
            pl.BlockSpec((tm, LANES), lambda j, i: (i % tiles_per_seq, 0)),
        ],
        out_specs=pl.BlockSpec((tm, tn), lambda j, i: (i, j)),
        out_shape=jax.ShapeDtypeStruct((m, n), BF16),
        scratch_shapes=[pltpu.VMEM((d, tn), BF16)],
        compiler_params=pltpu.CompilerParams(
            dimension_semantics=("arbitrary", "arbitrary"), vmem_limit_bytes=VMEM_LIMIT),
        name="proj_" + kind,
    )(h, w, gain, cos_t, sin_t)


Q_ROWS = 2
BAND = 10
GROUP = 1
AHEAD = 2
N_SLOTS = (AHEAD + 1) * GROUP


def _band_start(r0, rows):
    return min(max(r0 - 4, 0), rows - BAND)


def _build_bias_tables(rpb_ref, bias_ref, rows, win_r, win_c):
    c_io = lax.broadcasted_iota(jnp.int32, (GRID_W, LANES), 0)
    l_io = lax.broadcasted_iota(jnp.int32, (GRID_W, LANES), 1)
    cs = jnp.clip(c_io - win_c // 2, 0, GRID_W - win_c)
    inwin = (l_io >= cs) & (l_io < cs + win_c) & (l_io < GRID_W)
    low = l_io < GRID_W
    neg = jnp.full((GRID_W, LANES), NEG, F32)
    toep = []
    for dr in range(2 * win_r - 1):
        row = jnp.broadcast_to(rpb_ref[0, dr:dr + 1, :], (GRID_W, LANES))
        t = pltpu.roll(row, LANES - (win_c - 1), 1, stride=1, stride_axis=0)
        toep.append(jnp.where(inwin, t * LOG2E, NEG))
    reps = [0, 2, 4, rows - 4, rows - 2]
    for tb, r0 in enumerate(reps):
        s0 = _band_start(r0, rows)
        assert (r0 - s0) == 2 * tb
        for rho in range(Q_ROWS):
            r = r0 + rho
            rs = min(max(r - win_r // 2, 0), rows - win_r)
            blocks = []
            for i in range(BAND):
                kr = s0 + i
                blocks.append(toep[kr - r + win_r - 1] if rs <= kr < rs + win_r else neg)
            for p in range(BAND // 2):
                tile = jnp.where(low, blocks[2 * p], pltpu.roll(blocks[2 * p + 1], GRID_W, 1))
                bias_ref[tb, rho * GRID_W:(rho + 1) * GRID_W, p * LANES:(p + 1) * LANES] = tile


def _attn_kernel(rows, win_r, win_c,
                 q_ref, k_ref, v_ref, z_ref, kc_ref, vc_ref, rpb_ref, o_ref, bias_ref, s_ref, va_ref, vca_ref):
    @pl.when(pl.program_id(1) == 0)
    def _():
        _build_bias_tables(rpb_ref, bias_ref, rows, win_r, win_c)

    va_ref[:, :LANES] = v_ref[...]
    va_ref[:, LANES:] = jnp.ones(v_ref.shape, BF16)
    vca_ref[:, :LANES] = vc_ref[...]
    vca_ref[:, LANES:] = jnp.ones(vc_ref.shape, BF16)

    nq = Q_ROWS * GRID_W
    nk = BAND * GRID_W

    def scores(i, r0):
        s0 = _band_start(r0, rows)
        tb = (r0 - s0) // 2
        q0, k0 = r0 * GRID_W, s0 * GRID_W
        qb = q_ref[q0:q0 + nq, :]
        s_loc = _nt_dot(qb, k_ref[k0:k0 + nk, :]) + bias_ref[tb]
        s_ctx = _nt_dot(qb, kc_ref[...])
        slot = i % N_SLOTS
        s_ref[slot, :, :nk] = s_loc
        s_ref[slot, :, nk:] = s_ctx
        return jnp.maximum(jnp.max(s_loc, axis=-1, keepdims=True), jnp.max(s_ctx, axis=-1, keepdims=True))

    def finish(i, r0, m):
        q0, k0 = r0 * GRID_W, _band_start(r0, rows) * GRID_W
        pb = jnp.exp2(s_ref[i % N_SLOTS] - m).astype(BF16)
        o = (jnp.dot(pb[:, :nk], va_ref[k0:k0 + nk, :], preferred_element_type=F32)
             + jnp.dot(pb[:, nk:], vca_ref[...], preferred_element_type=F32))
        og = (o[:, :LANES] / o[:, LANES:]) * z_ref[q0:q0 + nq, :].astype(F32)
        o_ref[q0:q0 + nq, :] = og.astype(BF16)

    starts = list(range(0, rows, Q_ROWS))
    groups = [starts[g:g + GROUP] for g in range(0, len(starts), GROUP)]

    def issue(g):
        return [scores(g * GROUP + u, r0) for u, r0 in enumerate(groups[g])]

    pending = [issue(g) for g in range(AHEAD)]
    for g, grp in enumerate(groups):
        if g + AHEAD < len(groups):
            pending.append(issue(g + AHEAD))
        for u, (r0, m) in enumerate(zip(grp, pending.pop(0))):
            finish(g * GROUP + u, r0, m)


def _attention(q, k, v, z, kc, vc, rpb_pad, *, batch, seq, ctx_len, heads, win_r, win_c):
    rows = seq // GRID_W
    kern = functools.partial(_attn_kernel, rows, win_r, win_c)
    tok = pl.BlockSpec((seq, LANES), lambda h, b: (b, h))
    ctx = pl.BlockSpec((ctx_len, LANES), lambda h, b: (b, h))
    return pl.pallas_call(
        kern,
        grid=(heads, batch),
        in_specs=[tok, tok, tok, tok, ctx, ctx,
                  pl.BlockSpec((1,) + rpb_pad.shape[1:], lambda h, b: (h, 0, 0))],
        out_specs=tok,
        out_shape=jax.ShapeDtypeStruct((batch * seq, heads * LANES), BF16),
        scratch_shapes=[pltpu.VMEM((5, Q_ROWS * GRID_W, BAND * GRID_W), F32),
                        pltpu.VMEM((N_SLOTS, Q_ROWS * GRID_W, BAND * GRID_W + ctx_len), F32),
                        pltpu.VMEM((seq, 2 * LANES), BF16), pltpu.VMEM((ctx_len, 2 * LANES), BF16)],
        compiler_params=pltpu.CompilerParams(
            dimension_semantics=("arbitrary", "arbitrary"), vmem_limit_bytes=VMEM_LIMIT),
        name="attn",
    )(q, k, v, z, kc, vc, rpb_pad)


def _dft_mats(n):
    jk = (np.arange(n)[:, None] * np.arange(n)[None, :]) % n
    ang = 2.0 * np.pi * jk.astype(np.float64) / n
    return np.cos(ang) / np.sqrt(n), np.sin(ang) / np.sqrt(n)


def _fourier_kernel(gd, u_ref, zf_ref, csc_ref, ch_ref, sh_ref, nyq_ref, flip_ref, o_ref, a_ref, b_ref, e_ref):
    seq = u_ref.shape[0]
    half = seq // 2
    tk = ch_ref.shape[0]
    n_first = half // tk
    s = pl.program_id(1)

    @pl.when(s == 0)
    def _():
        rb = 512
        for g in range(u_ref.shape[1] // gd):
            for r in range(0, seq, rb):
                t = jnp.dot(u_ref[r:r + rb, g * gd:(g + 1) * gd], csc_ref[...], preferred_element_type=F32)
                a_ref[r:r + rb, g * gd:(g + 1) * gd] = t[:, :gd].astype(BF16)
                b_ref[r:r + rb, g * gd:(g + 1) * gd] = t[:, gd:].astype(BF16)
        e_ref[half:, :] = jnp.zeros((tk, e_ref.shape[1]), BF16)
        e_ref[half:half + nyq_ref.shape[0], :] = jnp.dot(
            nyq_ref[...], a_ref[...], preferred_element_type=F32).astype(BF16)

    rows = pl.ds(pl.multiple_of(s * tk, tk), tk)
    p = jnp.dot(ch_ref[...], a_ref[...], preferred_element_type=F32)
    q = jnp.dot(sh_ref[...], b_ref[...], preferred_element_type=F32)
    o_ref[rows, :] = ((p - q) * zf_ref[rows, :].astype(F32)).astype(BF16)
    e_ref[rows, :] = (p + q).astype(BF16)

    @pl.when(s == n_first - 1)
    def _():
        for t in range(n_first):
            base = half - (t + 1) * tk
            y = jnp.dot(flip_ref[...], e_ref[base:base + 2 * tk, :], preferred_element_type=F32)
            out = slice(half + t * tk, half + (t + 1) * tk)
            o_ref[out, :] = (y * zf_ref[out, :].astype(F32)).astype(BF16)


def _fourier(u, zf, csc, ch, sh, nyq, flip, *, batch, seq, gd, tk=256):
    fw = u.shape[1]
    half = seq // 2
    n_first = half // tk
    kern = functools.partial(_fourier_kernel, gd)
    per_batch = pl.BlockSpec((seq, fw), lambda b, k: (b, 0))
    half_rows = pl.BlockSpec((tk, seq), lambda b, k: (k, 0))
    whole = lambda a: pl.BlockSpec(a.shape, lambda b, k: (0, 0))
    return pl.pallas_call(
        kern,
        grid=(batch, n_first),
        in_specs=[per_batch, per_batch, whole(csc), half_rows, half_rows, whole(nyq), whole(flip)],
        out_specs=per_batch,
        out_shape=jax.ShapeDtypeStruct((batch * seq, fw), BF16),
        scratch_shapes=[pltpu.VMEM((seq, fw), BF16), pltpu.VMEM((seq, fw), BF16),
                        pltpu.VMEM((half + tk, fw), BF16)],
        compiler_params=pltpu.CompilerParams(
            dimension_semantics=("arbitrary", "arbitrary"), vmem_limit_bytes=VMEM_LIMIT),
        name="fourier",
    )(u, zf, csc, ch, sh, nyq, flip)


def _merge_kernel(yg_ref, og_ref, sgf_ref, sga_ref, x_ref, gate_ref, wf_ref, wa_ref, wo_ref, o_ref):
    yf = jnp.dot(yg_ref[...], wf_ref[...], preferred_element_type=F32)
    ya = jnp.dot(og_ref[...], wa_ref[...], preferred_element_type=F32)
    y = sgf_ref[...].astype(F32) * yf + sga_ref[...].astype(F32) * ya
    yo = jnp.dot(y.astype(BF16), wo_ref[...], preferred_element_type=F32)
    o_ref[...] = x_ref[...] + gate_ref[0] * yo


def _merge(yg, og, g, x2d, mod3, wf, wa, wo, *, seq, tm=256):
    m, d = x2d.shape
    tiles_per_seq = seq // tm
    const = lambda shape: pl.BlockSpec(shape, lambda i: (0, 0), pipeline_mode=pl.Buffered(1))
    return pl.pallas_call(
        _merge_kernel,
        grid=(m // tm,),
        in_specs=[
            pl.BlockSpec((tm, yg.shape[1]), lambda i: (i, 0)),
            pl.BlockSpec((tm, d), lambda i: (i, 0)),
            pl.BlockSpec((tm, d), lambda i: (i, 0)),
            pl.BlockSpec((tm, d), lambda i: (i, 1)),
            pl.BlockSpec((tm, d), lambda i: (i, 0)),
            pl.BlockSpec((1, 1, d), lambda i: (i // tiles_per_seq, 0, 2)),
            const(wf.shape), const(wa.shape), const(wo.shape),
        ],
        out_specs=pl.BlockSpec((tm, d), lambda i: (i, 0)),
        out_shape=jax.ShapeDtypeStruct((m, d), F32),
        compiler_params=pltpu.CompilerParams(
            dimension_semantics=("parallel",), vmem_limit_bytes=VMEM_LIMIT),
        name="merge",
    )(yg, og, g, g, x2d, mod3, wf, wa, wo)


def _rope_tables(seq, head_dim):
    n_freq = head_dim // 4
    t = np.arange(seq)
    pos = np.stack([t // GRID_W, t % GRID_W], axis=-1).astype(np.float32)
    inv_freq = (np.float32(ROPE_BASE) ** (-np.arange(n_freq, dtype=np.float32) / np.float32(n_freq)))
    ang = (pos[:, :, None] * inv_freq.astype(np.float32)).astype(np.float64)
    ang = np.broadcast_to(ang[:, None, :, :], (seq, 2, 2, n_freq))
    sign = np.array([-1.0, 1.0])[None, :, None, None]
    return (np.cos(ang).reshape(seq, head_dim).astype(np.float32),
            (np.sin(ang) * sign).reshape(seq, head_dim).astype(np.float32))


def _rope_lane_order(a, n_freq):
    lead = a.shape[:-1]
    return a.reshape(lead + (-1, 2, 2, n_freq)).swapaxes(-3, -2).reshape(a.shape)


def kernel(x, c, ctx, c_ctx, w_mod, b_mod, w_in, q_gain, k_gain, rpb, w_f_out, w_a_out, w_out):
    batch, seq, d = x.shape
    ctx_len = ctx.shape[1]
    depth, heads, n_dr, n_dc = rpb.shape
    assert depth == 1 and w_mod.shape[0] == 1
    head_dim = q_gain.shape[1]
    assert head_dim == LANES and seq % GRID_W == 0
    win_r, win_c = (n_dr + 1) // 2, (n_dc + 1) // 2
    attn_w = heads * head_dim
    fw = w_f_out.shape[1]
    gd = fw // F_GROUPS
    off_zf, off_q = fw, 2 * fw
    off_k, off_v, off_za = off_q + attn_w, off_q + 2 * attn_w, off_q + 3 * attn_w
    off_gf = off_za + attn_w
    off_ga = off_gf + d
    assert w_in.shape[2] == off_ga + d

    c_all = jnp.concatenate([c, c_ctx[None, :], jnp.zeros((16 - batch - 1, d), F32)], axis=0)
    mod = _mod(c_all, w_mod[0], b_mod)
    mod3 = mod.reshape(16, 1, 3 * d)

    x2d = x.reshape(batch * seq, d)
    u_f, h_x = _proj_x(x2d, mod3, w_in[0], n=fw, rows_per_mod=seq)
    h_c =_hnorm(ctx.reshape(batch * ctx_len, d), mod3, rows_per_mod=batch * ctx_len, mod_row0=batch)

    n_freq = head_dim // 4
    cos_np, sin_np = _rope_tables(seq, head_dim)
    cos_t, sin_t = jnp.asarray(cos_np), jnp.asarray(sin_np)
    qg = _rope_lane_order(q_gain, n_freq)
    kg = _rope_lane_order(k_gain, n_freq)
    q_scale = float(head_dim) ** -0.5 * LOG2E
    proj =functools.partial(_proj, w=w_in[0], cos_t=cos_t, sin_t=sin_t, q_scale=q_scale)

    z_f = proj(h_x, gain=qg, kind="silu", col0=off_zf, n=fw)
    q = proj(h_x, gain=qg, kind="q", col0=off_q, n=attn_w)
    k = proj(h_x, gain=kg, kind="k", col0=off_k, n=attn_w)
    v = proj(h_x, gain=qg, kind="raw", col0=off_v, n=attn_w)
    z_a = proj(h_x, gain=qg, kind="silu", col0=off_za, n=attn_w)
    g = proj(h_x, gain=qg, kind="sig", col0=off_gf, n=2 * d)
    k_c = proj(h_c, gain=kg, kind="kc", col0=off_k, n=attn_w)
    v_c = proj(h_c, gain=qg, kind="raw", col0=off_v, n=attn_w)

    rpb_pad = jnp.pad(rpb[0], ((0, 0), (0, 16 - n_dr), (0, LANES - n_dc)))
    og = _attention(q, k, v, z_a, k_c, v_c, rpb_pad, batch=batch, seq=seq, ctx_len=ctx_len, heads=heads,
                    win_r=win_r, win_c=win_c)

    cc, sc = _dft_mats(gd)
    cn, sn = _dft_mats(seq)
    const = lambda a: jnp.asarray(a.astype(np.float32)).astype(BF16)
    half, tk = seq // 2, 256
    nyq = np.zeros((16, seq))
    nyq[0] = cn[half]
    flip = np.zeros((tk, 2 * tk))
    flip[np.arange(tk), tk - np.arange(tk)] = 1.0
    yg = _fourier(u_f, z_f, const(np.concatenate([cc, sc], axis=1)), const(cn[:half]), const(sn[:half]),
                  const(nyq), const(flip), batch=batch, seq=seq, gd=gd, tk=tk)

    out = _merge(yg, og, g, x2d, mod3, w_f_out[0].astype(BF16), w_a_out[0].astype(BF16),
                 w_out[0].astype(BF16), seq=seq)
    return out.reshape(batch, seq, d)
```

```python
import functools

import numpy as np
import jax
import jax.numpy as jnp
from jax import lax
from jax.experimental import pallas as pl
from jax.experimental.pallas import tpu as pltpu

GRID_W = 64
F_GROUPS = 4
ROPE_BASE = 10000.0
EPS = 1e-6
NEG = -1e30
LOG2E = 1.4426950408889634
LANES = 128
VMEM_LIMIT = 56 * 1024 * 1024

BF16 = jnp.bfloat16
F32 = jnp.float32


def _nt_dot(a, b):
    return lax.dot_general(a, b, (((1,), (1,)), ((), ())), preferred_element_type=F32)


def _mod_kernel(c_ref, w_ref, b_ref, o_ref):
    a = jax.nn.silu(c_ref[...]).astype(BF16)
    o_ref[...] = jnp.dot(a, w_ref[...].astype(BF16), preferred_element_type=F32) + b_ref[...]


def _mod(c_all, w_mod, b_mod, tn=512):
    m, d = c_all.shape
    n = w_mod.shape[1]
    return pl.pallas_call(
        _mod_kernel,
        grid=(n // tn,),
        in_specs=[pl.BlockSpec((m, d), lambda j: (0, 0)),
                  pl.BlockSpec((d, tn), lambda j: (0, j)),
                  pl.BlockSpec((1, tn), lambda j: (0, j))],
        out_specs=pl.BlockSpec((m, tn), lambda j: (0, j)),
        out_shape=jax.ShapeDtypeStruct((m, n), F32),
        name="mod",
    )(c_all, w_mod, b_mod)


def _hnorm_kernel(rc, x_ref, shift_ref, scale_ref, o_ref):
    def body(t, carry):
        r = pl.multiple_of(t * rc, rc)
        xs = x_ref[pl.ds(r, rc), :]
        ms = jnp.mean(xs * xs, axis=-1, keepdims=True)
        h = xs * lax.rsqrt(ms + EPS) * (1.0 + scale_ref[0]) + shift_ref[0]
        o_ref[pl.ds(r, rc), :] = h.astype(BF16)
        return carry
    lax.fori_loop(0, x_ref.shape[0] // rc, body, 0)


def _hnorm(x2d, mod3, *, rows_per_mod, mod_row0, tm=512, rc=64):
    m, d = x2d.shape
    tiles_per_mod = rows_per_mod // tm
    return pl.pallas_call(
        functools.partial(_hnorm_kernel, rc),
        grid=(m // tm,),
        in_specs=[
            pl.BlockSpec((tm, d), lambda i: (i, 0)),
            pl.BlockSpec((1, 1, d), lambda i: (mod_row0 + i // tiles_per_mod, 0, 0)),
            pl.BlockSpec((1, 1, d), lambda i: (mod_row0 + i // tiles_per_mod, 0, 1)),
        ],
        out_specs=pl.BlockSpec((tm, d), lambda i: (i, 0)),
        out_shape=jax.ShapeDtypeStruct((m, d), BF16),
        compiler_params=pltpu.CompilerParams(dimension_semantics=("parallel",)),
        name="hnorm",
    )(x2d, mod3, mod3)


def _rope_lane_order_cols(w):
    n = w.shape[1]
    quarter = (lax.broadcasted_iota(jnp.int32, w.shape, 1) % LANES) // (LANES // 4)
    up = pltpu.roll(w, n - LANES // 4, 1)
    down = pltpu.roll(w, LANES // 4, 1)
    return jnp.where(quarter == 1, up, jnp.where(quarter == 2, down, w))


def _proj_kernel(kind, mm, rc, q_scale, h_ref, w_ref, gain_ref, cos_ref, sin_ref, o_ref, wb_ref):
    tm = h_ref.shape[0]
    tn = w_ref.shape[1]

    @pl.when(pl.program_id(1) == 0)
    def _():
        _cast_weight_tile(w_ref, wb_ref, rc, kind in ("q", "k", "kc"))

    for r in range(0, tm, mm):
        a = jnp.dot(h_ref[r:r + mm, :], wb_ref[...], preferred_element_type=F32)
        if kind == "raw":
            o_ref[r:r + mm, :] = a.astype(BF16)
        elif kind == "silu":
            o_ref[r:r + mm, :] = jax.nn.silu(a).astype(BF16)
        elif kind == "sig":
            o_ref[r:r + mm, :] = jax.nn.sigmoid(a).astype(BF16)
        else:
            gain = gain_ref[...]
            for hh in range(tn // LANES):
                xh = a[:, hh * LANES:(hh + 1) * LANES]
                ms = jnp.mean(xh * xh, axis=-1, keepdims=True)
                xn = xh * lax.rsqrt(ms + EPS) * gain
                if kind != "kc":
                    xn = xn * cos_ref[r:r + mm, :] + pltpu.roll(xn, LANES // 2, 1) * sin_ref[r:r + mm, :]
                if kind == "q":
                    xn = xn * q_scale
                o_ref[r:r + mm, hh * LANES:(hh + 1) * LANES] = xn.astype(BF16)


def _cast_weight_tile(w_ref, wb_ref, rc, reorder):
    def body(t, carry):
        r = pl.multiple_of(t * rc, rc)
        wt = w_ref[pl.ds(r, rc), :]
        if reorder:
            wt = _rope_lane_order_cols(wt)
        wb_ref[pl.ds(r, rc), :] = wt.astype(BF16)
        return carry
    lax.fori_loop(0, w_ref.shape[0] // rc, body, 0)


def _proj_x_kernel(mm, rc, x_ref, shift_ref, scale_ref, w_ref, o_ref, h_ref, wb_ref):
    @pl.when(pl.program_id(1) == 0)
    def _():
        _cast_weight_tile(w_ref, wb_ref, rc, False)

    for r in range(0, x_ref.shape[0], mm):
        xs = x_ref[r:r + mm, :]
        ms = jnp.mean(xs * xs, axis=-1, keepdims=True)
        h = (xs * lax.rsqrt(ms + EPS) * (1.0 + scale_ref[0]) + shift_ref[0]).astype(BF16)
        h_ref[r:r + mm, :] = h
        o_ref[r:r + mm, :] = jnp.dot(h, wb_ref[...], preferred_element_type=F32).astype(BF16)


def _proj_x(x2d, mod3, w, *, n, rows_per_mod, tm=1024, tn=1024, mm=256, rc=64):
    m, d = x2d.shape
    assert n == tn
    tiles_per_mod = rows_per_mod // tm
    return pl.pallas_call(
        functools.partial(_proj_x_kernel, mm, rc),
        grid=(n // tn, m // tm),
        in_specs=[
            pl.BlockSpec((tm, d), lambda j, i: (i, 0)),
            pl.BlockSpec((1, 1, d), lambda j, i: (i // tiles_per_mod, 0, 0)),
            pl.BlockSpec((1, 1, d), lambda j, i: (i // tiles_per_mod, 0, 1)),
            pl.BlockSpec((d, tn), lambda j, i: (0, j)),
        ],
        out_specs=[pl.BlockSpec((tm, tn), lambda j, i: (i, j)),
                   pl.BlockSpec((tm, d), lambda j, i: (i, 0))],
        out_shape=[jax.ShapeDtypeStruct((m, n), BF16), jax.ShapeDtypeStruct((m, d), BF16)],
        scratch_shapes=[pltpu.VMEM((d, tn), BF16)],
        compiler_params=pltpu.CompilerParams(
            dimension_semantics=("arbitrary", "arbitrary"), vmem_limit_bytes=VMEM_LIMIT),
        name="proj_x",
    )(x2d, mod3, mod3, w)


def _proj(h, w, gain, cos_t, sin_t, *, kind, col0, n, q_scale, tm=2048, tn=1024, rc=64):
    m, d = h.shape
    mm = 512 if kind == "raw" else 256
    tiles_per_seq = cos_t.shape[0] // tm
    j0 = col0 // tn
    kern = functools.partial(_proj_kernel, kind, mm, rc, q_scale)
    return pl.pallas_call(
        kern,
        grid=(n // tn, m // tm),
        in_specs=[
            pl.BlockSpec((tm, d), lambda j, i: (i, 0)),
            pl.BlockSpec((d, tn), lambda j, i: (0, j0 + j)),
            pl.BlockSpec((1, LANES), lambda j, i: (0, 0)),
            pl.BlockSpec((tm, LANES), lambda j, i: (i % tiles_per_seq, 0)),
            pl.BlockSpec((tm, LANES), lambda j, i: (i % tiles_per_seq, 0)),
        ],
        out_specs=pl.BlockSpec((tm, tn), lambda j, i: (i, j)),
        out_shape=jax.ShapeDtypeStruct((m, n), BF16),
        scratch_shapes=[pltpu.VMEM((d, tn), BF16)],
        compiler_params=pltpu.CompilerParams(
            dimension_semantics=("arbitrary", "arbitrary"), vmem_limit_bytes=VMEM_LIMIT),
        name="proj_" + kind,
    )(h, w, gain, cos_t, sin_t)


Q_ROWS = 4
BAND = 12
GROUP = 1
AHEAD = 2
N_SLOTS = (AHEAD + 1) * GROUP


def _band_start(r0, rows):
    return min(max(r0 - 4, 0), rows - BAND)


def _band_offsets(rows):
    return sorted({r0 - _band_start(r0, rows) for r0 in range(0, rows, Q_ROWS)})


def _valid_slots(r, s0, rows, win_r):
    rs = min(max(r - win_r // 2, 0), rows - win_r)
    return tuple(rs <= s0 + i < rs + win_r for i in range(BAND))


def _build_bias_tables(rpb_ref, bias_ref, rows, win_r, win_c):
    c_io = lax.broadcasted_iota(jnp.int32, (GRID_W, LANES), 0)
    l_io = lax.broadcasted_iota(jnp.int32, (GRID_W, LANES), 1)
    cs = jnp.clip(c_io - win_c // 2, 0, GRID_W - win_c)
    inwin = (l_io >= cs) & (l_io < cs + win_c) & (l_io < GRID_W)
    low = l_io < GRID_W
    neg = jnp.full((GRID_W, LANES), NEG, F32)
    toep = []
    for dr in range(2 * win_r - 1):
        row = jnp.broadcast_to(rpb_ref[0, dr:dr + 1, :], (GRID_W, LANES))
        t = pltpu.roll(row, LANES - (win_c - 1), 1, stride=1, stride_axis=0)
        toep.append(jnp.where(inwin, t * LOG2E, NEG))
    starts = range(0, rows, Q_ROWS)
    for tb, off in enumerate(_band_offsets(rows)):
        same = [r0 for r0 in starts if r0 - _band_start(r0, rows) == off]
        r0, s0 = same[0], _band_start(same[0], rows)
        for rho in range(Q_ROWS):
            r = r0 + rho
            valid = _valid_slots(r, s0, rows, win_r)
            assert all(_valid_slots(o + rho, _band_start(o, rows), rows, win_r) == valid for o in same)
            blocks = [toep[s0 + i - r + win_r - 1] if valid[i] else neg for i in range(BAND)]
            for p in range(BAND // 2):
                tile = jnp.where(low, blocks[2 * p], pltpu.roll(blocks[2 * p + 1], GRID_W, 1))
                bias_ref[tb, rho * GRID_W:(rho + 1) * GRID_W, p * LANES:(p + 1) * LANES] = tile


def _attn_kernel(rows, win_r, win_c,
                 q_ref, k_ref, v_ref, z_ref, kc_ref, vc_ref, rpb_ref, o_ref, bias_ref, s_ref):
    @pl.when(pl.program_id(1) == 0)
    def _():
        _build_bias_tables(rpb_ref, bias_ref, rows, win_r, win_c)

    nq = Q_ROWS * GRID_W
    nk = BAND * GRID_W

    def scores(i, r0):
        s0 = _band_start(r0, rows)
        tb = _band_offsets(rows).index(r0 - s0)
        q0, k0 = r0 * GRID_W, s0 * GRID_W
        qb = q_ref[q0:q0 + nq, :]
        s_loc = _nt_dot(qb, k_ref[k0:k0 + nk, :]) + bias_ref[tb]
        s_ctx = _nt_dot(qb, kc_ref[...])
        slot = i % N_SLOTS
        s_ref[slot, :, :nk] = s_loc
        s_ref[slot, :, nk:] = s_ctx
        return jnp.maximum(jnp.max(s_loc, axis=-1, keepdims=True), jnp.max(s_ctx, axis=-1, keepdims=True))

    def finish(i, r0, m):
        q0, k0 = r0 * GRID_W, _band_start(r0, rows) * GRID_W
        p = jnp.exp2(s_ref[i % N_SLOTS] - m)
        den = jnp.sum(p, axis=-1, keepdims=True)
        pb = p.astype(BF16)
        o = (jnp.dot(pb[:, :nk], v_ref[k0:k0 + nk, :], preferred_element_type=F32)
             + jnp.dot(pb[:, nk:], vc_ref[...], preferred_element_type=F32))
        og = (o / den) * z_ref[q0:q0 + nq, :].astype(F32)
        o_ref[q0:q0 + nq, :] = og.astype(BF16)

    starts = list(range(0, rows, Q_ROWS))
    groups = [starts[g:g + GROUP] for g in range(0, len(starts), GROUP)]

    def issue(g):
        return [scores(g * GROUP + u, r0) for u, r0 in enumerate(groups[g])]

    pending = [issue(g) for g in range(AHEAD)]
    for g, grp in enumerate(groups):
        if g + AHEAD < len(groups):
            pending.append(issue(g + AHEAD))
        for u, (r0, m) in enumerate(zip(grp, pending.pop(0))):
            finish(g * GROUP + u, r0, m)


def _attention(q, k, v, z, kc, vc, rpb_pad, *, batch, seq, ctx_len, heads, win_r, win_c):
    rows = seq // GRID_W
    kern = functools.partial(_attn_kernel, rows, win_r, win_c)
    tok = pl.BlockSpec((seq, LANES), lambda h, b: (b, h))
    ctx = pl.BlockSpec((ctx_len, LANES), lambda h, b: (b, h))
    return pl.pallas_call(
        kern,
        grid=(heads, batch),
        in_specs=[tok, tok, tok, tok, ctx, ctx,
                  pl.BlockSpec((1,) + rpb_pad.shape[1:], lambda h, b: (h, 0, 0))],
        out_specs=tok,
        out_shape=jax.ShapeDtypeStruct((batch * seq, heads * LANES), BF16),
        scratch_shapes=[pltpu.VMEM((len(_band_offsets(rows)), Q_ROWS * GRID_W, BAND * GRID_W), F32),
                        pltpu.VMEM((N_SLOTS, Q_ROWS * GRID_W, BAND * GRID_W + ctx_len), F32)],
        compiler_params=pltpu.CompilerParams(
            dimension_semantics=("arbitrary", "arbitrary"), vmem_limit_bytes=VMEM_LIMIT),
        name="attn",
    )(q, k, v, z, kc, vc, rpb_pad)


def _dft_mats(n):
    jk = (np.arange(n)[:, None] * np.arange(n)[None, :]) % n
    ang = 2.0 * np.pi * jk.astype(np.float64) / n
    return np.cos(ang) / np.sqrt(n), np.sin(ang) / np.sqrt(n)


def _fourier_kernel(gd, u_ref, zf_ref, csc_ref, ch_ref, sh_ref, nyq_ref, flip_ref, o_ref, a_ref, b_ref, e_ref):
    seq = u_ref.shape[0]
    half = seq // 2
    tk = ch_ref.shape[0]
    n_first = half // tk
    s = pl.program_id(1)

    @pl.when(s == 0)
    def _():
        rb = 512
        for g in range(u_ref.shape[1] // gd):
            for r in range(0, seq, rb):
                t = jnp.dot(u_ref[r:r + rb, g * gd:(g + 1) * gd], csc_ref[...], preferred_element_type=F32)
                a_ref[r:r + rb, g * gd:(g + 1) * gd] = t[:, :gd].astype(BF16)
                b_ref[r:r + rb, g * gd:(g + 1) * gd] = t[:, gd:].astype(BF16)
        e_ref[half:, :] = jnp.zeros((tk, e_ref.shape[1]), BF16)
        e_ref[half:half + nyq_ref.shape[0], :] = jnp.dot(
            nyq_ref[...], a_ref[...], preferred_element_type=F32).astype(BF16)

    rows = pl.ds(pl.multiple_of(s * tk, tk), tk)
    p = jnp.dot(ch_ref[...], a_ref[...], preferred_element_type=F32)
    q = jnp.dot(sh_ref[...], b_ref[...], preferred_element_type=F32)
    o_ref[rows, :] = ((p - q) * zf_ref[rows, :].astype(F32)).astype(BF16)
    e_ref[rows, :] = (p + q).astype(BF16)

    @pl.when(s == n_first - 1)
    def _():
        for t in range(n_first):
            base = half - (t + 1) * tk
            y = jnp.dot(flip_ref[...], e_ref[base:base + 2 * tk, :], preferred_element_type=F32)
            out = slice(half + t * tk, half + (t + 1) * tk)
            o_ref[out, :] = (y * zf_ref[out, :].astype(F32)).astype(BF16)


def _fourier(u, zf, csc, ch, sh, nyq, flip, *, batch, seq, gd, tk=256):
    fw = u.shape[1]
    half = seq // 2
    n_first = half // tk
    kern = functools.partial(_fourier_kernel, gd)
    per_batch = pl.BlockSpec((seq, fw), lambda b, k: (b, 0))
    half_rows = pl.BlockSpec((tk, seq), lambda b, k: (k, 0))
    whole = lambda a: pl.BlockSpec(a.shape, lambda b, k: (0, 0))
    return pl.pallas_call(
        kern,
        grid=(batch, n_first),
        in_specs=[per_batch, per_batch, whole(csc), half_rows, half_rows, whole(nyq), whole(flip)],
        out_specs=per_batch,
        out_shape=jax.ShapeDtypeStruct((batch * seq, fw), BF16),
        scratch_shapes=[pltpu.VMEM((seq, fw), BF16), pltpu.VMEM((seq, fw), BF16),
                        pltpu.VMEM((half + tk, fw), BF16)],
        compiler_params=pltpu.CompilerParams(
            dimension_semantics=("arbitrary", "arbitrary"), vmem_limit_bytes=VMEM_LIMIT),
        name="fourier",
    )(u, zf, csc, ch, sh, nyq, flip)


def _merge_kernel(yg_ref, og_ref, sgf_ref, sga_ref, x_ref, gate_ref, wf_ref, wa_ref, wo_ref, o_ref):
    yf = jnp.dot(yg_ref[...], wf_ref[...], preferred_element_type=F32)
    ya = jnp.dot(og_ref[...], wa_ref[...], preferred_element_type=F32)
    y = sgf_ref[...].astype(F32) * yf + sga_ref[...].astype(F32) * ya
    yo = jnp.dot(y.astype(BF16), wo_ref[...], preferred_element_type=F32)
    o_ref[...] = x_ref[...] + gate_ref[0] * yo


def _merge(yg, og, g, x2d, mod3, wf, wa, wo, *, seq, tm=256):
    m, d = x2d.shape
    tiles_per_seq = seq // tm
    const = lambda shape: pl.BlockSpec(shape, lambda i: (0, 0), pipeline_mode=pl.Buffered(1))
    return pl.pallas_call(
        _merge_kernel,
        grid=(m // tm,),
        in_specs=[
            pl.BlockSpec((tm, yg.shape[1]), lambda i: (i, 0)),
            pl.BlockSpec((tm, d), lambda i: (i, 0)),
            pl.BlockSpec((tm, d), lambda i: (i, 0)),
            pl.BlockSpec((tm, d), lambda i: (i, 1)),
            pl.BlockSpec((tm, d), lambda i: (i, 0)),
            pl.BlockSpec((1, 1, d), lambda i: (i // tiles_per_seq, 0, 2)),
            const(wf.shape), const(wa.shape), const(wo.shape),
        ],
        out_specs=pl.BlockSpec((tm, d), lambda i: (i, 0)),
        out_shape=jax.ShapeDtypeStruct((m, d), F32),
        compiler_params=pltpu.CompilerParams(
            dimension_semantics=("parallel",), vmem_limit_bytes=VMEM_LIMIT),
        name="merge",
    )(yg, og, g, g, x2d, mod3, wf, wa, wo)


def _rope_tables(seq, head_dim):
    n_freq = head_dim // 4
    t = np.arange(seq)
    pos = np.stack([t // GRID_W, t % GRID_W], axis=-1).astype(np.float32)
    inv_freq = (np.float32(ROPE_BASE) ** (-np.arange(n_freq, dtype=np.float32) / np.float32(n_freq)))
    ang = (pos[:, :, None] * inv_freq.astype(np.float32)).astype(np.float64)
    ang = np.broadcast_to(ang[:, None, :, :], (seq, 2, 2, n_freq))
    sign = np.array([-1.0, 1.0])[None, :, None, None]
    return (np.cos(ang).reshape(seq, head_dim).astype(np.float32),
            (np.sin(ang) * sign).reshape(seq, head_dim).astype(np.float32))


def _rope_lane_order(a, n_freq):
    lead = a.shape[:-1]
    return a.reshape(lead + (-1, 2, 2, n_freq)).swapaxes(-3, -2).reshape(a.shape)


def kernel(x, c, ctx, c_ctx, w_mod, b_mod, w_in, q_gain, k_gain, rpb, w_f_out, w_a_out, w_out):
    batch, seq, d = x.shape
    ctx_len = ctx.shape[1]
    depth, heads, n_dr, n_dc = rpb.shape
    assert depth == 1 and w_mod.shape[0] == 1
    head_dim = q_gain.shape[1]
    assert head_dim == LANES and seq % GRID_W == 0
    win_r, win_c = (n_dr + 1) // 2, (n_dc + 1) // 2
    attn_w = heads * head_dim
    fw = w_f_out.shape[1]
    gd = fw // F_GROUPS
    off_zf, off_q = fw, 2 * fw
    off_k, off_v, off_za = off_q + attn_w, off_q + 2 * attn_w, off_q + 3 * attn_w
    off_gf = off_za + attn_w
    off_ga = off_gf + d
    assert w_in.shape[2] == off_ga + d

    c_all = jnp.concatenate([c, c_ctx[None, :], jnp.zeros((16 - batch - 1, d), F32)], axis=0)
    mod = _mod(c_all, w_mod[0], b_mod)
    mod3 = mod.reshape(16, 1, 3 * d)

    x2d = x.reshape(batch * seq, d)
    u_f, h_x = _proj_x(x2d, mod3, w_in[0], n=fw, rows_per_mod=seq)
    h_c =_hnorm(ctx.reshape(batch * ctx_len, d), mod3, rows_per_mod=batch * ctx_len, mod_row0=batch)

    n_freq = head_dim // 4
    cos_np, sin_np = _rope_tables(seq, head_dim)
    cos_t, sin_t = jnp.asarray(cos_np), jnp.asarray(sin_np)
    qg = _rope_lane_order(q_gain, n_freq)
    kg = _rope_lane_order(k_gain, n_freq)
    q_scale = float(head_dim) ** -0.5 * LOG2E
    proj =functools.partial(_proj, w=w_in[0], cos_t=cos_t, sin_t=sin_t, q_scale=q_scale)

    z_f = proj(h_x, gain=qg, kind="silu", col0=off_zf, n=fw)
    q = proj(h_x, gain=qg, kind="q", col0=off_q, n=attn_w)
    k = proj(h_x, gain=kg, kind="k", col0=off_k, n=attn_w)
    v = proj(h_x, gain=qg, kind="raw", col0=off_v, n=attn_w)
    z_a = proj(h_x, gain=qg, kind="silu", col0=off_za, n=attn_w)
    g = proj(h_x, gain=qg, kind="sig", col0=off_gf, n=2 * d)
    k_c = proj(h_c, gain=kg, kind="kc", col0=off_k, n=attn_w)
    v_c = proj(h_c, gain=qg, kind="raw", col0=off_v, n=attn_w)

    rpb_pad = jnp.pad(rpb[0], ((0, 0), (0, 16 - n_dr), (0, LANES - n_dc)))
    og = _attention(q, k, v, z_a, k_c, v_c, rpb_pad, batch=batch, seq=seq, ctx_len=ctx_len, heads=heads,
                    win_r=win_r, win_c=win_c)

    cc, sc = _dft_mats(gd)
    cn, sn = _dft_mats(seq)
    const = lambda a: jnp.asarray(a.astype(np.float32)).astype(BF16)
    half, tk = seq // 2, 256
    nyq = np.zeros((16, seq))
    nyq[0] = cn[half]
    flip = np.zeros((tk, 2 * tk))
    flip[np.arange(tk), tk - np.arange(tk)] = 1.0
    yg = _fourier(u_f, z_f, const(np.concatenate([cc, sc], axis=1)), const(cn[:half]), const(sn[:half]),
                  const(nyq), const(flip), batch=batch, seq=seq, gd=gd, tk=tk)

    out = _merge(yg, og, g, x2d, mod3, w_f_out[0].astype(BF16), w_a_out[0].astype(BF16),
                 w_out[0].astype(BF16), seq=seq)
    return out.reshape(batch, seq, d)
```

```python
import functools

import numpy as np
import jax
import jax.numpy as jnp
from jax import lax
from jax.experimental import pallas as pl
from jax.experimental.pallas import tpu as pltpu

GRID_W = 64
F_GROUPS = 4
ROPE_BASE = 10000.0
EPS = 1e-6
NEG = -1e30
LOG2E = 1.4426950408889634
LANES = 128
VMEM_LIMIT = 56 * 1024 * 1024

BF16 = jnp.bfloat16
F32 = jnp.float32


def _nt_dot(a, b):
    return lax.dot_general(a, b, (((1,), (1,)), ((), ())), preferred_element_type=F32)


def _mod_kernel(c_ref, w_ref, b_ref, o_ref):
    a = jax.nn.silu(c_ref[...]).astype(BF16)
    o_ref[...] = jnp.dot(a, w_ref[...].astype(BF16), preferred_element_type=F32) + b_ref[...]


def _mod(c_all, w_mod, b_mod, tn=512):
    m, d = c_all.shape
    n = w_mod.shape[1]
    return pl.pallas_call(
        _mod_kernel,
        grid=(n // tn,),
        in_specs=[pl.BlockSpec((m, d), lambda j: (0, 0)),
                  pl.BlockSpec((d, tn), lambda j: (0, j)),
                  pl.BlockSpec((1, tn), lambda j: (0, j))],
        out_specs=pl.BlockSpec((m, tn), lambda j: (0, j)),
        out_shape=jax.ShapeDtypeStruct((m, n), F32),
        name="mod",
    )(c_all, w_mod, b_mod)


def _rope_lane_order_cols(w):
    n = w.shape[1]
    quarter = (lax.broadcasted_iota(jnp.int32, w.shape, 1) % LANES) // (LANES // 4)
    up = pltpu.roll(w, n - LANES // 4, 1)
    down = pltpu.roll(w, LANES // 4, 1)
    return jnp.where(quarter == 1, up, jnp.where(quarter == 2, down, w))


def _proj_kernel(kind, mm, rc, q_scale, h_ref, w_ref, gain_ref, cos_ref, sin_ref, o_ref, wb_ref):
    tm = h_ref.shape[0]
    tn = w_ref.shape[1]

    @pl.when(pl.program_id(1) == 0)
    def _():
        _cast_weight_tile(w_ref, wb_ref, rc, kind in ("q", "k"))

    for r in range(0, tm, mm):
        a = jnp.dot(h_ref[r:r + mm, :], wb_ref[...], preferred_element_type=F32)
        if kind == "raw":
            o_ref[r:r + mm, :] = a.astype(BF16)
        elif kind == "silu":
            o_ref[r:r + mm, :] = jax.nn.silu(a).astype(BF16)
        elif kind == "sig":
            o_ref[r:r + mm, :] = jax.nn.sigmoid(a).astype(BF16)
        else:
            gain = gain_ref[...]
            for hh in range(tn // LANES):
                xh = a[:, hh * LANES:(hh + 1) * LANES]
                ms = jnp.mean(xh * xh, axis=-1, keepdims=True)
                xn = xh * lax.rsqrt(ms + EPS) * gain
                xn = xn * cos_ref[r:r + mm, :] + pltpu.roll(xn, LANES // 2, 1) * sin_ref[r:r + mm, :]
                if kind == "q":
                    xn = xn * q_scale
                o_ref[r:r + mm, hh * LANES:(hh + 1) * LANES] = xn.astype(BF16)


def _cast_weight_tile(w_ref, wb_ref, rc, reorder):
    def body(t, carry):
        r = pl.multiple_of(t * rc, rc)
        wt = w_ref[pl.ds(r, rc), :]
        if reorder:
            wt = _rope_lane_order_cols(wt)
        wb_ref[pl.ds(r, rc), :] = wt.astype(BF16)
        return carry
    lax.fori_loop(0, w_ref.shape[0] // rc, body, 0)


def _proj_x_kernel(mm, rc, n_x, x_ref, c_ref, shift_ref, scale_ref, w_ref, o_ref, h_ref, wb_ref):
    @pl.when(pl.program_id(1) == 0)
    def _():
        _cast_weight_tile(w_ref, wb_ref, rc, False)

    is_ctx = pl.program_id(1) >= n_x
    for r in range(0, x_ref.shape[0], mm):
        xs = jnp.where(is_ctx, c_ref[r:r + mm, :], x_ref[r:r + mm, :])
        ms = jnp.mean(xs * xs, axis=-1, keepdims=True)
        h = (xs * lax.rsqrt(ms + EPS) * (1.0 + scale_ref[0]) + shift_ref[0]).astype(BF16)
        h_ref[r:r + mm, :] = h
        o_ref[r:r + mm, :] = jnp.dot(h, wb_ref[...], preferred_element_type=F32).astype(BF16)


def _proj_x(x2d, c2d, mod3, w, *, batch, tm=512, tn=1024, mm=256, rc=64):
    m, d = x2d.shape
    n_x, n_c = m // tm, c2d.shape[0] // tm
    tiles_per_mod = m // batch // tm
    mod_row = lambda i: jnp.minimum(i // tiles_per_mod, batch)
    return pl.pallas_call(
        functools.partial(_proj_x_kernel, mm, rc, n_x),
        grid=(1, n_x + n_c),
        in_specs=[
            pl.BlockSpec((tm, d), lambda j, i: (jnp.minimum(i, n_x - 1), 0)),
            pl.BlockSpec((tm, d), lambda j, i: (jnp.maximum(i - n_x, 0), 0)),
            pl.BlockSpec((1, 1, d), lambda j, i: (mod_row(i), 0, 0)),
            pl.BlockSpec((1, 1, d), lambda j, i: (mod_row(i), 0, 1)),
            pl.BlockSpec((d, tn), lambda j, i: (0, j)),
        ],
        out_specs=[pl.BlockSpec((tm, tn), lambda j, i: (i, j)),
                   pl.BlockSpec((tm, d), lambda j, i: (i, 0))],
        out_shape=[jax.ShapeDtypeStruct(((n_x + n_c) * tm, tn), BF16),
                   jax.ShapeDtypeStruct(((n_x + n_c) * tm, d), BF16)],
        scratch_shapes=[pltpu.VMEM((d, tn), BF16)],
        compiler_params=pltpu.CompilerParams(
            dimension_semantics=("arbitrary", "arbitrary"), vmem_limit_bytes=VMEM_LIMIT),
        name="proj_x",
    )(x2d, c2d, mod3, mod3, w)


def _lookup(j, values):
    out = values[-1]
    for idx in range(len(values) - 2, -1, -1):
        out = jnp.where(j == idx, values[idx], out)
    return out


def _proj(h, w, gain, cos_t, sin_t, *, kind, col_tiles, m, x_rows, seq, q_scale, tm=2048, tn=1024, rc=64):
    d = h.shape[1]
    mm = 512 if kind == "raw" else 256
    x_tiles, seq_tiles = x_rows // tm, seq // tm
    rope_tile = lambda i: jnp.where(i < x_tiles, i % seq_tiles, seq_tiles + i - x_tiles)
    kern = functools.partial(_proj_kernel, kind, mm, rc, q_scale)
    n = len(col_tiles) * tn
    return pl.pallas_call(
        kern,
        grid=(len(col_tiles), m // tm),
        in_specs=[
            pl.BlockSpec((tm, d), lambda j, i: (i, 0)),
            pl.BlockSpec((d, tn), lambda j, i: (0, _lookup(j, col_tiles))),
            pl.BlockSpec((1, LANES), lambda j, i: (0, 0)),
            pl.BlockSpec((tm, LANES), lambda j, i: (rope_tile(i), 0)),
            pl.BlockSpec((tm, LANES), lambda j, i: (rope_tile(i), 0)),
        ],
        out_specs=pl.BlockSpec((tm, tn), lambda j, i: (i, j)),
        out_shape=jax.ShapeDtypeStruct((m, n), BF16),
        scratch_shapes=[pltpu.VMEM((d, tn), BF16)],
        compiler_params=pltpu.CompilerParams(
            dimension_semantics=("arbitrary", "arbitrary"), vmem_limit_bytes=VMEM_LIMIT),
        name="proj_" + kind,
    )(h, w, gain, cos_t, sin_t)


Q_ROWS = 2
BAND = 10
GROUP = 1
AHEAD = 2
N_SLOTS = (AHEAD + 1) * GROUP


def _band_start(r0, rows):
    return min(max(r0 - 4, 0), rows - BAND)


def _band_offsets(rows):
    return sorted({r0 - _band_start(r0, rows) for r0 in range(0, rows, Q_ROWS)})


def _valid_slots(r, s0, rows, win_r):
    rs = min(max(r - win_r // 2, 0), rows - win_r)
    return tuple(rs <= s0 + i < rs + win_r for i in range(BAND))


def _build_bias_tables(rpb_ref, bias_ref, rows, win_r, win_c):
    c_io = lax.broadcasted_iota(jnp.int32, (GRID_W, LANES), 0)
    l_io = lax.broadcasted_iota(jnp.int32, (GRID_W, LANES), 1)
    cs = jnp.clip(c_io - win_c // 2, 0, GRID_W - win_c)
    inwin = (l_io >= cs) & (l_io < cs + win_c) & (l_io < GRID_W)
    low = l_io < GRID_W
    neg = jnp.full((GRID_W, LANES), NEG, F32)
    toep = []
    for dr in range(2 * win_r - 1):
        row = jnp.broadcast_to(rpb_ref[0, dr:dr + 1, :], (GRID_W, LANES))
        t = pltpu.roll(row, LANES - (win_c - 1), 1, stride=1, stride_axis=0)
        toep.append(jnp.where(inwin, t * LOG2E, NEG))
    starts = range(0, rows, Q_ROWS)
    for tb, off in enumerate(_band_offsets(rows)):
        same = [r0 for r0 in starts if r0 - _band_start(r0, rows) == off]
        r0, s0 = same[0], _band_start(same[0], rows)
        for rho in range(Q_ROWS):
            r = r0 + rho
            valid = _valid_slots(r, s0, rows, win_r)
            assert all(_valid_slots(o + rho, _band_start(o, rows), rows, win_r) == valid for o in same)
            blocks = [toep[s0 + i - r + win_r - 1] if valid[i] else neg for i in range(BAND)]
            for p in range(BAND // 2):
                tile = jnp.where(low, blocks[2 * p], pltpu.roll(blocks[2 * p + 1], GRID_W, 1))
                bias_ref[tb, rho * GRID_W:(rho + 1) * GRID_W, p * LANES:(p + 1) * LANES] = tile


def _attn_kernel(rows, win_r, win_c,
                 q_ref, k_ref, v_ref, z_ref, kc_ref, vc_ref, rpb_ref, o_ref, bias_ref, s_ref):
    @pl.when(pl.program_id(1) == 0)
    def _():
        _build_bias_tables(rpb_ref, bias_ref, rows, win_r, win_c)

    nq = Q_ROWS * GRID_W
    nk = BAND * GRID_W

    def scores(i, r0):
        s0 = _band_start(r0, rows)
        tb = _band_offsets(rows).index(r0 - s0)
        q0, k0 = r0 * GRID_W, s0 * GRID_W
        qb = q_ref[q0:q0 + nq, :]
        s_loc = _nt_dot(qb, k_ref[k0:k0 + nk, :]) + bias_ref[tb]
        s_ctx = _nt_dot(qb, kc_ref[...])
        slot = i % N_SLOTS
        s_ref[slot, :, :nk] = s_loc
        s_ref[slot, :, nk:] = s_ctx
        return jnp.maximum(jnp.max(s_loc, axis=-1, keepdims=True), jnp.max(s_ctx, axis=-1, keepdims=True))

    def finish(i, r0, m):
        q0, k0 = r0 * GRID_W, _band_start(r0, rows) * GRID_W
        p = jnp.exp2(s_ref[i % N_SLOTS] - m)
        den = jnp.sum(p, axis=-1, keepdims=True)
        pb = p.astype(BF16)
        o = (jnp.dot(pb[:, :nk], v_ref[k0:k0 + nk, :], preferred_element_type=F32)
             + jnp.dot(pb[:, nk:], vc_ref[...], preferred_element_type=F32))
        og = (o / den) * z_ref[q0:q0 + nq, :].astype(F32)
        o_ref[q0:q0 + nq, :] = og.astype(BF16)

    starts = list(range(0, rows, Q_ROWS))
    groups = [starts[g:g + GROUP] for g in range(0, len(starts), GROUP)]

    def issue(g):
        return [scores(g * GROUP + u, r0) for u, r0 in enumerate(groups[g])]

    pending = [issue(g) for g in range(AHEAD)]
    for g, grp in enumerate(groups):
        if g + AHEAD < len(groups):
            pending.append(issue(g + AHEAD))
        for u, (r0, m) in enumerate(zip(grp, pending.pop(0))):
            finish(g * GROUP + u, r0, m)


def _attention(q, k, v, z, rpb_pad, *, batch, seq, ctx_len, heads, z_col0, win_r, win_c):
    rows = seq // GRID_W
    kern = functools.partial(_attn_kernel, rows, win_r, win_c)
    tok = pl.BlockSpec((seq, LANES), lambda h, b: (b, h))
    ctx0 = batch * seq // ctx_len
    ctx = pl.BlockSpec((ctx_len, LANES), lambda h, b: (ctx0 + b, h))
    return pl.pallas_call(
        kern,
        grid=(heads, batch),
        in_specs=[tok, tok, tok, pl.BlockSpec((seq, LANES), lambda h, b: (b, z_col0 // LANES + h)), ctx, ctx,
                  pl.BlockSpec((1,) + rpb_pad.shape[1:], lambda h, b: (h, 0, 0))],
        out_specs=tok,
        out_shape=jax.ShapeDtypeStruct((batch * seq, heads * LANES), BF16),
        scratch_shapes=[pltpu.VMEM((len(_band_offsets(rows)), Q_ROWS * GRID_W, BAND * GRID_W), F32),
                        pltpu.VMEM((N_SLOTS, Q_ROWS * GRID_W, BAND * GRID_W + ctx_len), F32)],
        compiler_params=pltpu.CompilerParams(
            dimension_semantics=("arbitrary", "arbitrary"), vmem_limit_bytes=VMEM_LIMIT),
        name="attn",
    )(q, k, v, z, k, v, rpb_pad)


def _dft_mats(n):
    jk = (np.arange(n)[:, None] * np.arange(n)[None, :]) % n
    ang = 2.0 * np.pi * jk.astype(np.float64) / n
    return np.cos(ang) / np.sqrt(n), np.sin(ang) / np.sqrt(n)


def _fourier_kernel(gd, u_ref, zf_ref, csc_ref, ch_ref, sh_ref, nyq_ref, flip_ref, o_ref, a_ref, b_ref, e_ref):
    seq = u_ref.shape[0]
    half = seq // 2
    tk = ch_ref.shape[0]
    n_first = half // tk
    s = pl.program_id(1)

    @pl.when(s == 0)
    def _():
        rb = 512
        for g in range(u_ref.shape[1] // gd):
            for r in range(0, seq, rb):
                t = jnp.dot(u_ref[r:r + rb, g * gd:(g + 1) * gd], csc_ref[...], preferred_element_type=F32)
                a_ref[r:r + rb, g * gd:(g + 1) * gd] = t[:, :gd].astype(BF16)
                b_ref[r:r + rb, g * gd:(g + 1) * gd] = t[:, gd:].astype(BF16)
        e_ref[half:, :] = jnp.zeros((tk, e_ref.shape[1]), BF16)
        e_ref[half:half + nyq_ref.shape[0], :] = jnp.dot(
            nyq_ref[...], a_ref[...], preferred_element_type=F32).astype(BF16)

    rows = pl.ds(pl.multiple_of(s * tk, tk), tk)
    p = jnp.dot(ch_ref[...], a_ref[...], preferred_element_type=F32)
    q = jnp.dot(sh_ref[...], b_ref[...], preferred_element_type=F32)
    o_ref[rows, :] = ((p - q) * zf_ref[rows, :].astype(F32)).astype(BF16)
    e_ref[rows, :] = (p + q).astype(BF16)

    @pl.when(s == n_first - 1)
    def _():
        for t in range(n_first):
            base = half - (t + 1) * tk
            y = jnp.dot(flip_ref[...], e_ref[base:base + 2 * tk, :], preferred_element_type=F32)
            out = slice(half + t * tk, half + (t + 1) * tk)
            o_ref[out, :] = (y * zf_ref[out, :].astype(F32)).astype(BF16)


def _fourier(u, zf, csc, ch, sh, nyq, flip, *, batch, seq, gd, tk=256):
    fw = u.shape[1]
    half = seq // 2
    n_first = half // tk
    kern = functools.partial(_fourier_kernel, gd)
    per_batch = pl.BlockSpec((seq, fw), lambda b, k: (b, 0))
    half_rows = pl.BlockSpec((tk, seq), lambda b, k: (k, 0))
    whole = lambda a: pl.BlockSpec(a.shape, lambda b, k: (0, 0))
    return pl.pallas_call(
        kern,
        grid=(batch, n_first),
        in_specs=[per_batch, per_batch, whole(csc), half_rows, half_rows, whole(nyq), whole(flip)],
        out_specs=per_batch,
        out_shape=jax.ShapeDtypeStruct((batch * seq, fw), BF16),
        scratch_shapes=[pltpu.VMEM((seq, fw), BF16), pltpu.VMEM((seq, fw), BF16),
                        pltpu.VMEM((half + tk, fw), BF16)],
        compiler_params=pltpu.CompilerParams(
            dimension_semantics=("arbitrary", "arbitrary"), vmem_limit_bytes=VMEM_LIMIT),
        name="fourier",
    )(u, zf, csc, ch, sh, nyq, flip)


def _merge_kernel(yg_ref, og_ref, sgf_ref, sga_ref, x_ref, gate_ref, wf_ref, wa_ref, wo_ref, o_ref):
    yf = jnp.dot(yg_ref[...], wf_ref[...], preferred_element_type=F32)
    ya = jnp.dot(og_ref[...], wa_ref[...], preferred_element_type=F32)
    y = sgf_ref[...].astype(F32) * yf + sga_ref[...].astype(F32) * ya
    yo = jnp.dot(y.astype(BF16), wo_ref[...], preferred_element_type=F32)
    o_ref[...] = x_ref[...] + gate_ref[0] * yo


def _merge(yg, og, g, x2d, mod3, wf, wa, wo, *, seq, tm=256):
    m, d = x2d.shape
    tiles_per_seq = seq // tm
    const = lambda shape: pl.BlockSpec(shape, lambda i: (0, 0), pipeline_mode=pl.Buffered(1))
    return pl.pallas_call(
        _merge_kernel,
        grid=(m // tm,),
        in_specs=[
            pl.BlockSpec((tm, yg.shape[1]), lambda i: (i, 0)),
            pl.BlockSpec((tm, d), lambda i: (i, 0)),
            pl.BlockSpec((tm, d), lambda i: (i, 0)),
            pl.BlockSpec((tm, d), lambda i: (i, 1)),
            pl.BlockSpec((tm, d), lambda i: (i, 0)),
            pl.BlockSpec((1, 1, d), lambda i: (i // tiles_per_seq, 0, 2)),
            const(wf.shape), const(wa.shape), const(wo.shape),
        ],
        out_specs=pl.BlockSpec((tm, d), lambda i: (i, 0)),
        out_shape=jax.ShapeDtypeStruct((m, d), F32),
        compiler_params=pltpu.CompilerParams(
            dimension_semantics=("parallel",), vmem_limit_bytes=VMEM_LIMIT),
        name="merge",
    )(yg, og, g, g, x2d, mod3, wf, wa, wo)


def _rope_tables(seq, head_dim):
    n_freq = head_dim // 4
    t = np.arange(seq)
    pos = np.stack([t // GRID_W, t % GRID_W], axis=-1).astype(np.float32)
    inv_freq = (np.float32(ROPE_BASE) ** (-np.arange(n_freq, dtype=np.float32) / np.float32(n_freq)))
    ang = (pos[:, :, None] * inv_freq.astype(np.float32)).astype(np.float64)
    ang = np.broadcast_to(ang[:, None, :, :], (seq, 2, 2, n_freq))
    sign = np.array([-1.0, 1.0])[None, :, None, None]
    return (np.cos(ang).reshape(seq, head_dim).astype(np.float32),
            (np.sin(ang) * sign).reshape(seq, head_dim).astype(np.float32))


def _rope_lane_order(a, n_freq):
    lead = a.shape[:-1]
    return a.reshape(lead + (-1, 2, 2, n_freq)).swapaxes(-3, -2).reshape(a.shape)


def kernel(x, c, ctx, c_ctx, w_mod, b_mod, w_in, q_gain, k_gain, rpb, w_f_out, w_a_out, w_out):
    batch, seq, d = x.shape
    ctx_len = ctx.shape[1]
    depth, heads, n_dr, n_dc = rpb.shape
    assert depth == 1 and w_mod.shape[0] == 1
    head_dim = q_gain.shape[1]
    assert head_dim == LANES and seq % GRID_W == 0
    win_r, win_c = (n_dr + 1) // 2, (n_dc + 1) // 2
    attn_w = heads * head_dim
    fw = w_f_out.shape[1]
    gd = fw // F_GROUPS
    off_zf, off_q = fw, 2 * fw
    off_k, off_v, off_za = off_q + attn_w, off_q + 2 * attn_w, off_q + 3 * attn_w
    off_gf = off_za + attn_w
    off_ga = off_gf + d
    assert w_in.shape[2] == off_ga + d

    c_all = jnp.concatenate([c, c_ctx[None, :], jnp.zeros((16 - batch - 1, d), F32)], axis=0)
    mod = _mod(c_all, w_mod[0], b_mod)
    mod3 = mod.reshape(16, 1, 3 * d)

    x2d = x.reshape(batch * seq, d)
    c2d = ctx.reshape(batch * ctx_len, d)
    x_rows, all_rows = batch * seq, batch * (seq + ctx_len)
    u_f, h = _proj_x(x2d, c2d, mod3, w_in[0], batch=batch)

    n_freq = head_dim // 4
    tn = 1024
    assert batch * ctx_len == 2048 and fw == tn
    cos_np, sin_np = _rope_tables(seq, head_dim)
    ident = np.ones((batch * ctx_len, head_dim), np.float32)
    cos_t = jnp.asarray(np.concatenate([cos_np, ident]))
    sin_t = jnp.asarray(np.concatenate([sin_np, 0.0 * ident]))
    qg = _rope_lane_order(q_gain, n_freq)
    kg = _rope_lane_order(k_gain, n_freq)
    q_scale = float(head_dim) ** -0.5 * LOG2E
    proj = functools.partial(_proj, h, w_in[0], cos_t=cos_t, sin_t=sin_t, x_rows=x_rows, seq=seq,
                             q_scale=q_scale, tn=tn)
    tiles = lambda a, b: tuple(range(a // tn, b // tn))

    z = proj(gain=qg, kind="silu", col_tiles=tiles(off_zf, off_q) + tiles(off_za, off_gf), m=x_rows)
    q = proj(gain=qg, kind="q", col_tiles=tiles(off_q, off_k), m=x_rows)
    k = proj(gain=kg, kind="k", col_tiles=tiles(off_k, off_v), m=all_rows)
    v = proj(gain=qg, kind="raw", col_tiles=tiles(off_v, off_za), m=all_rows)
    g = proj(gain=qg, kind="sig", col_tiles=tiles(off_gf, off_ga + d), m=x_rows)

    rpb_pad = jnp.pad(rpb[0], ((0, 0), (0, 16 - n_dr), (0, LANES - n_dc)))
    og = _attention(q, k, v, z, rpb_pad, batch=batch, seq=seq, ctx_len=ctx_len, heads=heads, z_col0=fw,
                    win_r=win_r, win_c=win_c)
    z_f = z

    cc, sc = _dft_mats(gd)
    cn, sn = _dft_mats(seq)
    const = lambda a: jnp.asarray(a.astype(np.float32)).astype(BF16)
    half, tk = seq // 2, 256
    nyq = np.zeros((16, seq))
    nyq[0] = cn[half]
    flip = np.zeros((tk, 2 * tk))
    flip[np.arange(tk), tk - np.arange(tk)] = 1.0
    yg = _fourier(u_f, z_f, const(np.concatenate([cc, sc], axis=1)), const(cn[:half]), const(sn[:half]),
                  const(nyq), const(flip), batch=batch, seq=seq, gd=gd, tk=tk)

    out = _merge(yg, og, g, x2d, mod3, w_f_out[0].astype(BF16), w_a_out[0].astype(BF16),
                 w_out[0].astype(BF16), seq=seq)
    return out.reshape(batch, seq, d)
```

```python
import functools

import numpy as np
import jax
import jax.numpy as jnp
from jax import lax
from jax.experimental import pallas as pl
from jax.experimental.pallas import tpu as pltpu

GRID_W = 64
F_GROUPS = 4
ROPE_BASE = 10000.0
EPS = 1e-6
NEG = -1e30
LOG2E = 1.4426950408889634
LANES = 128
VMEM_LIMIT = 56 * 1024 * 1024

BF16 = jnp.bfloat16
F32 = jnp.float32


def _nt_dot(a, b):
    return lax.dot_general(a, b, (((1,), (1,)), ((), ())), preferred_element_type=F32)


def _mod_kernel(c_ref, w_ref, b_ref, o_ref):
    a = jax.nn.silu(c_ref[...]).astype(BF16)
    o_ref[...] = jnp.dot(a, w_ref[...].astype(BF16), preferred_element_type=F32) + b_ref[...]


def _mod(c_all, w_mod, b_mod, tn=512):
    m, d = c_all.shape
    n = w_mod.shape[1]
    return pl.pallas_call(
        _mod_kernel,
        grid=(n // tn,),
        in_specs=[pl.BlockSpec((m, d), lambda j: (0, 0)),
                  pl.BlockSpec((d, tn), lambda j: (0, j)),
                  pl.BlockSpec((1, tn), lambda j: (0, j))],
        out_specs=pl.BlockSpec((m, tn), lambda j: (0, j)),
        out_shape=jax.ShapeDtypeStruct((m, n), F32),
        name="mod",
    )(c_all, w_mod, b_mod)


def _rope_lane_order_cols(w):
    n = w.shape[1]
    quarter = (lax.broadcasted_iota(jnp.int32, w.shape, 1) % LANES) // (LANES // 4)
    up = pltpu.roll(w, n - LANES // 4, 1)
    down = pltpu.roll(w, LANES // 4, 1)
    return jnp.where(quarter == 1, up, jnp.where(quarter == 2, down, w))


def _proj_kernel(kind, mm, rc, q_scale, h_ref, w_ref, gain_ref, cos_ref, sin_ref, o_ref, wb_ref):
    tm = h_ref.shape[0]
    tn = w_ref.shape[1]

    @pl.when(pl.program_id(1) == 0)
    def _():
        _cast_weight_tile(w_ref, wb_ref, rc, kind in ("q", "k"))

    for r in range(0, tm, mm):
        a = jnp.dot(h_ref[r:r + mm, :], wb_ref[...], preferred_element_type=F32)
        if kind == "raw":
            o_ref[r:r + mm, :] = a.astype(BF16)
        elif kind == "silu":
            o_ref[r:r + mm, :] = jax.nn.silu(a).astype(BF16)
        elif kind == "sig":
            o_ref[r:r + mm, :] = jax.nn.sigmoid(a).astype(BF16)
        else:
            gain = gain_ref[...]
            for hh in range(tn // LANES):
                xh = a[:, hh * LANES:(hh + 1) * LANES]
                ms = jnp.mean(xh * xh, axis=-1, keepdims=True)
                xn = xh * lax.rsqrt(ms + EPS) * gain
                xn = xn * cos_ref[r:r + mm, :] + pltpu.roll(xn, LANES // 2, 1) * sin_ref[r:r + mm, :]
                if kind == "q":
                    xn = xn * q_scale
                o_ref[r:r + mm, hh * LANES:(hh + 1) * LANES] = xn.astype(BF16)


def _cast_weight_tile(w_ref, wb_ref, rc, reorder):
    def body(t, carry):
        r = pl.multiple_of(t * rc, rc)
        wt = w_ref[pl.ds(r, rc), :]
        if reorder:
            wt = _rope_lane_order_cols(wt)
        wb_ref[pl.ds(r, rc), :] = wt.astype(BF16)
        return carry
    lax.fori_loop(0, w_ref.shape[0] // rc, body, 0)


def _proj_x_kernel(mm, rc, n_x, x_ref, c_ref, shift_ref, scale_ref, w_ref, o_ref, h_ref, wb_ref):
    @pl.when(pl.program_id(1) == 0)
    def _():
        _cast_weight_tile(w_ref, wb_ref, rc, False)

    def body(t_ref):
        for r in range(0, t_ref.shape[0], mm):
            xs = t_ref[r:r + mm, :]
            ms = jnp.mean(xs * xs, axis=-1, keepdims=True)
            h = (xs * lax.rsqrt(ms + EPS) * (1.0 + scale_ref[0]) + shift_ref[0]).astype(BF16)
            h_ref[r:r + mm, :] = h
            o_ref[r:r + mm, :] = jnp.dot(h, wb_ref[...], preferred_element_type=F32).astype(BF16)

    is_ctx = pl.program_id(1) >= n_x
    pl.when(jnp.logical_not(is_ctx))(functools.partial(body, x_ref))
    pl.when(is_ctx)(functools.partial(body, c_ref))


def _proj_x(x2d, c2d, mod3, w, *, batch, tm=512, tn=1024, mm=256, rc=64):
    m, d = x2d.shape
    n_x, n_c = m // tm, c2d.shape[0] // tm
    tiles_per_mod = m // batch // tm
    mod_row = lambda i: jnp.minimum(i // tiles_per_mod, batch)
    return pl.pallas_call(
        functools.partial(_proj_x_kernel, mm, rc, n_x),
        grid=(1, n_x + n_c),
        in_specs=[
            pl.BlockSpec((tm, d), lambda j, i: (jnp.minimum(i, n_x - 1), 0)),
            pl.BlockSpec((tm, d), lambda j, i: (jnp.maximum(i - n_x, 0), 0)),
            pl.BlockSpec((1, 1, d), lambda j, i: (mod_row(i), 0, 0)),
            pl.BlockSpec((1, 1, d), lambda j, i: (mod_row(i), 0, 1)),
            pl.BlockSpec((d, tn), lambda j, i: (0, j)),
        ],
        out_specs=[pl.BlockSpec((tm, tn), lambda j, i: (i, j)),
                   pl.BlockSpec((tm, d), lambda j, i: (i, 0))],
        out_shape=[jax.ShapeDtypeStruct(((n_x + n_c) * tm, tn), BF16),
                   jax.ShapeDtypeStruct(((n_x + n_c) * tm, d), BF16)],
        scratch_shapes=[pltpu.VMEM((d, tn), BF16)],
        compiler_params=pltpu.CompilerParams(
            dimension_semantics=("arbitrary", "arbitrary"), vmem_limit_bytes=VMEM_LIMIT),
        name="proj_x",
    )(x2d, c2d, mod3, mod3, w)


def _lookup(j, values):
    out = values[-1]
    for idx in range(len(values) - 2, -1, -1):
        out = jnp.where(j == idx, values[idx], out)
    return out


def _proj(h, w, gain, cos_t, sin_t, *, kind, col_tiles, m, x_rows, seq, q_scale, tm=2048, tn=1024, rc=64):
    d = h.shape[1]
    mm = 512 if kind == "raw" else 256
    x_tiles, seq_tiles = x_rows // tm, seq // tm
    rope_tile = lambda i: jnp.where(i < x_tiles, i % seq_tiles, seq_tiles + i - x_tiles)
    kern = functools.partial(_proj_kernel, kind, mm, rc, q_scale)
    n = len(col_tiles) * tn
    return pl.pallas_call(
        kern,
        grid=(len(col_tiles), m // tm),
        in_specs=[
            pl.BlockSpec((tm, d), lambda j, i: (i, 0)),
            pl.BlockSpec((d, tn), lambda j, i: (0, _lookup(j, col_tiles))),
            pl.BlockSpec((1, LANES), lambda j, i: (0, 0)),
            pl.BlockSpec((tm, LANES), lambda j, i: (rope_tile(i), 0)),
            pl.BlockSpec((tm, LANES), lambda j, i: (rope_tile(i), 0)),
        ],
        out_specs=pl.BlockSpec((tm, tn), lambda j, i: (i, j)),
        out_shape=jax.ShapeDtypeStruct((m, n), BF16),
        scratch_shapes=[pltpu.VMEM((d, tn), BF16)],
        compiler_params=pltpu.CompilerParams(
            dimension_semantics=("arbitrary", "arbitrary"), vmem_limit_bytes=VMEM_LIMIT),
        name="proj_" + kind,
    )(h, w, gain, cos_t, sin_t)


Q_ROWS = 2
BAND = 10
GROUP = 1
AHEAD = 2
N_SLOTS = (AHEAD + 1) * GROUP


def _band_start(r0, rows):
    return min(max(r0 - 4, 0), rows - BAND)


def _band_offsets(rows):
    return sorted({r0 - _band_start(r0, rows) for r0 in range(0, rows, Q_ROWS)})


def _valid_slots(r, s0, rows, win_r):
    rs = min(max(r - win_r // 2, 0), rows - win_r)
    return tuple(rs <= s0 + i < rs + win_r for i in range(BAND))


def _build_bias_tables(rpb_ref, bias_ref, rows, win_r, win_c):
    c_io = lax.broadcasted_iota(jnp.int32, (GRID_W, LANES), 0)
    l_io = lax.broadcasted_iota(jnp.int32, (GRID_W, LANES), 1)
    cs = jnp.clip(c_io - win_c // 2, 0, GRID_W - win_c)
    inwin = (l_io >= cs) & (l_io < cs + win_c) & (l_io < GRID_W)
    low = l_io < GRID_W
    neg = jnp.full((GRID_W, LANES), NEG, F32)
    toep = []
    for dr in range(2 * win_r - 1):
        row = jnp.broadcast_to(rpb_ref[0, dr:dr + 1, :], (GRID_W, LANES))
        t = pltpu.roll(row, LANES - (win_c - 1), 1, stride=1, stride_axis=0)
        toep.append(jnp.where(inwin, t * LOG2E, NEG))
    starts = range(0, rows, Q_ROWS)
    for tb, off in enumerate(_band_offsets(rows)):
        same = [r0 for r0 in starts if r0 - _band_start(r0, rows) == off]
        r0, s0 = same[0], _band_start(same[0], rows)
        for rho in range(Q_ROWS):
            r = r0 + rho
            valid = _valid_slots(r, s0, rows, win_r)
            assert all(_valid_slots(o + rho, _band_start(o, rows), rows, win_r) == valid for o in same)
            blocks = [toep[s0 + i - r + win_r - 1] if valid[i] else neg for i in range(BAND)]
            for p in range(BAND // 2):
                tile = jnp.where(low, blocks[2 * p], pltpu.roll(blocks[2 * p + 1], GRID_W, 1))
                bias_ref[tb, rho * GRID_W:(rho + 1) * GRID_W, p * LANES:(p + 1) * LANES] = tile


def _attn_kernel(rows, win_r, win_c,
                 q_ref, k_ref, v_ref, z_ref, kc_ref, vc_ref, rpb_ref, o_ref, bias_ref, s_ref):
    @pl.when(pl.program_id(1) == 0)
    def _():
        _build_bias_tables(rpb_ref, bias_ref, rows, win_r, win_c)

    nq = Q_ROWS * GRID_W
    nk = BAND * GRID_W

    def scores(i, r0):
        s0 = _band_start(r0, rows)
        tb = _band_offsets(rows).index(r0 - s0)
        q0, k0 = r0 * GRID_W, s0 * GRID_W
        qb = q_ref[q0:q0 + nq, :]
        s_loc = _nt_dot(qb, k_ref[k0:k0 + nk, :]) + bias_ref[tb]
        s_ctx = _nt_dot(qb, kc_ref[...])
        slot = i % N_SLOTS
        s_ref[slot, :, :nk] = s_loc
        s_ref[slot, :, nk:] = s_ctx
        return jnp.maximum(jnp.max(s_loc, axis=-1, keepdims=True), jnp.max(s_ctx, axis=-1, keepdims=True))

    def finish(i, r0, m):
        q0, k0 = r0 * GRID_W, _band_start(r0, rows) * GRID_W
        p = jnp.exp2(s_ref[i % N_SLOTS] - m)
        den = jnp.sum(p, axis=-1, keepdims=True)
        pb = p.astype(BF16)
        o = (jnp.dot(pb[:, :nk], v_ref[k0:k0 + nk, :], preferred_element_type=F32)
             + jnp.dot(pb[:, nk:], vc_ref[...], preferred_element_type=F32))
        og = (o / den) * z_ref[q0:q0 + nq, :].astype(F32)
        o_ref[q0:q0 + nq, :] = og.astype(BF16)

    starts = list(range(0, rows, Q_ROWS))
    groups = [starts[g:g + GROUP] for g in range(0, len(starts), GROUP)]

    def issue(g):
        return [scores(g * GROUP + u, r0) for u, r0 in enumerate(groups[g])]

    pending = [issue(g) for g in range(AHEAD)]
    for g, grp in enumerate(groups):
        if g + AHEAD < len(groups):
            pending.append(issue(g + AHEAD))
        for u, (r0, m) in enumerate(zip(grp, pending.pop(0))):
            finish(g * GROUP + u, r0, m)


def _attention(q, k, v, z, rpb_pad, *, batch, seq, ctx_len, heads, z_col0, win_r, win_c):
    rows = seq // GRID_W
    kern = functools.partial(_attn_kernel, rows, win_r, win_c)
    tok = pl.BlockSpec((seq, LANES), lambda h, b: (b, h))
    ctx0 = batch * seq // ctx_len
    ctx = pl.BlockSpec((ctx_len, LANES), lambda h, b: (ctx0 + b, h))
    return pl.pallas_call(
        kern,
        grid=(heads, batch),
        in_specs=[tok, tok, tok, pl.BlockSpec((seq, LANES), lambda h, b: (b, z_col0 // LANES + h)), ctx, ctx,
                  pl.BlockSpec((1,) + rpb_pad.shape[1:], lambda h, b: (h, 0, 0))],
        out_specs=tok,
        out_shape=jax.ShapeDtypeStruct((batch * seq, heads * LANES), BF16),
        scratch_shapes=[pltpu.VMEM((len(_band_offsets(rows)), Q_ROWS * GRID_W, BAND * GRID_W), F32),
                        pltpu.VMEM((N_SLOTS, Q_ROWS * GRID_W, BAND * GRID_W + ctx_len), F32)],
        compiler_params=pltpu.CompilerParams(
            dimension_semantics=("arbitrary", "arbitrary"), vmem_limit_bytes=VMEM_LIMIT),
        name="attn",
    )(q, k, v, z, k, v, rpb_pad)


def _dft_mats(n):
    jk = (np.arange(n)[:, None] * np.arange(n)[None, :]) % n
    ang = 2.0 * np.pi * jk.astype(np.float64) / n
    return np.cos(ang) / np.sqrt(n), np.sin(ang) / np.sqrt(n)


def _fourier_kernel(gd, u_ref, zf_ref, csc_ref, ch_ref, sh_ref, nyq_ref, flip_ref, o_ref, a_ref, b_ref, e_ref):
    seq = u_ref.shape[0]
    half = seq // 2
    tk = ch_ref.shape[0]
    fk = flip_ref.shape[0]
    n_first = half // tk
    s = pl.program_id(1)

    @pl.when(s == 0)
    def _():
        rb = 512
        for g in range(u_ref.shape[1] // gd):
            for r in range(0, seq, rb):
                t = jnp.dot(u_ref[r:r + rb, g * gd:(g + 1) * gd], csc_ref[...], preferred_element_type=F32)
                a_ref[r:r + rb, g * gd:(g + 1) * gd] = t[:, :gd].astype(BF16)
                b_ref[r:r + rb, g * gd:(g + 1) * gd] = t[:, gd:].astype(BF16)
        e_ref[half:, :] = jnp.zeros((fk, e_ref.shape[1]), BF16)
        e_ref[half:half + nyq_ref.shape[0], :] = jnp.dot(
            nyq_ref[...], a_ref[...], preferred_element_type=F32).astype(BF16)

    rows = pl.ds(pl.multiple_of(s * tk, tk), tk)
    p = jnp.dot(ch_ref[...], a_ref[...], preferred_element_type=F32)
    q = jnp.dot(sh_ref[...], b_ref[...], preferred_element_type=F32)
    o_ref[rows, :] = ((p - q) * zf_ref[rows, :].astype(F32)).astype(BF16)
    e_ref[rows, :] = (p + q).astype(BF16)

    @pl.when(s == n_first - 1)
    def _():
        for t in range(half // fk):
            base = half - (t + 1) * fk
            y = jnp.dot(flip_ref[...], e_ref[base:base + 2 * fk, :], preferred_element_type=F32)
            out = slice(half + t * fk, half + (t + 1) * fk)
            o_ref[out, :] = (y * zf_ref[out, :].astype(F32)).astype(BF16)


def _fourier(u, zf, csc, ch, sh, nyq, flip, *, batch, seq, gd, tk=512):
    fw = u.shape[1]
    half = seq // 2
    n_first = half // tk
    fk = flip.shape[0]
    kern = functools.partial(_fourier_kernel, gd)
    per_batch = pl.BlockSpec((seq, fw), lambda b, k: (b, 0))
    half_rows = pl.BlockSpec((tk, seq), lambda b, k: (k, 0))
    whole = lambda a: pl.BlockSpec(a.shape, lambda b, k: (0, 0))
    return pl.pallas_call(
        kern,
        grid=(batch, n_first),
        in_specs=[per_batch, per_batch, whole(csc), half_rows, half_rows, whole(nyq), whole(flip)],
        out_specs=per_batch,
        out_shape=jax.ShapeDtypeStruct((batch * seq, fw), BF16),
        scratch_shapes=[pltpu.VMEM((seq, fw), BF16), pltpu.VMEM((seq, fw), BF16),
                        pltpu.VMEM((half + fk, fw), BF16)],
        compiler_params=pltpu.CompilerParams(
            dimension_semantics=("arbitrary", "arbitrary"), vmem_limit_bytes=VMEM_LIMIT),
        name="fourier",
    )(u, zf, csc, ch, sh, nyq, flip)


def _merge_kernel(yg_ref, og_ref, sgf_ref, sga_ref, x_ref, gate_ref, wf_ref, wa_ref, wo_ref, o_ref):
    yf = jnp.dot(yg_ref[...], wf_ref[...], preferred_element_type=F32)
    ya = jnp.dot(og_ref[...], wa_ref[...], preferred_element_type=F32)
    y = sgf_ref[...].astype(F32) * yf + sga_ref[...].astype(F32) * ya
    yo = jnp.dot(y.astype(BF16), wo_ref[...], preferred_element_type=F32)
    o_ref[...] = x_ref[...] + gate_ref[0] * yo


def _merge(yg, og, g, x2d, mod3, wf, wa, wo, *, seq, tm=256):
    m, d = x2d.shape
    tiles_per_seq = seq // tm
    const = lambda shape: pl.BlockSpec(shape, lambda i: (0, 0), pipeline_mode=pl.Buffered(1))
    return pl.pallas_call(
        _merge_kernel,
        grid=(m // tm,),
        in_specs=[
            pl.BlockSpec((tm, yg.shape[1]), lambda i: (i, 0)),
            pl.BlockSpec((tm, d), lambda i: (i, 0)),
            pl.BlockSpec((tm, d), lambda i: (i, 0)),
            pl.BlockSpec((tm, d), lambda i: (i, 1)),
            pl.BlockSpec((tm, d), lambda i: (i, 0)),
            pl.BlockSpec((1, 1, d), lambda i: (i // tiles_per_seq, 0, 2)),
            const(wf.shape), const(wa.shape), const(wo.shape),
        ],
        out_specs=pl.BlockSpec((tm, d), lambda i: (i, 0)),
        out_shape=jax.ShapeDtypeStruct((m, d), F32),
        compiler_params=pltpu.CompilerParams(
            dimension_semantics=("parallel",), vmem_limit_bytes=VMEM_LIMIT),
        name="merge",
    )(yg, og, g, g, x2d, mod3, wf, wa, wo)


def _rope_tables(seq, head_dim):
    n_freq = head_dim // 4
    t = np.arange(seq)
    pos = np.stack([t // GRID_W, t % GRID_W], axis=-1).astype(np.float32)
    inv_freq = (np.float32(ROPE_BASE) ** (-np.arange(n_freq, dtype=np.float32) / np.float32(n_freq)))
    ang = (pos[:, :, None] * inv_freq.astype(np.float32)).astype(np.float64)
    ang = np.broadcast_to(ang[:, None, :, :], (seq, 2, 2, n_freq))
    sign = np.array([-1.0, 1.0])[None, :, None, None]
    return (np.cos(ang).reshape(seq, head_dim).astype(np.float32),
            (np.sin(ang) * sign).reshape(seq, head_dim).astype(np.float32))


def _rope_lane_order(a, n_freq):
    lead = a.shape[:-1]
    return a.reshape(lead + (-1, 2, 2, n_freq)).swapaxes(-3, -2).reshape(a.shape)


def kernel(x, c, ctx, c_ctx, w_mod, b_mod, w_in, q_gain, k_gain, rpb, w_f_out, w_a_out, w_out):
    batch, seq, d = x.shape
    ctx_len = ctx.shape[1]
    depth, heads, n_dr, n_dc = rpb.shape
    assert depth == 1 and w_mod.shape[0] == 1
    head_dim = q_gain.shape[1]
    assert head_dim == LANES and seq % GRID_W == 0
    win_r, win_c = (n_dr + 1) // 2, (n_dc + 1) // 2
    attn_w = heads * head_dim
    fw = w_f_out.shape[1]
    gd = fw // F_GROUPS
    off_zf, off_q = fw, 2 * fw
    off_k, off_v, off_za = off_q + attn_w, off_q + 2 * attn_w, off_q + 3 * attn_w
    off_gf = off_za + attn_w
    off_ga = off_gf + d
    assert w_in.shape[2] == off_ga + d

    c_all = jnp.concatenate([c, c_ctx[None, :], jnp.zeros((16 - batch - 1, d), F32)], axis=0)
    mod = _mod(c_all, w_mod[0], b_mod)
    mod3 = mod.reshape(16, 1, 3 * d)

    x2d = x.reshape(batch * seq, d)
    c2d = ctx.reshape(batch * ctx_len, d)
    x_rows, all_rows = batch * seq, batch * (seq + ctx_len)
    u_f, h = _proj_x(x2d, c2d, mod3, w_in[0], batch=batch)

    n_freq = head_dim // 4
    tn = 1024
    assert batch * ctx_len == 2048 and fw == tn
    cos_np, sin_np = _rope_tables(seq, head_dim)
    ident = np.ones((batch * ctx_len, head_dim), np.float32)
    cos_t = jnp.asarray(np.concatenate([cos_np, ident]))
    sin_t = jnp.asarray(np.concatenate([sin_np, 0.0 * ident]))
    qg = _rope_lane_order(q_gain, n_freq)
    kg = _rope_lane_order(k_gain, n_freq)
    q_scale = float(head_dim) ** -0.5 * LOG2E
    proj = functools.partial(_proj, h, w_in[0], cos_t=cos_t, sin_t=sin_t, x_rows=x_rows, seq=seq,
                             q_scale=q_scale, tn=tn)
    tiles = lambda a, b: tuple(range(a // tn, b // tn))

    z = proj(gain=qg, kind="silu", col_tiles=tiles(off_zf, off_q) + tiles(off_za, off_gf), m=x_rows)
    q = proj(gain=qg, kind="q", col_tiles=tiles(off_q, off_k), m=x_rows)
    k = proj(gain=kg, kind="k", col_tiles=tiles(off_k, off_v), m=all_rows)
    v = proj(gain=qg, kind="raw", col_tiles=tiles(off_v, off_za), m=all_rows)
    g = proj(gain=qg, kind="sig", col_tiles=tiles(off_gf, off_ga + d), m=x_rows)

    rpb_pad = jnp.pad(rpb[0], ((0, 0), (0, 16 - n_dr), (0, LANES - n_dc)))
    og = _attention(q, k, v, z, rpb_pad, batch=batch, seq=seq, ctx_len=ctx_len, heads=heads, z_col0=fw,
                    win_r=win_r, win_c=win_c)
    z_f = z

    cc, sc = _dft_mats(gd)
    cn, sn = _dft_mats(seq)
    const = lambda a: jnp.asarray(a.astype(np.float32)).astype(BF16)
    half, fk = seq // 2, 256
    nyq = np.zeros((16, seq))
    nyq[0] = cn[half]
    flip = np.zeros((fk, 2 * fk))
    flip[np.arange(fk), fk - np.arange(fk)] = 1.0
    yg = _fourier(u_f, z_f, const(np.concatenate([cc, sc], axis=1)), const(cn[:half]), const(sn[:half]),
                  const(nyq), const(flip), batch=batch, seq=seq, gd=gd)

    out = _merge(yg, og, g, x2d, mod3, w_f_out[0].astype(BF16), w_a_out[0].astype(BF16),
                 w_out[0].astype(BF16), seq=seq)
    return out.reshape(batch, seq, d)
```

```python
import functools

import numpy as np
import jax
import jax.numpy as jnp
from jax import lax
from jax.experimental import pallas as pl
from jax.experimental.pallas import tpu as pltpu

GRID_W = 64
F_GROUPS = 4
ROPE_BASE = 10000.0
EPS = 1e-6
NEG = -1e30
LOG2E = 1.4426950408889634
LANES = 128
VMEM_LIMIT = 56 * 1024 * 1024

BF16 = jnp.bfloat16
F32 = jnp.float32


def _nt_dot(a, b):
    return lax.dot_general(a, b, (((1,), (1,)), ((), ())), preferred_element_type=F32)


def _mod_kernel(c_ref, w_ref, b_ref, o_ref):
    a = jax.nn.silu(c_ref[...]).astype(BF16)
    o_ref[...] = jnp.dot(a, w_ref[...].astype(BF16), preferred_element_type=F32) + b_ref[...]


def _mod(c_all, w_mod, b_mod, tn=512):
    m, d = c_all.shape
    n = w_mod.shape[1]
    return pl.pallas_call(
        _mod_kernel,
        grid=(n // tn,),
        in_specs=[pl.BlockSpec((m, d), lambda j: (0, 0)),
                  pl.BlockSpec((d, tn), lambda j: (0, j)),
                  pl.BlockSpec((1, tn), lambda j: (0, j))],
        out_specs=pl.BlockSpec((m, tn), lambda j: (0, j)),
        out_shape=jax.ShapeDtypeStruct((m, n), F32),
        name="mod",
    )(c_all, w_mod, b_mod)


def _rope_lane_order_cols(w):
    n = w.shape[1]
    quarter = (lax.broadcasted_iota(jnp.int32, w.shape, 1) % LANES) // (LANES // 4)
    up = pltpu.roll(w, n - LANES // 4, 1)
    down = pltpu.roll(w, LANES // 4, 1)
    return jnp.where(quarter == 1, up, jnp.where(quarter == 2, down, w))


def _proj_kernel(kind, mm, rc, q_scale, h_ref, w_ref, gain_ref, cos_ref, sin_ref, o_ref, wb_ref):
    tm = h_ref.shape[0]
    tn = w_ref.shape[1]

    @pl.when(pl.program_id(1) == 0)
    def _():
        _cast_weight_tile(w_ref, wb_ref, rc, kind in ("q", "k"))

    for r in range(0, tm, mm):
        a = jnp.dot(h_ref[r:r + mm, :], wb_ref[...], preferred_element_type=F32)
        if kind == "raw":
            o_ref[r:r + mm, :] = a.astype(BF16)
        elif kind == "silu":
            o_ref[r:r + mm, :] = jax.nn.silu(a).astype(BF16)
        elif kind == "sig":
            o_ref[r:r + mm, :] = jax.nn.sigmoid(a).astype(BF16)
        else:
            gain = gain_ref[...]
            for hh in range(tn // LANES):
                xh = a[:, hh * LANES:(hh + 1) * LANES]
                ms = jnp.mean(xh * xh, axis=-1, keepdims=True)
                xn = xh * lax.rsqrt(ms + EPS) * gain
                xn = xn * cos_ref[r:r + mm, :] + pltpu.roll(xn, LANES // 2, 1) * sin_ref[r:r + mm, :]
                if kind == "q":
                    xn = xn * q_scale
                o_ref[r:r + mm, hh * LANES:(hh + 1) * LANES] = xn.astype(BF16)


def _cast_weight_tile(w_ref, wb_ref, rc, reorder):
    def body(t, carry):
        r = pl.multiple_of(t * rc, rc)
        wt = w_ref[pl.ds(r, rc), :]
        if reorder:
            wt = _rope_lane_order_cols(wt)
        wb_ref[pl.ds(r, rc), :] = wt.astype(BF16)
        return carry
    lax.fori_loop(0, w_ref.shape[0] // rc, body, 0)


def _proj_x_kernel(mm, rc, n_x, x_ref, c_ref, shift_ref, scale_ref, w_ref, o_ref, h_ref, wb_ref):
    @pl.when(pl.program_id(1) == 0)
    def _():
        _cast_weight_tile(w_ref, wb_ref, rc, False)

    def body(t_ref):
        for r in range(0, t_ref.shape[0], mm):
            xs = t_ref[r:r + mm, :]
            ms = jnp.mean(xs * xs, axis=-1, keepdims=True)
            h = (xs * lax.rsqrt(ms + EPS) * (1.0 + scale_ref[0]) + shift_ref[0]).astype(BF16)
            h_ref[r:r + mm, :] = h
            o_ref[r:r + mm, :] = jnp.dot(h, wb_ref[...], preferred_element_type=F32).astype(BF16)

    is_ctx = pl.program_id(1) >= n_x
    pl.when(jnp.logical_not(is_ctx))(functools.partial(body, x_ref))
    pl.when(is_ctx)(functools.partial(body, c_ref))


def _proj_x(x2d, c2d, mod3, w, *, batch, tm=512, tn=1024, mm=256, rc=64):
    m, d = x2d.shape
    n_x, n_c = m // tm, c2d.shape[0] // tm
    tiles_per_mod = m // batch // tm
    mod_row = lambda i: jnp.minimum(i // tiles_per_mod, batch)
    return pl.pallas_call(
        functools.partial(_proj_x_kernel, mm, rc, n_x),
        grid=(1, n_x + n_c),
        in_specs=[
            pl.BlockSpec((tm, d), lambda j, i: (jnp.minimum(i, n_x - 1), 0)),
            pl.BlockSpec((tm, d), lambda j, i: (jnp.maximum(i - n_x, 0), 0)),
            pl.BlockSpec((1, 1, d), lambda j, i: (mod_row(i), 0, 0)),
            pl.BlockSpec((1, 1, d), lambda j, i: (mod_row(i), 0, 1)),
            pl.BlockSpec((d, tn), lambda j, i: (0, j)),
        ],
        out_specs=[pl.BlockSpec((tm, tn), lambda j, i: (i, j)),
                   pl.BlockSpec((tm, d), lambda j, i: (i, 0))],
        out_shape=[jax.ShapeDtypeStruct(((n_x + n_c) * tm, tn), BF16),
                   jax.ShapeDtypeStruct(((n_x + n_c) * tm, d), BF16)],
        scratch_shapes=[pltpu.VMEM((d, tn), BF16)],
        compiler_params=pltpu.CompilerParams(
            dimension_semantics=("arbitrary", "arbitrary"), vmem_limit_bytes=VMEM_LIMIT),
        name="proj_x",
    )(x2d, c2d, mod3, mod3, w)


def _lookup(j, values):
    out = values[-1]
    for idx in range(len(values) - 2, -1, -1):
        out = jnp.where(j == idx, values[idx], out)
    return out


def _proj(h, w, gain, cos_t, sin_t, *, kind, col_tiles, m, x_rows, seq, q_scale, tm=2048, tn=1024, rc=64):
    d = h.shape[1]
    mm = 512 if kind == "raw" else 256
    x_tiles, seq_tiles = x_rows // tm, seq // tm
    rope_tile = lambda i: jnp.where(i < x_tiles, i % seq_tiles, seq_tiles + i - x_tiles)
    kern = functools.partial(_proj_kernel, kind, mm, rc, q_scale)
    n = len(col_tiles) * tn
    return pl.pallas_call(
        kern,
        grid=(len(col_tiles), m // tm),
        in_specs=[
            pl.BlockSpec((tm, d), lambda j, i: (i, 0)),
            pl.BlockSpec((d, tn), lambda j, i: (0, _lookup(j, col_tiles))),
            pl.BlockSpec((1, LANES), lambda j, i: (0, 0)),
            pl.BlockSpec((tm, LANES), lambda j, i: (rope_tile(i), 0)),
            pl.BlockSpec((tm, LANES), lambda j, i: (rope_tile(i), 0)),
        ],
        out_specs=pl.BlockSpec((tm, tn), lambda j, i: (i, j)),
        out_shape=jax.ShapeDtypeStruct((m, n), BF16),
        scratch_shapes=[pltpu.VMEM((d, tn), BF16)],
        compiler_params=pltpu.CompilerParams(
            dimension_semantics=("arbitrary", "arbitrary"), vmem_limit_bytes=VMEM_LIMIT),
        name="proj_" + kind,
    )(h, w, gain, cos_t, sin_t)


Q_ROWS = 2
BAND = 10
AHEAD = 2
N_SLOTS = AHEAD + 1
HEADS_PER_STEP = 2


def _band_start(r0, rows):
    return min(max(r0 - 4, 0), rows - BAND)


def _band_offsets(rows):
    return sorted({r0 - _band_start(r0, rows) for r0 in range(0, rows, Q_ROWS)})


def _valid_slots(r, s0, rows, win_r):
    rs = min(max(r - win_r // 2, 0), rows - win_r)
    return tuple(rs <= s0 + i < rs + win_r for i in range(BAND))


def _build_bias_tables(rpb_ref, bias_ref, rows, win_r, win_c):
    c_io = lax.broadcasted_iota(jnp.int32, (GRID_W, LANES), 0)
    l_io = lax.broadcasted_iota(jnp.int32, (GRID_W, LANES), 1)
    cs = jnp.clip(c_io - win_c // 2, 0, GRID_W - win_c)
    inwin = (l_io >= cs) & (l_io < cs + win_c) & (l_io < GRID_W)
    low = l_io < GRID_W
    neg = jnp.full((GRID_W, LANES), NEG, F32)
    toep = []
    for dr in range(2 * win_r - 1):
        row = jnp.broadcast_to(rpb_ref[dr:dr + 1, :], (GRID_W, LANES))
        t = pltpu.roll(row, LANES - (win_c - 1), 1, stride=1, stride_axis=0)
        toep.append(jnp.where(inwin, t * LOG2E, NEG))
    starts = range(0, rows, Q_ROWS)
    for tb, off in enumerate(_band_offsets(rows)):
        same = [r0 for r0 in starts if r0 - _band_start(r0, rows) == off]
        r0, s0 = same[0], _band_start(same[0], rows)
        for rho in range(Q_ROWS):
            r = r0 + rho
            valid = _valid_slots(r, s0, rows, win_r)
            assert all(_valid_slots(o + rho, _band_start(o, rows), rows, win_r) == valid for o in same)
            blocks = [toep[s0 + i - r + win_r - 1] if valid[i] else neg for i in range(BAND)]
            for p in range(BAND // 2):
                tile = jnp.where(low, blocks[2 * p], pltpu.roll(blocks[2 * p + 1], GRID_W, 1))
                bias_ref[tb, rho * GRID_W:(rho + 1) * GRID_W, p * LANES:(p + 1) * LANES] = tile


def _attn_kernel(rows, win_r, win_c,
                 q_ref, k_ref, v_ref, z_ref, kc_ref, vc_ref, rpb_ref, o_ref, bias_ref, s_ref):
    n_heads = q_ref.shape[1] // LANES

    @pl.when(pl.program_id(1) == 0)
    def _():
        for hh in range(n_heads):
            _build_bias_tables(rpb_ref.at[hh], bias_ref.at[hh], rows, win_r, win_c)

    nq = Q_ROWS * GRID_W
    nk = BAND * GRID_W

    def scores(i, hh, r0):
        s0 = _band_start(r0, rows)
        tb = _band_offsets(rows).index(r0 - s0)
        q0, k0 = r0 * GRID_W, s0 * GRID_W
        hs = slice(hh * LANES, (hh + 1) * LANES)
        qb = q_ref[q0:q0 + nq, hs]
        s_loc = _nt_dot(qb, k_ref[k0:k0 + nk, hs]) + bias_ref[hh, tb]
        s_ctx = _nt_dot(qb, kc_ref[:, hs])
        slot = i % N_SLOTS
        s_ref[slot, :, :nk] = s_loc
        s_ref[slot, :, nk:] = s_ctx
        return jnp.maximum(jnp.max(s_loc, axis=-1, keepdims=True), jnp.max(s_ctx, axis=-1, keepdims=True))

    def finish(i, hh, r0, m):
        q0, k0 = r0 * GRID_W, _band_start(r0, rows) * GRID_W
        hs = slice(hh * LANES, (hh + 1) * LANES)
        p = jnp.exp2(s_ref[i % N_SLOTS] - m)
        den = jnp.sum(p, axis=-1, keepdims=True)
        pb = p.astype(BF16)
        o = (jnp.dot(pb[:, :nk], v_ref[k0:k0 + nk, hs], preferred_element_type=F32)
             + jnp.dot(pb[:, nk:], vc_ref[:, hs], preferred_element_type=F32))
        og = (o / den) * z_ref[q0:q0 + nq, hs].astype(F32)
        o_ref[q0:q0 + nq, hs] = og.astype(BF16)

    items = [(hh, r0) for hh in range(n_heads) for r0 in range(0, rows, Q_ROWS)]
    pending = [scores(i, *item) for i, item in enumerate(items[:AHEAD])]
    for i, item in enumerate(items):
        if i + AHEAD < len(items):
            pending.append(scores(i + AHEAD, *items[i + AHEAD]))
        finish(i, *item, pending.pop(0))


def _attention(q, k, v, z, rpb_pad, *, batch, seq, ctx_len, heads, z_col0, win_r, win_c):
    rows = seq // GRID_W
    kern = functools.partial(_attn_kernel, rows, win_r, win_c)
    hps, width = HEADS_PER_STEP, HEADS_PER_STEP * LANES
    assert heads % hps == 0 and z_col0 % width == 0
    tok = pl.BlockSpec((seq, width), lambda h, b: (b, h))
    ctx0 = batch * seq // ctx_len
    ctx = pl.BlockSpec((ctx_len, width), lambda h, b: (ctx0 + b, h))
    return pl.pallas_call(
        kern,
        grid=(heads // hps, batch),
        in_specs=[tok, tok, tok, pl.BlockSpec((seq, width), lambda h, b: (b, z_col0 // width + h)), ctx, ctx,
                  pl.BlockSpec((hps,) + rpb_pad.shape[1:], lambda h, b: (h, 0, 0))],
        out_specs=tok,
        out_shape=jax.ShapeDtypeStruct((batch * seq, heads * LANES), BF16),
        scratch_shapes=[pltpu.VMEM((hps, len(_band_offsets(rows)), Q_ROWS * GRID_W, BAND * GRID_W), F32),
                        pltpu.VMEM((N_SLOTS, Q_ROWS * GRID_W, BAND * GRID_W + ctx_len), F32)],
        compiler_params=pltpu.CompilerParams(
            dimension_semantics=("arbitrary", "arbitrary"), vmem_limit_bytes=VMEM_LIMIT),
        name="attn",
    )(q, k, v, z, k, v, rpb_pad)


def _dft_mats(n):
    jk = (np.arange(n)[:, None] * np.arange(n)[None, :]) % n
    ang = 2.0 * np.pi * jk.astype(np.float64) / n
    return np.cos(ang) / np.sqrt(n), np.sin(ang) / np.sqrt(n)


def _fourier_kernel(gd, u_ref, zf_ref, csc_ref, ch_ref, sh_ref, nyq_ref, flip_ref, o_ref, a_ref, b_ref, e_ref):
    seq = u_ref.shape[0]
    half = seq // 2
    tk = ch_ref.shape[0]
    fk = flip_ref.shape[0]
    n_first = half // tk
    s = pl.program_id(1)

    @pl.when(s == 0)
    def _():
        rb = 512
        for g in range(u_ref.shape[1] // gd):
            for r in range(0, seq, rb):
                t = jnp.dot(u_ref[r:r + rb, g * gd:(g + 1) * gd], csc_ref[...], preferred_element_type=F32)
                a_ref[r:r + rb, g * gd:(g + 1) * gd] = t[:, :gd].astype(BF16)
                b_ref[r:r + rb, g * gd:(g + 1) * gd] = t[:, gd:].astype(BF16)
        e_ref[half:, :] = jnp.zeros((fk, e_ref.shape[1]), BF16)
        e_ref[half:half + nyq_ref.shape[0], :] = jnp.dot(
            nyq_ref[...], a_ref[...], preferred_element_type=F32).astype(BF16)

    rows = pl.ds(pl.multiple_of(s * tk, tk), tk)
    p = jnp.dot(ch_ref[...], a_ref[...], preferred_element_type=F32)
    q = jnp.dot(sh_ref[...], b_ref[...], preferred_element_type=F32)
    o_ref[rows, :] = ((p - q) * zf_ref[rows, :].astype(F32)).astype(BF16)
    e_ref[rows, :] = (p + q).astype(BF16)

    @pl.when(s == n_first - 1)
    def _():
        for t in range(half // fk):
            base = half - (t + 1) * fk
            y = jnp.dot(flip_ref[...], e_ref[base:base + 2 * fk, :], preferred_element_type=F32)
            out = slice(half + t * fk, half + (t + 1) * fk)
            o_ref[out, :] = (y * zf_ref[out, :].astype(F32)).astype(BF16)


def _fourier(u, zf, csc, ch, sh, nyq, flip, *, batch, seq, gd, tk=512):
    fw = u.shape[1]
    half = seq // 2
    n_first = half // tk
    fk = flip.shape[0]
    kern = functools.partial(_fourier_kernel, gd)
    per_batch = pl.BlockSpec((seq, fw), lambda b, k: (b, 0))
    half_rows = pl.BlockSpec((tk, seq), lambda b, k: (k, 0))
    whole = lambda a: pl.BlockSpec(a.shape, lambda b, k: (0, 0))
    return pl.pallas_call(
        kern,
        grid=(batch, n_first),
        in_specs=[per_batch, per_batch, whole(csc), half_rows, half_rows, whole(nyq), whole(flip)],
        out_specs=per_batch,
        out_shape=jax.ShapeDtypeStruct((batch * seq, fw), BF16),
        scratch_shapes=[pltpu.VMEM((seq, fw), BF16), pltpu.VMEM((seq, fw), BF16),
                        pltpu.VMEM((half + fk, fw), BF16)],
        compiler_params=pltpu.CompilerParams(
            dimension_semantics=("arbitrary", "arbitrary"), vmem_limit_bytes=VMEM_LIMIT),
        name="fourier",
    )(u, zf, csc, ch, sh, nyq, flip)


def _merge_kernel(yg_ref, og_ref, sgf_ref, sga_ref, x_ref, gate_ref, wf_ref, wa_ref, wo_ref, o_ref):
    yf = jnp.dot(yg_ref[...], wf_ref[...], preferred_element_type=F32)
    ya = jnp.dot(og_ref[...], wa_ref[...], preferred_element_type=F32)
    y = sgf_ref[...].astype(F32) * yf + sga_ref[...].astype(F32) * ya
    yo = jnp.dot(y.astype(BF16), wo_ref[...], preferred_element_type=F32)
    o_ref[...] = x_ref[...] + gate_ref[0] * yo


def _merge(yg, og, g, x2d, mod3, wf, wa, wo, *, seq, tm=256):
    m, d = x2d.shape
    tiles_per_seq = seq // tm
    const = lambda shape: pl.BlockSpec(shape, lambda i: (0, 0), pipeline_mode=pl.Buffered(1))
    return pl.pallas_call(
        _merge_kernel,
        grid=(m // tm,),
        in_specs=[
            pl.BlockSpec((tm, yg.shape[1]), lambda i: (i, 0)),
            pl.BlockSpec((tm, d), lambda i: (i, 0)),
            pl.BlockSpec((tm, d), lambda i: (i, 0)),
            pl.BlockSpec((tm, d), lambda i: (i, 1)),
            pl.BlockSpec((tm, d), lambda i: (i, 0)),
            pl.BlockSpec((1, 1, d), lambda i: (i // tiles_per_seq, 0, 2)),
            const(wf.shape), const(wa.shape), const(wo.shape),
        ],
        out_specs=pl.BlockSpec((tm, d), lambda i: (i, 0)),
        out_shape=jax.ShapeDtypeStruct((m, d), F32),
        compiler_params=pltpu.CompilerParams(
            dimension_semantics=("parallel",), vmem_limit_bytes=VMEM_LIMIT),
        name="merge",
    )(yg, og, g, g, x2d, mod3, wf, wa, wo)


def _rope_tables(seq, head_dim):
    n_freq = head_dim // 4
    t = np.arange(seq)
    pos = np.stack([t // GRID_W, t % GRID_W], axis=-1).astype(np.float32)
    inv_freq = (np.float32(ROPE_BASE) ** (-np.arange(n_freq, dtype=np.float32) / np.float32(n_freq)))
    ang = (pos[:, :, None] * inv_freq.astype(np.float32)).astype(np.float64)
    ang = np.broadcast_to(ang[:, None, :, :], (seq, 2, 2, n_freq))
    sign = np.array([-1.0, 1.0])[None, :, None, None]
    return (np.cos(ang).reshape(seq, head_dim).astype(np.float32),
            (np.sin(ang) * sign).reshape(seq, head_dim).astype(np.float32))


def _rope_lane_order(a, n_freq):
    lead = a.shape[:-1]
    return a.reshape(lead + (-1, 2, 2, n_freq)).swapaxes(-3, -2).reshape(a.shape)


def kernel(x, c, ctx, c_ctx, w_mod, b_mod, w_in, q_gain, k_gain, rpb, w_f_out, w_a_out, w_out):
    batch, seq, d = x.shape
    ctx_len = ctx.shape[1]
    depth, heads, n_dr, n_dc = rpb.shape
    assert depth == 1 and w_mod.shape[0] == 1
    head_dim = q_gain.shape[1]
    assert head_dim == LANES and seq % GRID_W == 0
    win_r, win_c = (n_dr + 1) // 2, (n_dc + 1) // 2
    attn_w = heads * head_dim
    fw = w_f_out.shape[1]
    gd = fw // F_GROUPS
    off_zf, off_q = fw, 2 * fw
    off_k, off_v, off_za = off_q + attn_w, off_q + 2 * attn_w, off_q + 3 * attn_w
    off_gf = off_za + attn_w
    off_ga = off_gf + d
    assert w_in.shape[2] == off_ga + d

    c_all = jnp.concatenate([c, c_ctx[None, :], jnp.zeros((16 - batch - 1, d), F32)], axis=0)
    mod = _mod(c_all, w_mod[0], b_mod)
    mod3 = mod.reshape(16, 1, 3 * d)

    x2d = x.reshape(batch * seq, d)
    c2d = ctx.reshape(batch * ctx_len, d)
    x_rows, all_rows = batch * seq, batch * (seq + ctx_len)
    u_f, h = _proj_x(x2d, c2d, mod3, w_in[0], batch=batch)

    n_freq = head_dim // 4
    tn = 1024
    assert batch * ctx_len == 2048 and fw == tn
    cos_np, sin_np = _rope_tables(seq, head_dim)
    ident = np.ones((batch * ctx_len, head_dim), np.float32)
    cos_t = jnp.asarray(np.concatenate([cos_np, ident]))
    sin_t = jnp.asarray(np.concatenate([sin_np, 0.0 * ident]))
    qg = _rope_lane_order(q_gain, n_freq)
    kg = _rope_lane_order(k_gain, n_freq)
    q_scale = float(head_dim) ** -0.5 * LOG2E
    proj = functools.partial(_proj, h, w_in[0], cos_t=cos_t, sin_t=sin_t, x_rows=x_rows, seq=seq,
                             q_scale=q_scale, tn=tn)
    tiles = lambda a, b: tuple(range(a // tn, b // tn))

    z = proj(gain=qg, kind="silu", col_tiles=tiles(off_zf, off_q) + tiles(off_za, off_gf), m=x_rows)
    q = proj(gain=qg, kind="q", col_tiles=tiles(off_q, off_k), m=x_rows)
    k = proj(gain=kg, kind="k", col_tiles=tiles(off_k, off_v), m=all_rows)
    v = proj(gain=qg, kind="raw", col_tiles=tiles(off_v, off_za), m=all_rows)
    g = proj(gain=qg, kind="sig", col_tiles=tiles(off_gf, off_ga + d), m=x_rows)

    rpb_pad = jnp.pad(rpb[0], ((0, 0), (0, 16 - n_dr), (0, LANES - n_dc)))
    og = _attention(q, k, v, z, rpb_pad, batch=batch, seq=seq, ctx_len=ctx_len, heads=heads, z_col0=fw,
                    win_r=win_r, win_c=win_c)
    z_f = z

    cc, sc = _dft_mats(gd)
    cn, sn = _dft_mats(seq)
    const = lambda a: jnp.asarray(a.astype(np.float32)).astype(BF16)
    half, fk = seq // 2, 256
    nyq = np.zeros((16, seq))
    nyq[0] = cn[half]
    flip = np.zeros((fk, 2 * fk))
    flip[np.arange(fk), fk - np.arange(fk)] = 1.0
    yg = _fourier(u_f, z_f, const(np.concatenate([cc, sc], axis=1)), const(cn[:half]), const(sn[:half]),
                  const(nyq), const(flip), batch=batch, seq=seq, gd=gd)

    out = _merge(yg, og, g, x2d, mod3, w_f_out[0].astype(BF16), w_a_out[0].astype(BF16),
                 w_out[0].astype(BF16), seq=seq)
    return out.reshape(batch, seq, d)
```

```python
import functools

import numpy as np
import jax
import jax.numpy as jnp
from jax import lax
from jax.experimental import pallas as pl
from jax.experimental.pallas import tpu as pltpu

GRID_W = 64
F_GROUPS = 4
ROPE_BASE = 10000.0
EPS = 1e-6
NEG = -1e30
LOG2E = 1.4426950408889634
LANES = 128
VMEM_LIMIT = 56 * 1024 * 1024

MOD_TN = 512
PROJ_TN = 1024
PROJ_TM = 2048
PROJ_X_TM = 512
DOT_ROWS = 256
RAW_DOT_ROWS = 512
CAST_ROWS = 64
FOURIER_TK = 512
FOURIER_FLIP = 256
MERGE_TM = 256

BF16 = jnp.bfloat16
F32 = jnp.float32


def _nt_dot(a, b):
    return lax.dot_general(a, b, (((1,), (1,)), ((), ())), preferred_element_type=F32)


def _mod_kernel(c_ref, w_ref, b_ref, o_ref):
    a = jax.nn.silu(c_ref[...]).astype(BF16)
    o_ref[...] = jnp.dot(a, w_ref[...].astype(BF16), preferred_element_type=F32) + b_ref[...]


def _mod(c_all, w_mod, b_mod, tn=MOD_TN):
    m, d = c_all.shape
    n = w_mod.shape[1]
    return pl.pallas_call(
        _mod_kernel,
        grid=(n // tn,),
        in_specs=[pl.BlockSpec((m, d), lambda j: (0, 0)),
                  pl.BlockSpec((d, tn), lambda j: (0, j)),
                  pl.BlockSpec((1, tn), lambda j: (0, j))],
        out_specs=pl.BlockSpec((m, tn), lambda j: (0, j)),
        out_shape=jax.ShapeDtypeStruct((m, n), F32),
        name="mod",
    )(c_all, w_mod, b_mod)


def _rope_lane_order_cols(w):
    n = w.shape[1]
    quarter = (lax.broadcasted_iota(jnp.int32, w.shape, 1) % LANES) // (LANES // 4)
    up = pltpu.roll(w, n - LANES // 4, 1)
    down = pltpu.roll(w, LANES // 4, 1)
    return jnp.where(quarter == 1, up, jnp.where(quarter == 2, down, w))


def _proj_kernel(kind, mm, rc, h_ref, w_ref, gain_ref, cos_ref, sin_ref, o_ref, wb_ref):
    tm = h_ref.shape[0]
    tn = w_ref.shape[1]

    @pl.when(pl.program_id(1) == 0)
    def _():
        _cast_weight_tile(w_ref, wb_ref, rc, kind == "qk")

    for r in range(0, tm, mm):
        a = jnp.dot(h_ref[r:r + mm, :], wb_ref[...], preferred_element_type=F32)
        if kind == "raw":
            o_ref[r:r + mm, :] = a.astype(BF16)
        elif kind == "silu":
            o_ref[r:r + mm, :] = jax.nn.silu(a).astype(BF16)
        elif kind == "sig":
            o_ref[r:r + mm, :] = jax.nn.sigmoid(a).astype(BF16)
        else:
            gain = gain_ref[...]
            for hh in range(tn // LANES):
                xh = a[:, hh * LANES:(hh + 1) * LANES]
                ms = jnp.mean(xh * xh, axis=-1, keepdims=True)
                xn = xh * lax.rsqrt(ms + EPS) * gain
                xn = xn * cos_ref[r:r + mm, :] + pltpu.roll(xn, LANES // 2, 1) * sin_ref[r:r + mm, :]
                o_ref[r:r + mm, hh * LANES:(hh + 1) * LANES] = xn.astype(BF16)


def _cast_weight_tile(w_ref, wb_ref, rc, reorder):
    def body(t, carry):
        r = pl.multiple_of(t * rc, rc)
        wt = w_ref[pl.ds(r, rc), :]
        if reorder:
            wt = _rope_lane_order_cols(wt)
        wb_ref[pl.ds(r, rc), :] = wt.astype(BF16)
        return carry
    lax.fori_loop(0, w_ref.shape[0] // rc, body, 0)


def _proj_x_kernel(mm, rc, n_x, x_ref, c_ref, shift_ref, scale_ref, w_ref, o_ref, h_ref, wb_ref):
    @pl.when(pl.program_id(1) == 0)
    def _():
        _cast_weight_tile(w_ref, wb_ref, rc, False)

    def body(t_ref):
        for r in range(0, t_ref.shape[0], mm):
            xs = t_ref[r:r + mm, :]
            ms = jnp.mean(xs * xs, axis=-1, keepdims=True)
            h = (xs * lax.rsqrt(ms + EPS) * (1.0 + scale_ref[0]) + shift_ref[0]).astype(BF16)
            h_ref[r:r + mm, :] = h
            o_ref[r:r + mm, :] = jnp.dot(h, wb_ref[...], preferred_element_type=F32).astype(BF16)

    is_ctx = pl.program_id(1) >= n_x
    pl.when(jnp.logical_not(is_ctx))(functools.partial(body, x_ref))
    pl.when(is_ctx)(functools.partial(body, c_ref))


def _proj_x(x2d, c2d, mod3, w, *, batch, tm=PROJ_X_TM, tn=PROJ_TN, mm=DOT_ROWS, rc=CAST_ROWS):
    m, d = x2d.shape
    n_x, n_c = m // tm, c2d.shape[0] // tm
    tiles_per_mod = m // batch // tm
    mod_row = lambda i: jnp.minimum(i // tiles_per_mod, batch)
    return pl.pallas_call(
        functools.partial(_proj_x_kernel, mm, rc, n_x),
        grid=(1, n_x + n_c),
        in_specs=[
            pl.BlockSpec((tm, d), lambda j, i: (jnp.minimum(i, n_x - 1), 0)),
            pl.BlockSpec((tm, d), lambda j, i: (jnp.maximum(i - n_x, 0), 0)),
            pl.BlockSpec((1, 1, d), lambda j, i: (mod_row(i), 0, 0)),
            pl.BlockSpec((1, 1, d), lambda j, i: (mod_row(i), 0, 1)),
            pl.BlockSpec((d, tn), lambda j, i: (0, j)),
        ],
        out_specs=[pl.BlockSpec((tm, tn), lambda j, i: (i, j)),
                   pl.BlockSpec((tm, d), lambda j, i: (i, 0))],
        out_shape=[jax.ShapeDtypeStruct(((n_x + n_c) * tm, tn), BF16),
                   jax.ShapeDtypeStruct(((n_x + n_c) * tm, d), BF16)],
        scratch_shapes=[pltpu.VMEM((d, tn), BF16)],
        compiler_params=pltpu.CompilerParams(
            dimension_semantics=("arbitrary", "arbitrary"), vmem_limit_bytes=VMEM_LIMIT),
        name="proj_x",
    )(x2d, c2d, mod3, mod3, w)


def _lookup(j, values):
    out = values[-1]
    for idx in range(len(values) - 2, -1, -1):
        out = jnp.where(j == idx, values[idx], out)
    return out


def _proj(h, w, gain, cos_t, sin_t, *, kind, col_tiles, m, x_rows, seq, tm=PROJ_TM, tn=PROJ_TN, rc=CAST_ROWS):
    d = h.shape[1]
    mm = RAW_DOT_ROWS if kind == "raw" else DOT_ROWS
    x_tiles, seq_tiles = x_rows // tm, seq // tm
    rope_tile = lambda i: jnp.where(i < x_tiles, i % seq_tiles, seq_tiles + i - x_tiles)
    kern = functools.partial(_proj_kernel, kind, mm, rc)
    n = len(col_tiles) * tn
    return pl.pallas_call(
        kern,
        grid=(len(col_tiles), m // tm),
        in_specs=[
            pl.BlockSpec((tm, d), lambda j, i: (i, 0)),
            pl.BlockSpec((d, tn), lambda j, i: (0, _lookup(j, col_tiles))),
            pl.BlockSpec((1, LANES), lambda j, i: (0, 0)),
            pl.BlockSpec((tm, LANES), lambda j, i: (rope_tile(i), 0)),
            pl.BlockSpec((tm, LANES), lambda j, i: (rope_tile(i), 0)),
        ],
        out_specs=pl.BlockSpec((tm, tn), lambda j, i: (i, j)),
        out_shape=jax.ShapeDtypeStruct((m, n), BF16),
        scratch_shapes=[pltpu.VMEM((d, tn), BF16)],
        compiler_params=pltpu.CompilerParams(
            dimension_semantics=("arbitrary", "arbitrary"), vmem_limit_bytes=VMEM_LIMIT),
        name="proj_" + kind,
    )(h, w, gain, cos_t, sin_t)


Q_ROWS = 2
AHEAD = 2
N_SLOTS = AHEAD + 1
HEADS_PER_STEP = 2


class _BandPlan:
    def __init__(self, rows, win_r):
        self.rows, self.win_r = rows, win_r
        band = win_r + Q_ROWS - 1
        self.band = band + band % 2
        self.groups = list(range(0, rows, Q_ROWS))
        self.start = {r0: min(max(r0 - win_r // 2, 0), rows - self.band) for r0 in self.groups}
        self.offsets = sorted({r0 - s0 for r0, s0 in self.start.items()})

    def table(self, r0):
        return self.offsets.index(r0 - self.start[r0])

    def valid_slots(self, r, s0):
        rs = min(max(r - self.win_r // 2, 0), self.rows - self.win_r)
        return tuple(rs <= s0 + i < rs + self.win_r for i in range(self.band))


def _build_bias_tables(rpb_ref, bias_ref, plan, win_c):
    win_r = plan.win_r
    c_io = lax.broadcasted_iota(jnp.int32, (GRID_W, LANES), 0)
    l_io = lax.broadcasted_iota(jnp.int32, (GRID_W, LANES), 1)
    cs = jnp.clip(c_io - win_c // 2, 0, GRID_W - win_c)
    inwin = (l_io >= cs) & (l_io < cs + win_c) & (l_io < GRID_W)
    low = l_io < GRID_W
    neg = jnp.full((GRID_W, LANES), NEG, F32)
    toep = []
    for dr in range(2 * win_r - 1):
        row = jnp.broadcast_to(rpb_ref[dr:dr + 1, :], (GRID_W, LANES))
        t = pltpu.roll(row, LANES - (win_c - 1), 1, stride=1, stride_axis=0)
        toep.append(jnp.where(inwin, t * LOG2E, NEG))
    for tb in range(len(plan.offsets)):
        same = [r0 for r0 in plan.groups if plan.table(r0) == tb]
        r0, s0 = same[0], plan.start[same[0]]
        for rho in range(Q_ROWS):
            r = r0 + rho
            valid = plan.valid_slots(r, s0)
            assert all(plan.valid_slots(o + rho, plan.start[o]) == valid for o in same)
            blocks = [toep[s0 + i - r + win_r - 1] if valid[i] else neg for i in range(plan.band)]
            for p in range(plan.band // 2):
                tile = jnp.where(low, blocks[2 * p], pltpu.roll(blocks[2 * p + 1], GRID_W, 1))
                bias_ref[tb, rho * GRID_W:(rho + 1) * GRID_W, p * LANES:(p + 1) * LANES] = tile


def _attn_kernel(plan, win_c,
                 q_ref, k_ref, v_ref, z_ref, kc_ref, vc_ref, rpb_ref, o_ref, bias_ref, s_ref):
    n_heads = q_ref.shape[1] // LANES

    @pl.when(pl.program_id(1) == 0)
    def _():
        for hh in range(n_heads):
            _build_bias_tables(rpb_ref.at[hh], bias_ref.at[hh], plan, win_c)

    nq = Q_ROWS * GRID_W
    nk = plan.band * GRID_W

    def scores(i, hh, r0):
        q0, k0 = r0 * GRID_W, plan.start[r0] * GRID_W
        hs = slice(hh * LANES, (hh + 1) * LANES)
        qb = q_ref[q0:q0 + nq, hs]
        s_loc = _nt_dot(qb, k_ref[k0:k0 + nk, hs]) + bias_ref[hh, plan.table(r0)]
        s_ctx = _nt_dot(qb, kc_ref[:, hs])
        slot = i % N_SLOTS
        s_ref[slot, :, :nk] = s_loc
        s_ref[slot, :, nk:] = s_ctx
        return jnp.maximum(jnp.max(s_loc, axis=-1, keepdims=True), jnp.max(s_ctx, axis=-1, keepdims=True))

    def finish(i, hh, r0, m):
        q0, k0 = r0 * GRID_W, plan.start[r0] * GRID_W
        hs = slice(hh * LANES, (hh + 1) * LANES)
        p = jnp.exp2(s_ref[i % N_SLOTS] - m)
        den = jnp.sum(p, axis=-1, keepdims=True)
        pb = p.astype(BF16)
        o = (jnp.dot(pb[:, :nk], v_ref[k0:k0 + nk, hs], preferred_element_type=F32)
             + jnp.dot(pb[:, nk:], vc_ref[:, hs], preferred_element_type=F32))
        og = (o / den) * z_ref[q0:q0 + nq, hs].astype(F32)
        o_ref[q0:q0 + nq, hs] = og.astype(BF16)

    items = [(hh, r0) for hh in range(n_heads) for r0 in plan.groups]
    pending = [scores(i, *item) for i, item in enumerate(items[:AHEAD])]
    for i, item in enumerate(items):
        if i + AHEAD < len(items):
            pending.append(scores(i + AHEAD, *items[i + AHEAD]))
        finish(i, *item, pending.pop(0))


def _attention(q, k, v, z, rpb_pad, *, batch, seq, ctx_len, heads, z_col0, win_r, win_c):
    plan = _BandPlan(seq // GRID_W, win_r)
    kern = functools.partial(_attn_kernel, plan, win_c)
    nq, nk = Q_ROWS * GRID_W, plan.band * GRID_W
    hps, width = HEADS_PER_STEP, HEADS_PER_STEP * LANES
    assert heads % hps == 0 and z_col0 % width == 0
    tok = pl.BlockSpec((seq, width), lambda h, b: (b, h))
    ctx0 = batch * seq // ctx_len
    ctx = pl.BlockSpec((ctx_len, width), lambda h, b: (ctx0 + b, h))
    return pl.pallas_call(
        kern,
        grid=(heads // hps, batch),
        in_specs=[tok, tok, tok, pl.BlockSpec((seq, width), lambda h, b: (b, z_col0 // width + h)), ctx, ctx,
                  pl.BlockSpec((hps,) + rpb_pad.shape[1:], lambda h, b: (h, 0, 0))],
        out_specs=tok,
        out_shape=jax.ShapeDtypeStruct((batch * seq, heads * LANES), BF16),
        scratch_shapes=[pltpu.VMEM((hps, len(plan.offsets), nq, nk), F32),
                        pltpu.VMEM((N_SLOTS, nq, nk + ctx_len), F32)],
        compiler_params=pltpu.CompilerParams(
            dimension_semantics=("arbitrary", "arbitrary"), vmem_limit_bytes=VMEM_LIMIT),
        name="attn",
    )(q, k, v, z, k, v, rpb_pad)


def _dft_mats(n):
    jk = (np.arange(n)[:, None] * np.arange(n)[None, :]) % n
    ang = 2.0 * np.pi * jk.astype(np.float64) / n
    return np.cos(ang) / np.sqrt(n), np.sin(ang) / np.sqrt(n)


def _fourier_kernel(gd, u_ref, zf_ref, csc_ref, ch_ref, sh_ref, nyq_ref, flip_ref, o_ref, a_ref, b_ref, e_ref):
    seq = u_ref.shape[0]
    half = seq // 2
    tk = ch_ref.shape[0]
    fk = flip_ref.shape[0]
    n_first = half // tk
    s = pl.program_id(1)

    @pl.when(s == 0)
    def _():
        rb = 2 * DOT_ROWS
        for g in range(u_ref.shape[1] // gd):
            for r in range(0, seq, rb):
                t = jnp.dot(u_ref[r:r + rb, g * gd:(g + 1) * gd], csc_ref[...], preferred_element_type=F32)
                a_ref[r:r + rb, g * gd:(g + 1) * gd] = t[:, :gd].astype(BF16)
                b_ref[r:r + rb, g * gd:(g + 1) * gd] = t[:, gd:].astype(BF16)
        e_ref[half:, :] = jnp.zeros((fk, e_ref.shape[1]), BF16)
        e_ref[half:half + nyq_ref.shape[0], :] = jnp.dot(
            nyq_ref[...], a_ref[...], preferred_element_type=F32).astype(BF16)

    rows = pl.ds(pl.multiple_of(s * tk, tk), tk)
    p = jnp.dot(ch_ref[...], a_ref[...], preferred_element_type=F32)
    q = jnp.dot(sh_ref[...], b_ref[...], preferred_element_type=F32)
    o_ref[rows, :] = ((p - q) * zf_ref[rows, :].astype(F32)).astype(BF16)
    e_ref[rows, :] = (p + q).astype(BF16)

    @pl.when(s == n_first - 1)
    def _():
        for t in range(half // fk):
            base = half - (t + 1) * fk
            y = jnp.dot(flip_ref[...], e_ref[base:base + 2 * fk, :], preferred_element_type=F32)
            out = slice(half + t * fk, half + (t + 1) * fk)
            o_ref[out, :] = (y * zf_ref[out, :].astype(F32)).astype(BF16)


def _fourier(u, zf, csc, ch, sh, nyq, flip, *, batch, seq, gd, tk=FOURIER_TK):
    fw = u.shape[1]
    half = seq // 2
    n_first = half // tk
    fk = flip.shape[0]
    kern = functools.partial(_fourier_kernel, gd)
    per_batch = pl.BlockSpec((seq, fw), lambda b, k: (b, 0))
    half_rows = pl.BlockSpec((tk, seq), lambda b, k: (k, 0))
    whole = lambda a: pl.BlockSpec(a.shape, lambda b, k: (0, 0))
    return pl.pallas_call(
        kern,
        grid=(batch, n_first),
        in_specs=[per_batch, per_batch, whole(csc), half_rows, half_rows, whole(nyq), whole(flip)],
        out_specs=per_batch,
        out_shape=jax.ShapeDtypeStruct((batch * seq, fw), BF16),
        scratch_shapes=[pltpu.VMEM((seq, fw), BF16), pltpu.VMEM((seq, fw), BF16),
                        pltpu.VMEM((half + fk, fw), BF16)],
        compiler_params=pltpu.CompilerParams(
            dimension_semantics=("arbitrary", "arbitrary"), vmem_limit_bytes=VMEM_LIMIT),
        name="fourier",
    )(u, zf, csc, ch, sh, nyq, flip)


def _merge_kernel(yg_ref, og_ref, sgf_ref, sga_ref, x_ref, gate_ref, wf_ref, wa_ref, wo_ref, o_ref):
    yf = jnp.dot(yg_ref[...], wf_ref[...], preferred_element_type=F32)
    ya = jnp.dot(og_ref[...], wa_ref[...], preferred_element_type=F32)
    y = sgf_ref[...].astype(F32) * yf + sga_ref[...].astype(F32) * ya
    yo = jnp.dot(y.astype(BF16), wo_ref[...], preferred_element_type=F32)
    o_ref[...] = x_ref[...] + gate_ref[0] * yo


def _merge(yg, og, g, x2d, mod3, wf, wa, wo, *, seq, tm=MERGE_TM):
    m, d = x2d.shape
    tiles_per_seq = seq // tm
    const = lambda shape: pl.BlockSpec(shape, lambda i: (0, 0), pipeline_mode=pl.Buffered(1))
    return pl.pallas_call(
        _merge_kernel,
        grid=(m // tm,),
        in_specs=[
            pl.BlockSpec((tm, yg.shape[1]), lambda i: (i, 0)),
            pl.BlockSpec((tm, d), lambda i: (i, 0)),
            pl.BlockSpec((tm, d), lambda i: (i, 0)),
            pl.BlockSpec((tm, d), lambda i: (i, 1)),
            pl.BlockSpec((tm, d), lambda i: (i, 0)),
            pl.BlockSpec((1, 1, d), lambda i: (i // tiles_per_seq, 0, 2)),
            const(wf.shape), const(wa.shape), const(wo.shape),
        ],
        out_specs=pl.BlockSpec((tm, d), lambda i: (i, 0)),
        out_shape=jax.ShapeDtypeStruct((m, d), F32),
        compiler_params=pltpu.CompilerParams(
            dimension_semantics=("parallel",), vmem_limit_bytes=VMEM_LIMIT),
        name="merge",
    )(yg, og, g, g, x2d, mod3, wf, wa, wo)


def _rope_tables(seq, head_dim):
    n_freq = head_dim // 4
    t = np.arange(seq)
    pos = np.stack([t // GRID_W, t % GRID_W], axis=-1).astype(np.float32)
    inv_freq = (np.float32(ROPE_BASE) ** (-np.arange(n_freq, dtype=np.float32) / np.float32(n_freq)))
    ang = (pos[:, :, None] * inv_freq.astype(np.float32)).astype(np.float64)
    ang = np.broadcast_to(ang[:, None, :, :], (seq, 2, 2, n_freq))
    sign = np.array([-1.0, 1.0])[None, :, None, None]
    return (np.cos(ang).reshape(seq, head_dim).astype(np.float32),
            (np.sin(ang) * sign).reshape(seq, head_dim).astype(np.float32))


def _rope_lane_order(a, n_freq):
    lead = a.shape[:-1]
    return a.reshape(lead + (-1, 2, 2, n_freq)).swapaxes(-3, -2).reshape(a.shape)


def kernel(x, c, ctx, c_ctx, w_mod, b_mod, w_in, q_gain, k_gain, rpb, w_f_out, w_a_out, w_out):
    batch, seq, d = x.shape
    ctx_len = ctx.shape[1]
    depth, heads, n_dr, n_dc = rpb.shape
    assert depth == 1 and w_mod.shape[0] == 1
    head_dim = q_gain.shape[1]
    assert head_dim == LANES and seq % GRID_W == 0
    win_r, win_c = (n_dr + 1) // 2, (n_dc + 1) // 2
    attn_w = heads * head_dim
    fw = w_f_out.shape[1]
    gd = fw // F_GROUPS
    off_zf, off_q = fw, 2 * fw
    off_k, off_v, off_za = off_q + attn_w, off_q + 2 * attn_w, off_q + 3 * attn_w
    off_gf = off_za + attn_w
    off_ga = off_gf + d
    assert w_in.shape[2] == off_ga + d

    c_all = jnp.concatenate([c, c_ctx[None, :], jnp.zeros((16 - batch - 1, d), F32)], axis=0)
    mod = _mod(c_all, w_mod[0], b_mod)
    mod3 = mod.reshape(16, 1, 3 * d)

    x2d = x.reshape(batch * seq, d)
    c2d = ctx.reshape(batch * ctx_len, d)
    x_rows, all_rows = batch * seq, batch * (seq + ctx_len)
    u_f, h = _proj_x(x2d, c2d, mod3, w_in[0], batch=batch)

    n_freq = head_dim // 4
    tn = PROJ_TN
    assert batch * ctx_len == PROJ_TM and fw == tn
    cos_np, sin_np = _rope_tables(seq, head_dim)
    ident = np.ones((batch * ctx_len, head_dim), np.float32)
    cos_t = jnp.asarray(np.concatenate([cos_np, ident]))
    sin_t = jnp.asarray(np.concatenate([sin_np, 0.0 * ident]))
    qg = _rope_lane_order(q_gain, n_freq) * (float(head_dim) ** -0.5 * LOG2E)
    kg = _rope_lane_order(k_gain, n_freq)
    proj = functools.partial(_proj, h, w_in[0], cos_t=cos_t, sin_t=sin_t, x_rows=x_rows, seq=seq, tn=tn)
    tiles = lambda a, b: tuple(range(a // tn, b // tn))

    z = proj(gain=qg, kind="silu", col_tiles=tiles(off_zf, off_q) + tiles(off_za, off_gf), m=x_rows)
    q = proj(gain=qg, kind="qk", col_tiles=tiles(off_q, off_k), m=x_rows)
    k = proj(gain=kg, kind="qk", col_tiles=tiles(off_k, off_v), m=all_rows)
    v = proj(gain=qg, kind="raw", col_tiles=tiles(off_v, off_za), m=all_rows)
    g = proj(gain=qg, kind="sig", col_tiles=tiles(off_gf, off_ga + d), m=x_rows)

    rpb_pad = jnp.pad(rpb[0], ((0, 0), (0, 16 - n_dr), (0, LANES - n_dc)))
    og = _attention(q, k, v, z, rpb_pad, batch=batch, seq=seq, ctx_len=ctx_len, heads=heads, z_col0=fw,
                    win_r=win_r, win_c=win_c)

    cc, sc = _dft_mats(gd)
    cn, sn = _dft_mats(seq)
    const = lambda a: jnp.asarray(a.astype(np.float32)).astype(BF16)
    half, fk = seq // 2, FOURIER_FLIP
    nyq = np.zeros((16, seq))
    nyq[0] = cn[half]
    flip = np.zeros((fk, 2 * fk))
    flip[np.arange(fk), fk - np.arange(fk)] = 1.0
    yg = _fourier(u_f, z, const(np.concatenate([cc, sc], axis=1)), const(cn[:half]), const(sn[:half]),
                  const(nyq), const(flip), batch=batch, seq=seq, gd=gd)

    out = _merge(yg, og, g, x2d, mod3, w_f_out[0].astype(BF16), w_a_out[0].astype(BF16),
                 w_out[0].astype(BF16), seq=seq)
    return out.reshape(batch, seq, d)
```

```python
import functools

import numpy as np
import jax
import jax.numpy as jnp
from jax import lax
from jax.experimental import pallas as pl
from jax.experimental.pallas import tpu as pltpu

GRID_W = 64
F_GROUPS = 4
ROPE_BASE = 10000.0
EPS = 1e-6
NEG = -1e30
LOG2E = 1.4426950408889634
LANES = 128
VMEM_LIMIT = 56 * 1024 * 1024

MOD_TN = 512
PROJ_TN = 1024
PROJ_TM = 2048
PROJ_X_TM = 512
DOT_ROWS = 256
RAW_DOT_ROWS = 512
CAST_ROWS = 64
FOURIER_TK = 512
FOURIER_FLIP = 256
MERGE_TM = 256

BF16 = jnp.bfloat16
F32 = jnp.float32


def _nt_dot(a, b):
    return lax.dot_general(a, b, (((1,), (1,)), ((), ())), preferred_element_type=F32)


def _mod_kernel(c_ref, w_ref, b_ref, o_ref):
    a = jax.nn.silu(c_ref[...]).astype(BF16)
    o_ref[...] = jnp.dot(a, w_ref[...].astype(BF16), preferred_element_type=F32) + b_ref[...]


def _mod(c_all, w_mod, b_mod, tn=MOD_TN):
    m, d = c_all.shape
    n = w_mod.shape[1]
    return pl.pallas_call(
        _mod_kernel,
        grid=(n // tn,),
        in_specs=[pl.BlockSpec((m, d), lambda j: (0, 0)),
                  pl.BlockSpec((d, tn), lambda j: (0, j)),
                  pl.BlockSpec((1, tn), lambda j: (0, j))],
        out_specs=pl.BlockSpec((m, tn), lambda j: (0, j)),
        out_shape=jax.ShapeDtypeStruct((m, n), F32),
        name="mod",
    )(c_all, w_mod, b_mod)


def _rope_lane_order_cols(w):
    n = w.shape[1]
    quarter = (lax.broadcasted_iota(jnp.int32, w.shape, 1) % LANES) // (LANES // 4)
    up = pltpu.roll(w, n - LANES // 4, 1)
    down = pltpu.roll(w, LANES // 4, 1)
    return jnp.where(quarter == 1, up, jnp.where(quarter == 2, down, w))


def _proj_kernel(kind, mm, rc, h_ref, w_ref, gain_ref, cos_ref, sin_ref, o_ref, wb_ref):
    tm = h_ref.shape[0]
    tn = w_ref.shape[1]

    @pl.when(pl.program_id(1) == 0)
    def _():
        _cast_weight_tile(w_ref, wb_ref, rc, kind == "qk")

    for r in range(0, tm, mm):
        a = jnp.dot(h_ref[r:r + mm, :], wb_ref[...], preferred_element_type=F32)
        if kind == "raw":
            o_ref[r:r + mm, :] = a.astype(BF16)
        elif kind == "silu":
            o_ref[r:r + mm, :] = jax.nn.silu(a).astype(BF16)
        elif kind == "sig":
            o_ref[r:r + mm, :] = jax.nn.sigmoid(a).astype(BF16)
        else:
            gain = gain_ref[...]
            for hh in range(tn // LANES):
                xh = a[:, hh * LANES:(hh + 1) * LANES]
                ms = jnp.mean(xh * xh, axis=-1, keepdims=True)
                xn = xh * lax.rsqrt(ms + EPS) * gain
                xn = xn * cos_ref[r:r + mm, :] + pltpu.roll(xn, LANES // 2, 1) * sin_ref[r:r + mm, :]
                o_ref[r:r + mm, hh * LANES:(hh + 1) * LANES] = xn.astype(BF16)


def _cast_weight_tile(w_ref, wb_ref, rc, reorder):
    def body(t, carry):
        r = pl.multiple_of(t * rc, rc)
        wt = w_ref[pl.ds(r, rc), :]
        if reorder:
            wt = _rope_lane_order_cols(wt)
        wb_ref[pl.ds(r, rc), :] = wt.astype(BF16)
        return carry
    lax.fori_loop(0, w_ref.shape[0] // rc, body, 0)


def _proj_x_kernel(mm, rc, n_x, x_ref, c_ref, shift_ref, scale_ref, w_ref, o_ref, h_ref, wb_ref):
    @pl.when(pl.program_id(1) == 0)
    def _():
        _cast_weight_tile(w_ref, wb_ref, rc, False)

    def body(t_ref):
        for r in range(0, t_ref.shape[0], mm):
            xs = t_ref[r:r + mm, :]
            ms = jnp.mean(xs * xs, axis=-1, keepdims=True)
            h = (xs * lax.rsqrt(ms + EPS) * (1.0 + scale_ref[0]) + shift_ref[0]).astype(BF16)
            h_ref[r:r + mm, :] = h
            o_ref[r:r + mm, :] = jnp.dot(h, wb_ref[...], preferred_element_type=F32).astype(BF16)

    is_ctx = pl.program_id(1) >= n_x
    pl.when(jnp.logical_not(is_ctx))(functools.partial(body, x_ref))
    pl.when(is_ctx)(functools.partial(body, c_ref))


def _proj_x(x2d, c2d, mod3, w, *, batch, tm=PROJ_X_TM, tn=PROJ_TN, mm=DOT_ROWS, rc=CAST_ROWS):
    m, d = x2d.shape
    n_x, n_c = m // tm, c2d.shape[0] // tm
    tiles_per_mod = m // batch // tm
    mod_row = lambda i: jnp.minimum(i // tiles_per_mod, batch)
    return pl.pallas_call(
        functools.partial(_proj_x_kernel, mm, rc, n_x),
        grid=(1, n_x + n_c),
        in_specs=[
            pl.BlockSpec((tm, d), lambda j, i: (jnp.minimum(i, n_x - 1), 0)),
            pl.BlockSpec((tm, d), lambda j, i: (jnp.maximum(i - n_x, 0), 0)),
            pl.BlockSpec((1, 1, d), lambda j, i: (mod_row(i), 0, 0)),
            pl.BlockSpec((1, 1, d), lambda j, i: (mod_row(i), 0, 1)),
            pl.BlockSpec((d, tn), lambda j, i: (0, j)),
        ],
        out_specs=[pl.BlockSpec((tm, tn), lambda j, i: (i, j)),
                   pl.BlockSpec((tm, d), lambda j, i: (i, 0))],
        out_shape=[jax.ShapeDtypeStruct(((n_x + n_c) * tm, tn), BF16),
                   jax.ShapeDtypeStruct(((n_x + n_c) * tm, d), BF16)],
        scratch_shapes=[pltpu.VMEM((d, tn), BF16)],
        compiler_params=pltpu.CompilerParams(
            dimension_semantics=("arbitrary", "arbitrary"), vmem_limit_bytes=VMEM_LIMIT),
        name="proj_x",
    )(x2d, c2d, mod3, mod3, w)


def _lookup(j, values):
    out = values[-1]
    for idx in range(len(values) - 2, -1, -1):
        out = jnp.where(j == idx, values[idx], out)
    return out


def _proj(h, w, gain, cos_t, sin_t, *, kind, col_tiles, m, x_rows, seq, tm=PROJ_TM, tn=PROJ_TN, rc=CAST_ROWS):
    d = h.shape[1]
    mm = RAW_DOT_ROWS if kind == "raw" else DOT_ROWS
    x_tiles, seq_tiles = x_rows // tm, seq // tm
    rope_tile = lambda i: jnp.where(i < x_tiles, i % seq_tiles, seq_tiles + i - x_tiles)
    kern = functools.partial(_proj_kernel, kind, mm, rc)
    n = len(col_tiles) * tn
    return pl.pallas_call(
        kern,
        grid=(len(col_tiles), m // tm),
        in_specs=[
            pl.BlockSpec((tm, d), lambda j, i: (i, 0)),
            pl.BlockSpec((d, tn), lambda j, i: (0, _lookup(j, col_tiles))),
            pl.BlockSpec((1, LANES), lambda j, i: (0, 0)),
            pl.BlockSpec((tm, LANES), lambda j, i: (rope_tile(i), 0)),
            pl.BlockSpec((tm, LANES), lambda j, i: (rope_tile(i), 0)),
        ],
        out_specs=pl.BlockSpec((tm, tn), lambda j, i: (i, j)),
        out_shape=jax.ShapeDtypeStruct((m, n), BF16),
        scratch_shapes=[pltpu.VMEM((d, tn), BF16)],
        compiler_params=pltpu.CompilerParams(
            dimension_semantics=("arbitrary", "arbitrary"), vmem_limit_bytes=VMEM_LIMIT),
        name="proj_" + kind,
    )(h, w, gain, cos_t, sin_t)


Q_ROWS = 4
AHEAD = 2
N_SLOTS = AHEAD + 1
HEADS_PER_STEP = 2


class _BandPlan:
    def __init__(self, rows, win_r):
        self.rows, self.win_r = rows, win_r
        band = win_r + Q_ROWS - 1
        self.band = band + band % 2
        self.groups = list(range(0, rows, Q_ROWS))
        self.start = {r0: min(max(r0 - win_r // 2, 0), rows - self.band) for r0 in self.groups}
        self.offsets = sorted({r0 - s0 for r0, s0 in self.start.items()})

    def table(self, r0):
        return self.offsets.index(r0 - self.start[r0])

    def valid_slots(self, r, s0):
        rs = min(max(r - self.win_r // 2, 0), self.rows - self.win_r)
        return tuple(rs <= s0 + i < rs + self.win_r for i in range(self.band))


def _build_bias_tables(rpb_ref, bias_ref, plan, win_c):
    win_r = plan.win_r
    k_io = lax.broadcasted_iota(jnp.int32, (GRID_W, LANES), 0)
    q_io = lax.broadcasted_iota(jnp.int32, (GRID_W, LANES), 1)
    cs = jnp.clip(q_io - win_c // 2, 0, GRID_W - win_c)
    inwin = (k_io >= cs) & (k_io < cs + win_c) & (q_io < GRID_W)
    low = q_io < GRID_W
    neg = jnp.full((GRID_W, LANES), NEG, F32)
    toep = []
    for dr in range(2 * win_r - 1):
        row = jnp.broadcast_to(rpb_ref[dr:dr + 1, :], (GRID_W, LANES))
        t = pltpu.roll(row, LANES - (win_c - 1), 1, stride=1, stride_axis=0)
        toep.append(jnp.where(inwin, t * LOG2E, NEG))
    for tb in range(len(plan.offsets)):
        same = [r0 for r0 in plan.groups if plan.table(r0) == tb]
        r0, s0 = same[0], plan.start[same[0]]
        valid = [plan.valid_slots(r0 + rho, s0) for rho in range(Q_ROWS)]
        assert all(plan.valid_slots(o + rho, plan.start[o]) == valid[rho] for o in same for rho in range(Q_ROWS))
        block = lambda i, rho: toep[s0 + i - (r0 + rho) + win_r - 1] if valid[rho][i] else neg
        for i in range(plan.band):
            for p in range(Q_ROWS // 2):
                tile = jnp.where(low, block(i, 2 * p), pltpu.roll(block(i, 2 * p + 1), GRID_W, 1))
                bias_ref[tb, i * GRID_W:(i + 1) * GRID_W, p * LANES:(p + 1) * LANES] = tile


def _attn_kernel(plan, win_c,
                 q_ref, k_ref, v_ref, z_ref, kc_ref, vc_ref, rpb_ref, o_ref, bias_ref, s_ref, vt_ref, vct_ref):
    n_heads = q_ref.shape[1] // LANES
    seq, ctx_len = v_ref.shape[0], vc_ref.shape[0]

    @pl.when(pl.program_id(1) == 0)
    def _():
        for hh in range(n_heads):
            _build_bias_tables(rpb_ref.at[hh], bias_ref.at[hh], plan, win_c)

    for hh in range(n_heads):
        hs = slice(hh * LANES, (hh + 1) * LANES)
        for r in range(0, seq, LANES):
            vt_ref[hh, :, r:r + LANES] = v_ref[r:r + LANES, hs].T
        for r in range(0, ctx_len, LANES):
            vct_ref[hh, :, r:r + LANES] = vc_ref[r:r + LANES, hs].T

    nq = Q_ROWS * GRID_W
    nk = plan.band * GRID_W

    def scores(i, hh, r0):
        q0, k0 = r0 * GRID_W, plan.start[r0] * GRID_W
        hs = slice(hh * LANES, (hh + 1) * LANES)
        qb = q_ref[q0:q0 + nq, hs]
        s_loc = _nt_dot(k_ref[k0:k0 + nk, hs], qb) + bias_ref[hh, plan.table(r0)]
        s_ctx = _nt_dot(kc_ref[:, hs], qb)
        slot = i % N_SLOTS
        s_ref[slot, :nk, :] = s_loc
        s_ref[slot, nk:, :] = s_ctx
        return jnp.maximum(jnp.max(s_loc, axis=0, keepdims=True), jnp.max(s_ctx, axis=0, keepdims=True))

    def finish(i, hh, r0, m):
        q0, k0 = r0 * GRID_W, plan.start[r0] * GRID_W
        hs = slice(hh * LANES, (hh + 1) * LANES)
        p = jnp.exp2(s_ref[i % N_SLOTS] - m)
        den = jnp.sum(p, axis=0, keepdims=True)
        pb = p.astype(BF16)
        ot = (jnp.dot(vt_ref[hh, :, k0:k0 + nk], pb[:nk, :], preferred_element_type=F32)
              + jnp.dot(vct_ref[hh], pb[nk:, :], preferred_element_type=F32))
        og = (ot / den).T * z_ref[q0:q0 + nq, hs].astype(F32)
        o_ref[q0:q0 + nq, hs] = og.astype(BF16)

    items = [(hh, r0) for hh in range(n_heads) for r0 in plan.groups]
    pending = [scores(i, *item) for i, item in enumerate(items[:AHEAD])]
    for i, item in enumerate(items):
        if i + AHEAD < len(items):
            pending.append(scores(i + AHEAD, *items[i + AHEAD]))
        finish(i, *item, pending.pop(0))


def _attention(q, k, v, z, rpb_pad, *, batch, seq, ctx_len, heads, z_col0, win_r, win_c):
    plan = _BandPlan(seq // GRID_W, win_r)
    kern = functools.partial(_attn_kernel, plan, win_c)
    nq, nk = Q_ROWS * GRID_W, plan.band * GRID_W
    hps, width = HEADS_PER_STEP, HEADS_PER_STEP * LANES
    assert heads % hps == 0 and z_col0 % width == 0
    tok = pl.BlockSpec((seq, width), lambda h, b: (b, h))
    ctx0 = batch * seq // ctx_len
    ctx = pl.BlockSpec((ctx_len, width), lambda h, b: (ctx0 + b, h))
    return pl.pallas_call(
        kern,
        grid=(heads // hps, batch),
        in_specs=[tok, tok, tok, pl.BlockSpec((seq, width), lambda h, b: (b, z_col0 // width + h)), ctx, ctx,
                  pl.BlockSpec((hps,) + rpb_pad.shape[1:], lambda h, b: (h, 0, 0))],
        out_specs=tok,
        out_shape=jax.ShapeDtypeStruct((batch * seq, heads * LANES), BF16),
        scratch_shapes=[pltpu.VMEM((hps, len(plan.offsets), nk, nq), F32),
                        pltpu.VMEM((N_SLOTS, nk + ctx_len, nq), F32),
                        pltpu.VMEM((hps, LANES, seq), BF16), pltpu.VMEM((hps, LANES, ctx_len), BF16)],
        compiler_params=pltpu.CompilerParams(
            dimension_semantics=("arbitrary", "arbitrary"), vmem_limit_bytes=VMEM_LIMIT),
        name="attn",
    )(q, k, v, z, k, v, rpb_pad)


def _dft_mats(n):
    jk = (np.arange(n)[:, None] * np.arange(n)[None, :]) % n
    ang = 2.0 * np.pi * jk.astype(np.float64) / n
    return np.cos(ang) / np.sqrt(n), np.sin(ang) / np.sqrt(n)


def _fourier_kernel(gd, u_ref, zf_ref, csc_ref, ch_ref, sh_ref, nyq_ref, flip_ref, o_ref, a_ref, b_ref, e_ref):
    seq = u_ref.shape[0]
    half = seq // 2
    tk = ch_ref.shape[0]
    fk = flip_ref.shape[0]
    n_first = half // tk
    s = pl.program_id(1)

    @pl.when(s == 0)
    def _():
        rb = 2 * DOT_ROWS
        for g in range(u_ref.shape[1] // gd):
            for r in range(0, seq, rb):
                t = jnp.dot(u_ref[r:r + rb, g * gd:(g + 1) * gd], csc_ref[...], preferred_element_type=F32)
                a_ref[r:r + rb, g * gd:(g + 1) * gd] = t[:, :gd].astype(BF16)
                b_ref[r:r + rb, g * gd:(g + 1) * gd] = t[:, gd:].astype(BF16)
        e_ref[half:, :] = jnp.zeros((fk, e_ref.shape[1]), BF16)
        e_ref[half:half + nyq_ref.shape[0], :] = jnp.dot(
            nyq_ref[...], a_ref[...], preferred_element_type=F32).astype(BF16)

    rows = pl.ds(pl.multiple_of(s * tk, tk), tk)
    p = jnp.dot(ch_ref[...], a_ref[...], preferred_element_type=F32)
    q = jnp.dot(sh_ref[...], b_ref[...], preferred_element_type=F32)
    o_ref[rows, :] = ((p - q) * zf_ref[rows, :].astype(F32)).astype(BF16)
    e_ref[rows, :] = (p + q).astype(BF16)

    @pl.when(s == n_first - 1)
    def _():
        for t in range(half // fk):
            base = half - (t + 1) * fk
            y = jnp.dot(flip_ref[...], e_ref[base:base + 2 * fk, :], preferred_element_type=F32)
            out = slice(half + t * fk, half + (t + 1) * fk)
            o_ref[out, :] = (y * zf_ref[out, :].astype(F32)).astype(BF16)


def _fourier(u, zf, csc, ch, sh, nyq, flip, *, batch, seq, gd, tk=FOURIER_TK):
    fw = u.shape[1]
    half = seq // 2
    n_first = half // tk
    fk = flip.shape[0]
    kern = functools.partial(_fourier_kernel, gd)
    per_batch = pl.BlockSpec((seq, fw), lambda b, k: (b, 0))
    half_rows = pl.BlockSpec((tk, seq), lambda b, k: (k, 0))
    whole = lambda a: pl.BlockSpec(a.shape, lambda b, k: (0, 0))
    return pl.pallas_call(
        kern,
        grid=(batch, n_first),
        in_specs=[per_batch, per_batch, whole(csc), half_rows, half_rows, whole(nyq), whole(flip)],
        out_specs=per_batch,
        out_shape=jax.ShapeDtypeStruct((batch * seq, fw), BF16),
        scratch_shapes=[pltpu.VMEM((seq, fw), BF16), pltpu.VMEM((seq, fw), BF16),
                        pltpu.VMEM((half + fk, fw), BF16)],
        compiler_params=pltpu.CompilerParams(
            dimension_semantics=("arbitrary", "arbitrary"), vmem_limit_bytes=VMEM_LIMIT),
        name="fourier",
    )(u, zf, csc, ch, sh, nyq, flip)


def _merge_kernel(yg_ref, og_ref, sgf_ref, sga_ref, x_ref, gate_ref, wf_ref, wa_ref, wo_ref, o_ref):
    yf = jnp.dot(yg_ref[...], wf_ref[...], preferred_element_type=F32)
    ya = jnp.dot(og_ref[...], wa_ref[...], preferred_element_type=F32)
    y = sgf_ref[...].astype(F32) * yf + sga_ref[...].astype(F32) * ya
    yo = jnp.dot(y.astype(BF16), wo_ref[...], preferred_element_type=F32)
    o_ref[...] = x_ref[...] + gate_ref[0] * yo


def _merge(yg, og, g, x2d, mod3, wf, wa, wo, *, seq, tm=MERGE_TM):
    m, d = x2d.shape
    tiles_per_seq = seq // tm
    const = lambda shape: pl.BlockSpec(shape, lambda i: (0, 0), pipeline_mode=pl.Buffered(1))
    return pl.pallas_call(
        _merge_kernel,
        grid=(m // tm,),
        in_specs=[
            pl.BlockSpec((tm, yg.shape[1]), lambda i: (i, 0)),
            pl.BlockSpec((tm, d), lambda i: (i, 0)),
            pl.BlockSpec((tm, d), lambda i: (i, 0)),
            pl.BlockSpec((tm, d), lambda i: (i, 1)),
            pl.BlockSpec((tm, d), lambda i: (i, 0)),
            pl.BlockSpec((1, 1, d), lambda i: (i // tiles_per_seq, 0, 2)),
            const(wf.shape), const(wa.shape), const(wo.shape),
        ],
        out_specs=pl.BlockSpec((tm, d), lambda i: (i, 0)),
        out_shape=jax.ShapeDtypeStruct((m, d), F32),
        compiler_params=pltpu.CompilerParams(
            dimension_semantics=("parallel",), vmem_limit_bytes=VMEM_LIMIT),
        name="merge",
    )(yg, og, g, g, x2d, mod3, wf, wa, wo)


def _rope_tables(seq, head_dim):
    n_freq = head_dim // 4
    t = np.arange(seq)
    pos = np.stack([t // GRID_W, t % GRID_W], axis=-1).astype(np.float32)
    inv_freq = (np.float32(ROPE_BASE) ** (-np.arange(n_freq, dtype=np.float32) / np.float32(n_freq)))
    ang = (pos[:, :, None] * inv_freq.astype(np.float32)).astype(np.float64)
    ang = np.broadcast_to(ang[:, None, :, :], (seq, 2, 2, n_freq))
    sign = np.array([-1.0, 1.0])[None, :, None, None]
    return (np.cos(ang).reshape(seq, head_dim).astype(np.float32),
            (np.sin(ang) * sign).reshape(seq, head_dim).astype(np.float32))


def _rope_lane_order(a, n_freq):
    lead = a.shape[:-1]
    return a.reshape(lead + (-1, 2, 2, n_freq)).swapaxes(-3, -2).reshape(a.shape)


def kernel(x, c, ctx, c_ctx, w_mod, b_mod, w_in, q_gain, k_gain, rpb, w_f_out, w_a_out, w_out):
    batch, seq, d = x.shape
    ctx_len = ctx.shape[1]
    depth, heads, n_dr, n_dc = rpb.shape
    assert depth == 1 and w_mod.shape[0] == 1
    head_dim = q_gain.shape[1]
    assert head_dim == LANES and seq % GRID_W == 0
    win_r, win_c = (n_dr + 1) // 2, (n_dc + 1) // 2
    attn_w = heads * head_dim
    fw = w_f_out.shape[1]
    gd = fw // F_GROUPS
    off_zf, off_q = fw, 2 * fw
    off_k, off_v, off_za = off_q + attn_w, off_q + 2 * attn_w, off_q + 3 * attn_w
    off_gf = off_za + attn_w
    off_ga = off_gf + d
    assert w_in.shape[2] == off_ga + d

    c_all = jnp.concatenate([c, c_ctx[None, :], jnp.zeros((16 - batch - 1, d), F32)], axis=0)
    mod = _mod(c_all, w_mod[0], b_mod)
    mod3 = mod.reshape(16, 1, 3 * d)

    x2d = x.reshape(batch * seq, d)
    c2d = ctx.reshape(batch * ctx_len, d)
    x_rows, all_rows = batch * seq, batch * (seq + ctx_len)
    u_f, h = _proj_x(x2d, c2d, mod3, w_in[0], batch=batch)

    n_freq = head_dim // 4
    tn = PROJ_TN
    assert batch * ctx_len == PROJ_TM and fw == tn
    cos_np, sin_np = _rope_tables(seq, head_dim)
    ident = np.ones((batch * ctx_len, head_dim), np.float32)
    cos_t = jnp.asarray(np.concatenate([cos_np, ident]))
    sin_t = jnp.asarray(np.concatenate([sin_np, 0.0 * ident]))
    qg = _rope_lane_order(q_gain, n_freq) * (float(head_dim) ** -0.5 * LOG2E)
    kg = _rope_lane_order(k_gain, n_freq)
    proj = functools.partial(_proj, h, w_in[0], cos_t=cos_t, sin_t=sin_t, x_rows=x_rows, seq=seq, tn=tn)
    tiles = lambda a, b: tuple(range(a // tn, b // tn))

    z = proj(gain=qg, kind="silu", col_tiles=tiles(off_zf, off_q) + tiles(off_za, off_gf), m=x_rows)
    q = proj(gain=qg, kind="qk", col_tiles=tiles(off_q, off_k), m=x_rows)
    k = proj(gain=kg, kind="qk", col_tiles=tiles(off_k, off_v), m=all_rows)
    v = proj(gain=qg, kind="raw", col_tiles=tiles(off_v, off_za), m=all_rows)
    g = proj(gain=qg, kind="sig", col_tiles=tiles(off_gf, off_ga + d), m=x_rows)

    rpb_pad = jnp.pad(rpb[0][..., ::-1], ((0, 0), (0, 16 - n_dr), (0, LANES - n_dc)))
    og = _attention(q, k, v, z, rpb_pad, batch=batch, seq=seq, ctx_len=ctx_len, heads=heads, z_col0=fw,
                    win_r=win_r, win_c=win_c)

    cc, sc = _dft_mats(gd)
    cn, sn = _dft_mats(seq)
    const = lambda a: jnp.asarray(a.astype(np.float32)).astype(BF16)
    half, fk = seq // 2, FOURIER_FLIP
    nyq = np.zeros((16, seq))
    nyq[0] = cn[half]
    flip = np.zeros((fk, 2 * fk))
    flip[np.arange(fk), fk - np.arange(fk)] = 1.0
    yg = _fourier(u_f, z, const(np.concatenate([cc, sc], axis=1)), const(cn[:half]), const(sn[:half]),
                  const(nyq), const(flip), batch=batch, seq=seq, gd=gd)

    out = _merge(yg, og, g, x2d, mod3, w_f_out[0].astype(BF16), w_a_out[0].astype(BF16),
                 w_out[0].astype(BF16), seq=seq)
    return out.reshape(batch, seq, d)
```

```python
import functools

import numpy as np
import jax
import jax.numpy as jnp
from jax import lax
from jax.experimental import pallas as pl
from jax.experimental.pallas import tpu as pltpu

GRID_W = 64
F_GROUPS = 4
ROPE_BASE = 10000.0
EPS = 1e-6
NEG = -1e30
LOG2E = 1.4426950408889634
LANES = 128
VMEM_LIMIT = 56 * 1024 * 1024

MOD_TN = 512
PROJ_TN = 1024
PROJ_TM = 2048
PROJ_X_TM = 512
DOT_ROWS = 128
CAST_ROWS = 64
FOURIER_TK = 512
FOURIER_FLIP = 256
MERGE_TM = 256

BF16 = jnp.bfloat16
F32 = jnp.float32


def _nt_dot(a, b):
    return lax.dot_general(a, b, (((1,), (1,)), ((), ())), preferred_element_type=F32)


def _mod_kernel(c_ref, w_ref, b_ref, o_ref):
    a = jax.nn.silu(c_ref[...]).astype(BF16)
    o_ref[...] = jnp.dot(a, w_ref[...].astype(BF16), preferred_element_type=F32) + b_ref[...]


def _mod(c_all, w_mod, b_mod, tn=MOD_TN):
    m, d = c_all.shape
    n = w_mod.shape[1]
    return pl.pallas_call(
        _mod_kernel,
        grid=(n // tn,),
        in_specs=[pl.BlockSpec((m, d), lambda j: (0, 0)),
                  pl.BlockSpec((d, tn), lambda j: (0, j)),
                  pl.BlockSpec((1, tn), lambda j: (0, j))],
        out_specs=pl.BlockSpec((m, tn), lambda j: (0, j)),
        out_shape=jax.ShapeDtypeStruct((m, n), F32),
        name="mod",
    )(c_all, w_mod, b_mod)


def _rope_lane_order_cols(w):
    n = w.shape[1]
    quarter = (lax.broadcasted_iota(jnp.int32, w.shape, 1) % LANES) // (LANES // 4)
    up = pltpu.roll(w, n - LANES // 4, 1)
    down = pltpu.roll(w, LANES // 4, 1)
    return jnp.where(quarter == 1, up, jnp.where(quarter == 2, down, w))


def _proj_kernel(kind, mm, rc, h_ref, w_ref, gain_ref, cos_ref, sin_ref, o_ref, wb_ref):
    tm = h_ref.shape[0]
    tn = w_ref.shape[1]

    @pl.when(pl.program_id(1) == 0)
    def _():
        _cast_weight_tile(w_ref, wb_ref, rc, kind == "qk")

    for r in range(0, tm, mm):
        a = jnp.dot(h_ref[r:r + mm, :], wb_ref[...], preferred_element_type=F32)
        if kind == "raw":
            o_ref[r:r + mm, :] = a.astype(BF16)
        elif kind == "silu":
            o_ref[r:r + mm, :] = jax.nn.silu(a).astype(BF16)
        elif kind == "sig":
            o_ref[r:r + mm, :] = jax.nn.sigmoid(a).astype(BF16)
        else:
            gain = gain_ref[...]
            for hh in range(tn // LANES):
                xh = a[:, hh * LANES:(hh + 1) * LANES]
                ms = jnp.mean(xh * xh, axis=-1, keepdims=True)
                xn = xh * lax.rsqrt(ms + EPS) * gain
                xn = xn * cos_ref[r:r + mm, :] + pltpu.roll(xn, LANES // 2, 1) * sin_ref[r:r + mm, :]
                o_ref[r:r + mm, hh * LANES:(hh + 1) * LANES] = xn.astype(BF16)


def _cast_weight_tile(w_ref, wb_ref, rc, reorder):
    def body(t, carry):
        r = pl.multiple_of(t * rc, rc)
        wt = w_ref[pl.ds(r, rc), :]
        if reorder:
            wt = _rope_lane_order_cols(wt)
        wb_ref[pl.ds(r, rc), :] = wt.astype(BF16)
        return carry
    lax.fori_loop(0, w_ref.shape[0] // rc, body, 0)


def _proj_x_kernel(mm, rc, n_x, x_ref, c_ref, shift_ref, scale_ref, w_ref, o_ref, h_ref, wb_ref):
    @pl.when(pl.program_id(1) == 0)
    def _():
        _cast_weight_tile(w_ref, wb_ref, rc, False)

    def body(t_ref):
        for r in range(0, t_ref.shape[0], mm):
            xs = t_ref[r:r + mm, :]
            ms = jnp.mean(xs * xs, axis=-1, keepdims=True)
            h = (xs * lax.rsqrt(ms + EPS) * (1.0 + scale_ref[0]) + shift_ref[0]).astype(BF16)
            h_ref[r:r + mm, :] = h
            o_ref[r:r + mm, :] = jnp.dot(h, wb_ref[...], preferred_element_type=F32).astype(BF16)

    is_ctx = pl.program_id(1) >= n_x
    pl.when(jnp.logical_not(is_ctx))(functools.partial(body, x_ref))
    pl.when(is_ctx)(functools.partial(body, c_ref))


def _proj_x(x2d, c2d, mod3, w, *, batch, tm=PROJ_X_TM, tn=PROJ_TN, mm=DOT_ROWS, rc=CAST_ROWS):
    m, d = x2d.shape
    n_x, n_c = m // tm, c2d.shape[0] // tm
    tiles_per_mod = m // batch // tm
    mod_row = lambda i: jnp.minimum(i // tiles_per_mod, batch)
    return pl.pallas_call(
        functools.partial(_proj_x_kernel, mm, rc, n_x),
        grid=(1, n_x + n_c),
        in_specs=[
            pl.BlockSpec((tm, d), lambda j, i: (jnp.minimum(i, n_x - 1), 0)),
            pl.BlockSpec((tm, d), lambda j, i: (jnp.maximum(i - n_x, 0), 0)),
            pl.BlockSpec((1, 1, d), lambda j, i: (mod_row(i), 0, 0)),
            pl.BlockSpec((1, 1, d), lambda j, i: (mod_row(i), 0, 1)),
            pl.BlockSpec((d, tn), lambda j, i: (0, j)),
        ],
        out_specs=[pl.BlockSpec((tm, tn), lambda j, i: (i, j)),
                   pl.BlockSpec((tm, d), lambda j, i: (i, 0))],
        out_shape=[jax.ShapeDtypeStruct(((n_x + n_c) * tm, tn), BF16),
                   jax.ShapeDtypeStruct(((n_x + n_c) * tm, d), BF16)],
        scratch_shapes=[pltpu.VMEM((d, tn), BF16)],
        compiler_params=pltpu.CompilerParams(
            dimension_semantics=("arbitrary", "arbitrary"), vmem_limit_bytes=VMEM_LIMIT),
        name="proj_x",
    )(x2d, c2d, mod3, mod3, w)


def _lookup(j, values):
    out = values[-1]
    for idx in range(len(values) - 2, -1, -1):
        out = jnp.where(j == idx, values[idx], out)
    return out


def _proj(h, w, gain, cos_t, sin_t, *, kind, col_tiles, m, x_rows, seq, tm=PROJ_TM, tn=PROJ_TN, rc=CAST_ROWS):
    d = h.shape[1]
    mm = DOT_ROWS
    x_tiles, seq_tiles = x_rows // tm, seq // tm
    rope_tile = lambda i: jnp.where(i < x_tiles, i % seq_tiles, seq_tiles + i - x_tiles)
    kern = functools.partial(_proj_kernel, kind, mm, rc)
    n = len(col_tiles) * tn
    return pl.pallas_call(
        kern,
        grid=(len(col_tiles), m // tm),
        in_specs=[
            pl.BlockSpec((tm, d), lambda j, i: (i, 0)),
            pl.BlockSpec((d, tn), lambda j, i: (0, _lookup(j, col_tiles))),
            pl.BlockSpec((1, LANES), lambda j, i: (0, 0)),
            pl.BlockSpec((tm, LANES), lambda j, i: (rope_tile(i), 0)),
            pl.BlockSpec((tm, LANES), lambda j, i: (rope_tile(i), 0)),
        ],
        out_specs=pl.BlockSpec((tm, tn), lambda j, i: (i, j)),
        out_shape=jax.ShapeDtypeStruct((m, n), BF16),
        scratch_shapes=[pltpu.VMEM((d, tn), BF16)],
        compiler_params=pltpu.CompilerParams(
            dimension_semantics=("arbitrary", "arbitrary"), vmem_limit_bytes=VMEM_LIMIT),
        name="proj_" + kind,
    )(h, w, gain, cos_t, sin_t)


Q_ROWS = 2
AHEAD = 2
N_SLOTS = AHEAD + 1
HEADS_PER_STEP = 2


class _BandPlan:
    def __init__(self, rows, win_r):
        self.rows, self.win_r = rows, win_r
        band = win_r + Q_ROWS - 1
        self.band = band + band % 2
        self.groups = list(range(0, rows, Q_ROWS))
        self.start = {r0: min(max(r0 - win_r // 2, 0), rows - self.band) for r0 in self.groups}
        self.offsets = sorted({r0 - s0 for r0, s0 in self.start.items()})

    def table(self, r0):
        return self.offsets.index(r0 - self.start[r0])

    def valid_slots(self, r, s0):
        rs = min(max(r - self.win_r // 2, 0), self.rows - self.win_r)
        return tuple(rs <= s0 + i < rs + self.win_r for i in range(self.band))


def _build_bias_tables(rpb_ref, bias_ref, plan, win_c):
    win_r = plan.win_r
    c_io = lax.broadcasted_iota(jnp.int32, (GRID_W, LANES), 0)
    l_io = lax.broadcasted_iota(jnp.int32, (GRID_W, LANES), 1)
    cs = jnp.clip(c_io - win_c // 2, 0, GRID_W - win_c)
    inwin = (l_io >= cs) & (l_io < cs + win_c) & (l_io < GRID_W)
    low = l_io < GRID_W
    neg = jnp.full((GRID_W, LANES), NEG, F32)
    toep = []
    for dr in range(2 * win_r - 1):
        row = jnp.broadcast_to(rpb_ref[dr:dr + 1, :], (GRID_W, LANES))
        t = pltpu.roll(row, LANES - (win_c - 1), 1, stride=1, stride_axis=0)
        toep.append(jnp.where(inwin, t * LOG2E, NEG))
    for tb in range(len(plan.offsets)):
        same = [r0 for r0 in plan.groups if plan.table(r0) == tb]
        r0, s0 = same[0], plan.start[same[0]]
        for rho in range(Q_ROWS):
            r = r0 + rho
            valid = plan.valid_slots(r, s0)
            assert all(plan.valid_slots(o + rho, plan.start[o]) == valid for o in same)
            blocks = [toep[s0 + i - r + win_r - 1] if valid[i] else neg for i in range(plan.band)]
            for p in range(plan.band // 2):
                tile = jnp.where(low, blocks[2 * p], pltpu.roll(blocks[2 * p + 1], GRID_W, 1))
                bias_ref[tb, rho * GRID_W:(rho + 1) * GRID_W, p * LANES:(p + 1) * LANES] = tile


def _attn_kernel(plan, win_c,
                 q_ref, k_ref, v_ref, z_ref, kc_ref, vc_ref, rpb_ref, o_ref, bias_ref, s_ref):
    n_heads = q_ref.shape[1] // LANES

    @pl.when(pl.program_id(1) == 0)
    def _():
        for hh in range(n_heads):
            _build_bias_tables(rpb_ref.at[hh], bias_ref.at[hh], plan, win_c)

    nq = Q_ROWS * GRID_W
    nk = plan.band * GRID_W

    def scores(i, hh, r0):
        q0, k0 = r0 * GRID_W, plan.start[r0] * GRID_W
        hs = slice(hh * LANES, (hh + 1) * LANES)
        qb = q_ref[q0:q0 + nq, hs]
        s_loc = _nt_dot(qb, k_ref[k0:k0 + nk, hs]) + bias_ref[hh, plan.table(r0)]
        s_ctx = _nt_dot(qb, kc_ref[:, hs])
        slot = i % N_SLOTS
        s_ref[slot, :, :nk] = s_loc
        s_ref[slot, :, nk:] = s_ctx
        return jnp.maximum(jnp.max(s_loc, axis=-1, keepdims=True), jnp.max(s_ctx, axis=-1, keepdims=True))

    def finish(i, hh, r0, m):
        q0, k0 = r0 * GRID_W, plan.start[r0] * GRID_W
        hs = slice(hh * LANES, (hh + 1) * LANES)
        p = jnp.exp2(s_ref[i % N_SLOTS] - m)
        den = jnp.sum(p, axis=-1, keepdims=True)
        pb = p.astype(BF16)
        o = (jnp.dot(pb[:, :nk], v_ref[k0:k0 + nk, hs], preferred_element_type=F32)
             + jnp.dot(pb[:, nk:], vc_ref[:, hs], preferred_element_type=F32))
        og = (o / den) * z_ref[q0:q0 + nq, hs].astype(F32)
        o_ref[q0:q0 + nq, hs] = og.astype(BF16)

    items = [(hh, r0) for hh in range(n_heads) for r0 in plan.groups]
    pending = [scores(i, *item) for i, item in enumerate(items[:AHEAD])]
    for i, item in enumerate(items):
        if i + AHEAD < len(items):
            pending.append(scores(i + AHEAD, *items[i + AHEAD]))
        finish(i, *item, pending.pop(0))


def _attention(q, k, v, z, rpb_pad, *, batch, seq, ctx_len, heads, z_col0, win_r, win_c):
    plan = _BandPlan(seq // GRID_W, win_r)
    kern = functools.partial(_attn_kernel, plan, win_c)
    nq, nk = Q_ROWS * GRID_W, plan.band * GRID_W
    hps, width = HEADS_PER_STEP, HEADS_PER_STEP * LANES
    assert heads % hps == 0 and z_col0 % width == 0
    tok = pl.BlockSpec((seq, width), lambda h, b: (b, h))
    ctx0 = batch * seq // ctx_len
    ctx = pl.BlockSpec((ctx_len, width), lambda h, b: (ctx0 + b, h))
    return pl.pallas_call(
        kern,
        grid=(heads // hps, batch),
        in_specs=[tok, tok, tok, pl.BlockSpec((seq, width), lambda h, b: (b, z_col0 // width + h)), ctx, ctx,
                  pl.BlockSpec((hps,) + rpb_pad.shape[1:], lambda h, b: (h, 0, 0))],
        out_specs=tok,
        out_shape=jax.ShapeDtypeStruct((batch * seq, heads * LANES), BF16),
        scratch_shapes=[pltpu.VMEM((hps, len(plan.offsets), nq, nk), F32),
                        pltpu.VMEM((N_SLOTS, nq, nk + ctx_len), F32)],
        compiler_params=pltpu.CompilerParams(
            dimension_semantics=("arbitrary", "arbitrary"), vmem_limit_bytes=VMEM_LIMIT),
        name="attn",
    )(q, k, v, z, k, v, rpb_pad)


def _dft_mats(n):
    jk = (np.arange(n)[:, None] * np.arange(n)[None, :]) % n
    ang = 2.0 * np.pi * jk.astype(np.float64) / n
    return np.cos(ang) / np.sqrt(n), np.sin(ang) / np.sqrt(n)


def _fourier_kernel(gd, u_ref, zf_ref, csc_ref, ch_ref, sh_ref, nyq_ref, flip_ref, o_ref, a_ref, b_ref, e_ref):
    seq = u_ref.shape[0]
    half = seq // 2
    tk = ch_ref.shape[0]
    fk = flip_ref.shape[0]
    n_first = half // tk
    s = pl.program_id(1)

    @pl.when(s == 0)
    def _():
        rb = 4 * DOT_ROWS
        for g in range(u_ref.shape[1] // gd):
            for r in range(0, seq, rb):
                t = jnp.dot(u_ref[r:r + rb, g * gd:(g + 1) * gd], csc_ref[...], preferred_element_type=F32)
                a_ref[r:r + rb, g * gd:(g + 1) * gd] = t[:, :gd].astype(BF16)
                b_ref[r:r + rb, g * gd:(g + 1) * gd] = t[:, gd:].astype(BF16)
        e_ref[half:, :] = jnp.zeros((fk, e_ref.shape[1]), BF16)
        e_ref[half:half + nyq_ref.shape[0], :] = jnp.dot(
            nyq_ref[...], a_ref[...], preferred_element_type=F32).astype(BF16)

    rows = pl.ds(pl.multiple_of(s * tk, tk), tk)
    p = jnp.dot(ch_ref[...], a_ref[...], preferred_element_type=F32)
    q = jnp.dot(sh_ref[...], b_ref[...], preferred_element_type=F32)
    o_ref[rows, :] = ((p - q) * zf_ref[rows, :].astype(F32)).astype(BF16)
    e_ref[rows, :] = (p + q).astype(BF16)

    @pl.when(s == n_first - 1)
    def _():
        for t in range(half // fk):
            base = half - (t + 1) * fk
            y = jnp.dot(flip_ref[...], e_ref[base:base + 2 * fk, :], preferred_element_type=F32)
            out = slice(half + t * fk, half + (t + 1) * fk)
            o_ref[out, :] = (y * zf_ref[out, :].astype(F32)).astype(BF16)


def _fourier(u, zf, csc, ch, sh, nyq, flip, *, batch, seq, gd, tk=FOURIER_TK):
    fw = u.shape[1]
    half = seq // 2
    n_first = half // tk
    fk = flip.shape[0]
    kern = functools.partial(_fourier_kernel, gd)
    per_batch = pl.BlockSpec((seq, fw), lambda b, k: (b, 0))
    half_rows = pl.BlockSpec((tk, seq), lambda b, k: (k, 0))
    whole = lambda a: pl.BlockSpec(a.shape, lambda b, k: (0, 0))
    return pl.pallas_call(
        kern,
        grid=(batch, n_first),
        in_specs=[per_batch, per_batch, whole(csc), half_rows, half_rows, whole(nyq), whole(flip)],
        out_specs=per_batch,
        out_shape=jax.ShapeDtypeStruct((batch * seq, fw), BF16),
        scratch_shapes=[pltpu.VMEM((seq, fw), BF16), pltpu.VMEM((seq, fw), BF16),
                        pltpu.VMEM((half + fk, fw), BF16)],
        compiler_params=pltpu.CompilerParams(
            dimension_semantics=("arbitrary", "arbitrary"), vmem_limit_bytes=VMEM_LIMIT),
        name="fourier",
    )(u, zf, csc, ch, sh, nyq, flip)


def _merge_kernel(yg_ref, og_ref, sgf_ref, sga_ref, x_ref, gate_ref, wf_ref, wa_ref, wo_ref, o_ref):
    yf = jnp.dot(yg_ref[...], wf_ref[...], preferred_element_type=F32)
    ya = jnp.dot(og_ref[...], wa_ref[...], preferred_element_type=F32)
    y = sgf_ref[...].astype(F32) * yf + sga_ref[...].astype(F32) * ya
    yo = jnp.dot(y.astype(BF16), wo_ref[...], preferred_element_type=F32)
    o_ref[...] = x_ref[...] + gate_ref[0] * yo


def _merge(yg, og, g, x2d, mod3, wf, wa, wo, *, seq, tm=MERGE_TM):
    m, d = x2d.shape
    tiles_per_seq = seq // tm
    const = lambda shape: pl.BlockSpec(shape, lambda i: (0, 0), pipeline_mode=pl.Buffered(1))
    return pl.pallas_call(
        _merge_kernel,
        grid=(m // tm,),
        in_specs=[
            pl.BlockSpec((tm, yg.shape[1]), lambda i: (i, 0)),
            pl.BlockSpec((tm, d), lambda i: (i, 0)),
            pl.BlockSpec((tm, d), lambda i: (i, 0)),
            pl.BlockSpec((tm, d), lambda i: (i, 1)),
            pl.BlockSpec((tm, d), lambda i: (i, 0)),
            pl.BlockSpec((1, 1, d), lambda i: (i // tiles_per_seq, 0, 2)),
            const(wf.shape), const(wa.shape), const(wo.shape),
        ],
        out_specs=pl.BlockSpec((tm, d), lambda i: (i, 0)),
        out_shape=jax.ShapeDtypeStruct((m, d), F32),
        compiler_params=pltpu.CompilerParams(
            dimension_semantics=("parallel",), vmem_limit_bytes=VMEM_LIMIT),
        name="merge",
    )(yg, og, g, g, x2d, mod3, wf, wa, wo)


def _rope_tables(seq, head_dim):
    n_freq = head_dim // 4
    t = np.arange(seq)
    pos = np.stack([t // GRID_W, t % GRID_W], axis=-1).astype(np.float32)
    inv_freq = (np.float32(ROPE_BASE) ** (-np.arange(n_freq, dtype=np.float32) / np.float32(n_freq)))
    ang = (pos[:, :, None] * inv_freq.astype(np.float32)).astype(np.float64)
    ang = np.broadcast_to(ang[:, None, :, :], (seq, 2, 2, n_freq))
    sign = np.array([-1.0, 1.0])[None, :, None, None]
    return (np.cos(ang).reshape(seq, head_dim).astype(np.float32),
            (np.sin(ang) * sign).reshape(seq, head_dim).astype(np.float32))


def _rope_lane_order(a, n_freq):
    lead = a.shape[:-1]
    return a.reshape(lead + (-1, 2, 2, n_freq)).swapaxes(-3, -2).reshape(a.shape)


def kernel(x, c, ctx, c_ctx, w_mod, b_mod, w_in, q_gain, k_gain, rpb, w_f_out, w_a_out, w_out):
    batch, seq, d = x.shape
    ctx_len = ctx.shape[1]
    depth, heads, n_dr, n_dc = rpb.shape
    assert depth == 1 and w_mod.shape[0] == 1
    head_dim = q_gain.shape[1]
    assert head_dim == LANES and seq % GRID_W == 0
    win_r, win_c = (n_dr + 1) // 2, (n_dc + 1) // 2
    attn_w = heads * head_dim
    fw = w_f_out.shape[1]
    gd = fw // F_GROUPS
    off_zf, off_q = fw, 2 * fw
    off_k, off_v, off_za = off_q + attn_w, off_q + 2 * attn_w, off_q + 3 * attn_w
    off_gf = off_za + attn_w
    off_ga = off_gf + d
    assert w_in.shape[2] == off_ga + d

    c_all = jnp.concatenate([c, c_ctx[None, :], jnp.zeros((16 - batch - 1, d), F32)], axis=0)
    mod = _mod(c_all, w_mod[0], b_mod)
    mod3 = mod.reshape(16, 1, 3 * d)

    x2d = x.reshape(batch * seq, d)
    c2d = ctx.reshape(batch * ctx_len, d)
    x_rows, all_rows = batch * seq, batch * (seq + ctx_len)
    u_f, h = _proj_x(x2d, c2d, mod3, w_in[0], batch=batch)

    n_freq = head_dim // 4
    tn = PROJ_TN
    assert batch * ctx_len == PROJ_TM and fw == tn
    cos_np, sin_np = _rope_tables(seq, head_dim)
    ident = np.ones((batch * ctx_len, head_dim), np.float32)
    cos_t = jnp.asarray(np.concatenate([cos_np, ident]))
    sin_t = jnp.asarray(np.concatenate([sin_np, 0.0 * ident]))
    qg = _rope_lane_order(q_gain, n_freq) * (float(head_dim) ** -0.5 * LOG2E)
    kg = _rope_lane_order(k_gain, n_freq)
    proj = functools.partial(_proj, h, w_in[0], cos_t=cos_t, sin_t=sin_t, x_rows=x_rows, seq=seq, tn=tn)
    tiles = lambda a, b: tuple(range(a // tn, b // tn))

    z = proj(gain=qg, kind="silu", col_tiles=tiles(off_zf, off_q) + tiles(off_za, off_gf), m=x_rows)
    q = proj(gain=qg, kind="qk", col_tiles=tiles(off_q, off_k), m=x_rows)
    k = proj(gain=kg, kind="qk", col_tiles=tiles(off_k, off_v), m=all_rows)
    v = proj(gain=qg, kind="raw", col_tiles=tiles(off_v, off_za), m=all_rows)
    g = proj(gain=qg, kind="sig", col_tiles=tiles(off_gf, off_ga + d), m=x_rows)

    rpb_pad = jnp.pad(rpb[0], ((0, 0), (0, 16 - n_dr), (0, LANES - n_dc)))
    og = _attention(q, k, v, z, rpb_pad, batch=batch, seq=seq, ctx_len=ctx_len, heads=heads, z_col0=fw,
                    win_r=win_r, win_c=win_c)

    cc, sc = _dft_mats(gd)
    cn, sn = _dft_mats(seq)
    const = lambda a: jnp.asarray(a.astype(np.float32)).astype(BF16)
    half, fk = seq // 2, FOURIER_FLIP
    nyq = np.zeros((16, seq))
    nyq[0] = cn[half]
    flip = np.zeros((fk, 2 * fk))
    flip[np.arange(fk), fk - np.arange(fk)] = 1.0
    yg = _fourier(u_f, z, const(np.concatenate([cc, sc], axis=1)), const(cn[:half]), const(sn[:half]),
                  const(nyq), const(flip), batch=batch, seq=seq, gd=gd)

    out = _merge(yg, og, g, x2d, mod3, w_f_out[0].astype(BF16), w_a_out[0].astype(BF16),
                 w_out[0].astype(BF16), seq=seq)
    return out.reshape(batch, seq, d)
```

```python
import functools

import numpy as np
import jax
import jax.numpy as jnp
from jax import lax
from jax.experimental import pallas as pl
from jax.experimental.pallas import tpu as pltpu

GRID_W = 64
F_GROUPS = 4
ROPE_BASE = 10000.0
EPS = 1e-6
NEG = -1e30
LOG2E = 1.4426950408889634
LANES = 128
VMEM_LIMIT = 56 * 1024 * 1024

MOD_TN = 512
PROJ_TN = 1024
PROJ_TM = 2048
PROJ_X_TM = 512
DOT_ROWS = 128
CAST_ROWS = 64
FOURIER_TK = 512
FOURIER_FLIP = 256
MERGE_TM = 256

BF16 = jnp.bfloat16
F32 = jnp.float32


def _nt_dot(a, b):
    return lax.dot_general(a, b, (((1,), (1,)), ((), ())), preferred_element_type=F32)


def _mod_kernel(c_ref, w_ref, b_ref, o_ref):
    a = jax.nn.silu(c_ref[...]).astype(BF16)
    o_ref[...] = jnp.dot(a, w_ref[...].astype(BF16), preferred_element_type=F32) + b_ref[...]


def _mod(c_all, w_mod, b_mod, tn=MOD_TN):
    m, d = c_all.shape
    n = w_mod.shape[1]
    return pl.pallas_call(
        _mod_kernel,
        grid=(n // tn,),
        in_specs=[pl.BlockSpec((m, d), lambda j: (0, 0)),
                  pl.BlockSpec((d, tn), lambda j: (0, j)),
                  pl.BlockSpec((1, tn), lambda j: (0, j))],
        out_specs=pl.BlockSpec((m, tn), lambda j: (0, j)),
        out_shape=jax.ShapeDtypeStruct((m, n), F32),
        name="mod",
    )(c_all, w_mod, b_mod)


def _rope_lane_order_cols(w):
    n = w.shape[1]
    quarter = (lax.broadcasted_iota(jnp.int32, w.shape, 1) % LANES) // (LANES // 4)
    up = pltpu.roll(w, n - LANES // 4, 1)
    down = pltpu.roll(w, LANES // 4, 1)
    return jnp.where(quarter == 1, up, jnp.where(quarter == 2, down, w))


def _proj_kernel(kind, mm, rc, h_ref, w_ref, gain_ref, cos_ref, sin_ref, o_ref, wb_ref):
    tm = h_ref.shape[0]
    tn = w_ref.shape[1]

    @pl.when(pl.program_id(1) == 0)
    def _():
        _cast_weight_tile(w_ref, wb_ref, rc, kind == "qk")

    for r in range(0, tm, mm):
        a = jnp.dot(h_ref[r:r + mm, :], wb_ref[...], preferred_element_type=F32)
        if kind == "raw":
            o_ref[r:r + mm, :] = a.astype(BF16)
        elif kind == "silu":
            o_ref[r:r + mm, :] = jax.nn.silu(a).astype(BF16)
        elif kind == "sig":
            o_ref[r:r + mm, :] = jax.nn.sigmoid(a).astype(BF16)
        else:
            gain = gain_ref[...]
            for hh in range(tn // LANES):
                xh = a[:, hh * LANES:(hh + 1) * LANES]
                ms = jnp.mean(xh * xh, axis=-1, keepdims=True)
                xn = xh * lax.rsqrt(ms + EPS) * gain
                xn = xn * cos_ref[r:r + mm, :] + pltpu.roll(xn, LANES // 2, 1) * sin_ref[r:r + mm, :]
                o_ref[r:r + mm, hh * LANES:(hh + 1) * LANES] = xn.astype(BF16)


def _cast_weight_tile(w_ref, wb_ref, rc, reorder):
    def body(t, carry):
        r = pl.multiple_of(t * rc, rc)
        wt = w_ref[pl.ds(r, rc), :]
        if reorder:
            wt = _rope_lane_order_cols(wt)
        wb_ref[pl.ds(r, rc), :] = wt.astype(BF16)
        return carry
    lax.fori_loop(0, w_ref.shape[0] // rc, body, 0)


def _proj_x_kernel(mm, rc, n_x, x_ref, c_ref, shift_ref, scale_ref, w_ref, o_ref, h_ref, wb_ref):
    @pl.when(pl.program_id(1) == 0)
    def _():
        _cast_weight_tile(w_ref, wb_ref, rc, False)

    def body(t_ref):
        for r in range(0, t_ref.shape[0], mm):
            xs = t_ref[r:r + mm, :]
            ms = jnp.mean(xs * xs, axis=-1, keepdims=True)
            h = (xs * lax.rsqrt(ms + EPS) * (1.0 + scale_ref[0]) + shift_ref[0]).astype(BF16)
            h_ref[r:r + mm, :] = h
            o_ref[r:r + mm, :] = jnp.dot(h, wb_ref[...], preferred_element_type=F32).astype(BF16)

    is_ctx = pl.program_id(1) >= n_x
    pl.when(jnp.logical_not(is_ctx))(functools.partial(body, x_ref))
    pl.when(is_ctx)(functools.partial(body, c_ref))


def _proj_x(x2d, c2d, mod3, w, *, batch, tm=PROJ_X_TM, tn=PROJ_TN, mm=DOT_ROWS, rc=CAST_ROWS):
    m, d = x2d.shape
    n_x, n_c = m // tm, c2d.shape[0] // tm
    tiles_per_mod = m // batch // tm
    mod_row = lambda i: jnp.minimum(i // tiles_per_mod, batch)
    return pl.pallas_call(
        functools.partial(_proj_x_kernel, mm, rc, n_x),
        grid=(1, n_x + n_c),
        in_specs=[
            pl.BlockSpec((tm, d), lambda j, i: (jnp.minimum(i, n_x - 1), 0)),
            pl.BlockSpec((tm, d), lambda j, i: (jnp.maximum(i - n_x, 0), 0)),
            pl.BlockSpec((1, 1, d), lambda j, i: (mod_row(i), 0, 0)),
            pl.BlockSpec((1, 1, d), lambda j, i: (mod_row(i), 0, 1)),
            pl.BlockSpec((d, tn), lambda j, i: (0, j)),
        ],
        out_specs=[pl.BlockSpec((tm, tn), lambda j, i: (i, j)),
                   pl.BlockSpec((tm, d), lambda j, i: (i, 0))],
        out_shape=[jax.ShapeDtypeStruct(((n_x + n_c) * tm, tn), BF16),
                   jax.ShapeDtypeStruct(((n_x + n_c) * tm, d), BF16)],
        scratch_shapes=[pltpu.VMEM((d, tn), BF16)],
        compiler_params=pltpu.CompilerParams(
            dimension_semantics=("arbitrary", "arbitrary"), vmem_limit_bytes=VMEM_LIMIT),
        name="proj_x",
    )(x2d, c2d, mod3, mod3, w)


def _lookup(j, values):
    out = values[-1]
    for idx in range(len(values) - 2, -1, -1):
        out = jnp.where(j == idx, values[idx], out)
    return out


def _proj(h, w, gain, cos_t, sin_t, *, kind, col_tiles, m, x_rows, seq, tm=PROJ_TM, tn=PROJ_TN, rc=CAST_ROWS):
    d = h.shape[1]
    mm = DOT_ROWS
    x_tiles, seq_tiles = x_rows // tm, seq // tm
    rope_tile = lambda i: jnp.where(i < x_tiles, i % seq_tiles, seq_tiles + i - x_tiles)
    kern = functools.partial(_proj_kernel, kind, mm, rc)
    n = len(col_tiles) * tn
    return pl.pallas_call(
        kern,
        grid=(len(col_tiles), m // tm),
        in_specs=[
            pl.BlockSpec((tm, d), lambda j, i: (i, 0)),
            pl.BlockSpec((d, tn), lambda j, i: (0, _lookup(j, col_tiles))),
            pl.BlockSpec((1, LANES), lambda j, i: (0, 0)),
            pl.BlockSpec((tm, LANES), lambda j, i: (rope_tile(i), 0)),
            pl.BlockSpec((tm, LANES), lambda j, i: (rope_tile(i), 0)),
        ],
        out_specs=pl.BlockSpec((tm, tn), lambda j, i: (i, j)),
        out_shape=jax.ShapeDtypeStruct((m, n), BF16),
        scratch_shapes=[pltpu.VMEM((d, tn), BF16)],
        compiler_params=pltpu.CompilerParams(
            dimension_semantics=("arbitrary", "arbitrary"), vmem_limit_bytes=VMEM_LIMIT),
        name="proj_" + kind,
    )(h, w, gain, cos_t, sin_t)


Q_ROWS = 2
AHEAD = 2
N_SLOTS = AHEAD + 1
HEADS_PER_STEP = 2


class _BandPlan:
    def __init__(self, rows, win_r):
        self.rows, self.win_r = rows, win_r
        band = win_r + Q_ROWS - 1
        self.band = band + band % 2
        self.groups = list(range(0, rows, Q_ROWS))
        self.start = {r0: min(max(r0 - win_r // 2, 0), rows - self.band) for r0 in self.groups}
        self.offsets = sorted({r0 - s0 for r0, s0 in self.start.items()})

    def table(self, r0):
        return self.offsets.index(r0 - self.start[r0])

    def valid_slots(self, r, s0):
        rs = min(max(r - self.win_r // 2, 0), self.rows - self.win_r)
        return tuple(rs <= s0 + i < rs + self.win_r for i in range(self.band))


def _build_bias_tables(rpb_ref, bias_ref, plan, win_c):
    win_r = plan.win_r
    c_io = lax.broadcasted_iota(jnp.int32, (GRID_W, LANES), 0)
    l_io = lax.broadcasted_iota(jnp.int32, (GRID_W, LANES), 1)
    cs = jnp.clip(c_io - win_c // 2, 0, GRID_W - win_c)
    inwin = (l_io >= cs) & (l_io < cs + win_c) & (l_io < GRID_W)
    low = l_io < GRID_W
    neg = jnp.full((GRID_W, LANES), NEG, F32)
    toep = []
    for dr in range(2 * win_r - 1):
        row = jnp.broadcast_to(rpb_ref[dr:dr + 1, :], (GRID_W, LANES))
        t = pltpu.roll(row, LANES - (win_c - 1), 1, stride=1, stride_axis=0)
        toep.append(jnp.where(inwin, t * LOG2E, NEG))
    for tb in range(len(plan.offsets)):
        same = [r0 for r0 in plan.groups if plan.table(r0) == tb]
        r0, s0 = same[0], plan.start[same[0]]
        for rho in range(Q_ROWS):
            r = r0 + rho
            valid = plan.valid_slots(r, s0)
            assert all(plan.valid_slots(o + rho, plan.start[o]) == valid for o in same)
            blocks = [toep[s0 + i - r + win_r - 1] if valid[i] else neg for i in range(plan.band)]
            for p in range(plan.band // 2):
                tile = jnp.where(low, blocks[2 * p], pltpu.roll(blocks[2 * p + 1], GRID_W, 1))
                bias_ref[tb, rho * GRID_W:(rho + 1) * GRID_W, p * LANES:(p + 1) * LANES] = tile


def _attn_kernel(plan, win_c,
                 q_ref, k_ref, v_ref, z_ref, kc_ref, vc_ref, rpb_ref, w1_ref, w2_ref, w3_ref,
                 o_ref, w1b_ref, w2b_ref, w3b_ref, bias_ref, s_ref):
    n_heads = q_ref.shape[1] // LANES

    for w_ref, wb_ref in ((w1_ref, w1b_ref), (w2_ref, w2b_ref), (w3_ref, w3b_ref)):
        wb_ref[...] = w_ref[...].astype(BF16)

    @pl.when(pl.program_id(1) == 0)
    def _():
        for hh in range(n_heads):
            _build_bias_tables(rpb_ref.at[hh], bias_ref.at[hh], plan, win_c)

    nq = Q_ROWS * GRID_W
    nk = plan.band * GRID_W
    dyn_zero = jnp.minimum(pl.program_id(0), 0)

    def scores(i, hh, r0):
        q0, k0 = r0 * GRID_W, plan.start[r0] * GRID_W
        hs = slice(hh * LANES, (hh + 1) * LANES)
        qb = q_ref[q0:q0 + nq, hs]
        s_loc = _nt_dot(qb, k_ref[k0:k0 + nk, hs]) + bias_ref[hh, plan.table(r0)]
        s_ctx = _nt_dot(qb, kc_ref[:, hs])
        slot = i % N_SLOTS + dyn_zero
        s_ref[slot, :, :nk] = s_loc
        s_ref[slot, :, nk:] = s_ctx
        return jnp.maximum(jnp.max(s_loc, axis=-1, keepdims=True), jnp.max(s_ctx, axis=-1, keepdims=True))

    def finish(i, hh, r0, m):
        q0, k0 = r0 * GRID_W, plan.start[r0] * GRID_W
        hs = slice(hh * LANES, (hh + 1) * LANES)
        p = jnp.exp2(s_ref[i % N_SLOTS + dyn_zero] - m)
        den = jnp.sum(p, axis=-1, keepdims=True)
        pb = p.astype(BF16)
        o = (jnp.dot(pb[:, :nk], v_ref[k0:k0 + nk, hs], preferred_element_type=F32)
             + jnp.dot(pb[:, nk:], vc_ref[:, hs], preferred_element_type=F32))
        og = (o / den) * z_ref[q0:q0 + nq, hs].astype(F32)
        o_ref[q0:q0 + nq, hs] = og.astype(BF16)

    items = [(hh, r0) for hh in range(n_heads) for r0 in plan.groups]
    pending = [scores(i, *item) for i, item in enumerate(items[:AHEAD])]
    for i, item in enumerate(items):
        if i + AHEAD < len(items):
            pending.append(scores(i + AHEAD, *items[i + AHEAD]))
        finish(i, *item, pending.pop(0))


def _attention(q, k, v, z, rpb_pad, weights, *, batch, seq, ctx_len, heads, z_col0, win_r, win_c):
    plan = _BandPlan(seq // GRID_W, win_r)
    kern = functools.partial(_attn_kernel, plan, win_c)
    nq, nk = Q_ROWS * GRID_W, plan.band * GRID_W
    hps, width = HEADS_PER_STEP, HEADS_PER_STEP * LANES
    assert heads % hps == 0 and z_col0 % width == 0
    steps = heads // hps * batch
    tok = pl.BlockSpec((seq, width), lambda h, b: (b, h))
    ctx0 = batch * seq // ctx_len
    ctx = pl.BlockSpec((ctx_len, width), lambda h, b: (ctx0 + b, h))
    w_rows = [pl.BlockSpec((w.shape[0] // steps, w.shape[1]), lambda h, b: (h * batch + b, 0)) for w in weights]
    return pl.pallas_call(
        kern,
        grid=(heads // hps, batch),
        in_specs=[tok, tok, tok, pl.BlockSpec((seq, width), lambda h, b: (b, z_col0 // width + h)), ctx, ctx,
                  pl.BlockSpec((hps,) + rpb_pad.shape[1:], lambda h, b: (h, 0, 0))] + w_rows,
        out_specs=[tok] + w_rows,
        out_shape=[jax.ShapeDtypeStruct((batch * seq, heads * LANES), BF16)]
        + [jax.ShapeDtypeStruct(w.shape, BF16) for w in weights],
        scratch_shapes=[pltpu.VMEM((hps, len(plan.offsets), nq, nk), F32),
                        pltpu.VMEM((N_SLOTS, nq, nk + ctx_len), F32)],
        compiler_params=pltpu.CompilerParams(
            dimension_semantics=("arbitrary", "arbitrary"), vmem_limit_bytes=VMEM_LIMIT),
        name="attn",
    )(q, k, v, z, k, v, rpb_pad, *weights)


def _dft_mats(n):
    jk = (np.arange(n)[:, None] * np.arange(n)[None, :]) % n
    ang = 2.0 * np.pi * jk.astype(np.float64) / n
    return np.cos(ang) / np.sqrt(n), np.sin(ang) / np.sqrt(n)


def _fourier_kernel(gd, u_ref, zf_ref, csc_ref, ch_ref, sh_ref, nyq_ref, flip_ref, o_ref, a_ref, b_ref, e_ref):
    seq = u_ref.shape[0]
    half = seq // 2
    tk = ch_ref.shape[0]
    fk = flip_ref.shape[0]
    n_first = half // tk
    s = pl.program_id(1)

    @pl.when(s == 0)
    def _():
        rb = 4 * DOT_ROWS
        for g in range(u_ref.shape[1] // gd):
            for r in range(0, seq, rb):
                t = jnp.dot(u_ref[r:r + rb, g * gd:(g + 1) * gd], csc_ref[...], preferred_element_type=F32)
                a_ref[r:r + rb, g * gd:(g + 1) * gd] = t[:, :gd].astype(BF16)
                b_ref[r:r + rb, g * gd:(g + 1) * gd] = t[:, gd:].astype(BF16)
        e_ref[half:, :] = jnp.zeros((fk, e_ref.shape[1]), BF16)
        e_ref[half:half + nyq_ref.shape[0], :] = jnp.dot(
            nyq_ref[...], a_ref[...], preferred_element_type=F32).astype(BF16)

    rows = pl.ds(pl.multiple_of(s * tk, tk), tk)
    p = jnp.dot(ch_ref[...], a_ref[...], preferred_element_type=F32)
    q = jnp.dot(sh_ref[...], b_ref[...], preferred_element_type=F32)
    o_ref[rows, :] = ((p - q) * zf_ref[rows, :].astype(F32)).astype(BF16)
    e_ref[rows, :] = (p + q).astype(BF16)

    @pl.when(s == n_first - 1)
    def _():
        for t in range(half // fk):
            base = half - (t + 1) * fk
            y = jnp.dot(flip_ref[...], e_ref[base:base + 2 * fk, :], preferred_element_type=F32)
            out = slice(half + t * fk, half + (t + 1) * fk)
            o_ref[out, :] = (y * zf_ref[out, :].astype(F32)).astype(BF16)


def _fourier(u, zf, csc, ch, sh, nyq, flip, *, batch, seq, gd, tk=FOURIER_TK):
    fw = u.shape[1]
    half = seq // 2
    n_first = half // tk
    fk = flip.shape[0]
    kern = functools.partial(_fourier_kernel, gd)
    per_batch = pl.BlockSpec((seq, fw), lambda b, k: (b, 0))
    half_rows = pl.BlockSpec((tk, seq), lambda b, k: (k, 0))
    whole = lambda a: pl.BlockSpec(a.shape, lambda b, k: (0, 0))
    return pl.pallas_call(
        kern,
        grid=(batch, n_first),
        in_specs=[per_batch, per_batch, whole(csc), half_rows, half_rows, whole(nyq), whole(flip)],
        out_specs=per_batch,
        out_shape=jax.ShapeDtypeStruct((batch * seq, fw), BF16),
        scratch_shapes=[pltpu.VMEM((seq, fw), BF16), pltpu.VMEM((seq, fw), BF16),
                        pltpu.VMEM((half + fk, fw), BF16)],
        compiler_params=pltpu.CompilerParams(
            dimension_semantics=("arbitrary", "arbitrary"), vmem_limit_bytes=VMEM_LIMIT),
        name="fourier",
    )(u, zf, csc, ch, sh, nyq, flip)


def _merge_kernel(yg_ref, og_ref, sgf_ref, sga_ref, x_ref, gate_ref, wf_ref, wa_ref, wo_ref, o_ref):
    yf = jnp.dot(yg_ref[...], wf_ref[...], preferred_element_type=F32)
    ya = jnp.dot(og_ref[...], wa_ref[...], preferred_element_type=F32)
    y = sgf_ref[...].astype(F32) * yf + sga_ref[...].astype(F32) * ya
    yo = jnp.dot(y.astype(BF16), wo_ref[...], preferred_element_type=F32)
    o_ref[...] = x_ref[...] + gate_ref[0] * yo


def _merge(yg, og, g, x2d, mod3, wf, wa, wo, *, seq, tm=MERGE_TM):
    m, d = x2d.shape
    tiles_per_seq = seq // tm
    const = lambda shape: pl.BlockSpec(shape, lambda i: (0, 0), pipeline_mode=pl.Buffered(1))
    return pl.pallas_call(
        _merge_kernel,
        grid=(m // tm,),
        in_specs=[
            pl.BlockSpec((tm, yg.shape[1]), lambda i: (i, 0)),
            pl.BlockSpec((tm, d), lambda i: (i, 0)),
            pl.BlockSpec((tm, d), lambda i: (i, 0)),
            pl.BlockSpec((tm, d), lambda i: (i, 1)),
            pl.BlockSpec((tm, d), lambda i: (i, 0)),
            pl.BlockSpec((1, 1, d), lambda i: (i // tiles_per_seq, 0, 2)),
            const(wf.shape), const(wa.shape), const(wo.shape),
        ],
        out_specs=pl.BlockSpec((tm, d), lambda i: (i, 0)),
        out_shape=jax.ShapeDtypeStruct((m, d), F32),
        compiler_params=pltpu.CompilerParams(
            dimension_semantics=("parallel",), vmem_limit_bytes=VMEM_LIMIT),
        name="merge",
    )(yg, og, g, g, x2d, mod3, wf, wa, wo)


def _rope_tables(seq, head_dim):
    n_freq = head_dim // 4
    t = np.arange(seq)
    pos = np.stack([t // GRID_W, t % GRID_W], axis=-1).astype(np.float32)
    inv_freq = (np.float32(ROPE_BASE) ** (-np.arange(n_freq, dtype=np.float32) / np.float32(n_freq)))
    ang = (pos[:, :, None] * inv_freq.astype(np.float32)).astype(np.float64)
    ang = np.broadcast_to(ang[:, None, :, :], (seq, 2, 2, n_freq))
    sign = np.array([-1.0, 1.0])[None, :, None, None]
    return (np.cos(ang).reshape(seq, head_dim).astype(np.float32),
            (np.sin(ang) * sign).reshape(seq, head_dim).astype(np.float32))


def _rope_lane_order(a, n_freq):
    lead = a.shape[:-1]
    return a.reshape(lead + (-1, 2, 2, n_freq)).swapaxes(-3, -2).reshape(a.shape)


def kernel(x, c, ctx, c_ctx, w_mod, b_mod, w_in, q_gain, k_gain, rpb, w_f_out, w_a_out, w_out):
    batch, seq, d = x.shape
    ctx_len = ctx.shape[1]
    depth, heads, n_dr, n_dc = rpb.shape
    assert depth == 1 and w_mod.shape[0] == 1
    head_dim = q_gain.shape[1]
    assert head_dim == LANES and seq % GRID_W == 0
    win_r, win_c = (n_dr + 1) // 2, (n_dc + 1) // 2
    attn_w = heads * head_dim
    fw = w_f_out.shape[1]
    gd = fw // F_GROUPS
    off_zf, off_q = fw, 2 * fw
    off_k, off_v, off_za = off_q + attn_w, off_q + 2 * attn_w, off_q + 3 * attn_w
    off_gf = off_za + attn_w
    off_ga = off_gf + d
    assert w_in.shape[2] == off_ga + d

    c_all = jnp.concatenate([c, c_ctx[None, :], jnp.zeros((16 - batch - 1, d), F32)], axis=0)
    mod = _mod(c_all, w_mod[0], b_mod)
    mod3 = mod.reshape(16, 1, 3 * d)

    x2d = x.reshape(batch * seq, d)
    c2d = ctx.reshape(batch * ctx_len, d)
    x_rows, all_rows = batch * seq, batch * (seq + ctx_len)
    u_f, h = _proj_x(x2d, c2d, mod3, w_in[0], batch=batch)

    n_freq = head_dim // 4
    tn = PROJ_TN
    assert batch * ctx_len == PROJ_TM and fw == tn
    cos_np, sin_np = _rope_tables(seq, head_dim)
    ident = np.ones((batch * ctx_len, head_dim), np.float32)
    cos_t = jnp.asarray(np.concatenate([cos_np, ident]))
    sin_t = jnp.asarray(np.concatenate([sin_np, 0.0 * ident]))
    qg = _rope_lane_order(q_gain, n_freq) * (float(head_dim) ** -0.5 * LOG2E)
    kg = _rope_lane_order(k_gain, n_freq)
    proj = functools.partial(_proj, h, w_in[0], cos_t=cos_t, sin_t=sin_t, x_rows=x_rows, seq=seq, tn=tn)
    tiles = lambda a, b: tuple(range(a // tn, b // tn))

    z = proj(gain=qg, kind="silu", col_tiles=tiles(off_zf, off_q) + tiles(off_za, off_gf), m=x_rows)
    q = proj(gain=qg, kind="qk", col_tiles=tiles(off_q, off_k), m=x_rows)
    k = proj(gain=kg, kind="qk", col_tiles=tiles(off_k, off_v), m=all_rows)
    v = proj(gain=qg, kind="raw", col_tiles=tiles(off_v, off_za), m=all_rows)
    g = proj(gain=qg, kind="sig", col_tiles=tiles(off_gf, off_ga + d), m=x_rows)

    rpb_pad = jnp.pad(rpb[0], ((0, 0), (0, 16 - n_dr), (0, LANES - n_dc)))
    og, wf, wa, wo = _attention(q, k, v, z, rpb_pad, (w_f_out[0], w_a_out[0], w_out[0]), batch=batch, seq=seq,
                                ctx_len=ctx_len, heads=heads, z_col0=fw, win_r=win_r, win_c=win_c)

    cc, sc = _dft_mats(gd)
    cn, sn = _dft_mats(seq)
    const = lambda a: jnp.asarray(a.astype(np.float32)).astype(BF16)
    half, fk = seq // 2, FOURIER_FLIP
    nyq = np.zeros((16, seq))
    nyq[0] = cn[half]
    flip = np.zeros((fk, 2 * fk))
    flip[np.arange(fk), fk - np.arange(fk)] = 1.0
    yg = _fourier(u_f, z, const(np.concatenate([cc, sc], axis=1)), const(cn[:half]), const(sn[:half]),
                  const(nyq), const(flip), batch=batch, seq=seq, gd=gd)

    out = _merge(yg, og, g, x2d, mod3, wf, wa, wo, seq=seq)
    return out.reshape(batch, seq, d)
```

```python
import functools

import numpy as np
import jax
import jax.numpy as jnp
from jax import lax
from jax.experimental import pallas as pl
from jax.experimental.pallas import tpu as pltpu

GRID_W = 64
F_GROUPS = 4
ROPE_BASE = 10000.0
EPS = 1e-6
NEG = -1e30
LOG2E = 1.4426950408889634
LANES = 128
VMEM_LIMIT = 56 * 1024 * 1024

MOD_TN = 512
PROJ_TN = 1024
PROJ_TM = 2048
PROJ_X_TM = 512
DOT_ROWS = 128
CAST_ROWS = 64
FOURIER_TK = 512
FOURIER_FLIP = 256
MERGE_TM = 256

BF16 = jnp.bfloat16
F32 = jnp.float32


def _nt_dot(a, b):
    return lax.dot_general(a, b, (((1,), (1,)), ((), ())), preferred_element_type=F32)


def _mod_kernel(c_ref, w_ref, b_ref, o_ref):
    a = jax.nn.silu(c_ref[...]).astype(BF16)
    o_ref[...] = jnp.dot(a, w_ref[...].astype(BF16), preferred_element_type=F32) + b_ref[...]


def _mod(c_all, w_mod, b_mod, tn=MOD_TN):
    m, d = c_all.shape
    n = w_mod.shape[1]
    return pl.pallas_call(
        _mod_kernel,
        grid=(n // tn,),
        in_specs=[pl.BlockSpec((m, d), lambda j: (0, 0)),
                  pl.BlockSpec((d, tn), lambda j: (0, j)),
                  pl.BlockSpec((1, tn), lambda j: (0, j))],
        out_specs=pl.BlockSpec((m, tn), lambda j: (0, j)),
        out_shape=jax.ShapeDtypeStruct((m, n), F32),
        name="mod",
    )(c_all, w_mod, b_mod)


def _rope_lane_order_cols(w):
    n = w.shape[1]
    quarter = (lax.broadcasted_iota(jnp.int32, w.shape, 1) % LANES) // (LANES // 4)
    up = pltpu.roll(w, n - LANES // 4, 1)
    down = pltpu.roll(w, LANES // 4, 1)
    return jnp.where(quarter == 1, up, jnp.where(quarter == 2, down, w))


def _proj_kernel(kind, mm, rc, h_ref, w_ref, gain_ref, cos_ref, sin_ref, o_ref, wb_ref):
    tm = h_ref.shape[0]
    tn = w_ref.shape[1]

    @pl.when(pl.program_id(1) == 0)
    def _():
        _cast_weight_tile(w_ref, wb_ref, rc, kind == "qk")

    for r in range(0, tm, mm):
        a = jnp.dot(h_ref[r:r + mm, :], wb_ref[...], preferred_element_type=F32)
        if kind == "raw":
            o_ref[r:r + mm, :] = a.astype(BF16)
        elif kind == "silu":
            o_ref[r:r + mm, :] = jax.nn.silu(a).astype(BF16)
        elif kind == "sig":
            o_ref[r:r + mm, :] = jax.nn.sigmoid(a).astype(BF16)
        else:
            gain = gain_ref[...]
            for hh in range(tn // LANES):
                xh = a[:, hh * LANES:(hh + 1) * LANES]
                ms = jnp.mean(xh * xh, axis=-1, keepdims=True)
                xn = xh * lax.rsqrt(ms + EPS) * gain
                xn = xn * cos_ref[r:r + mm, :] + pltpu.roll(xn, LANES // 2, 1) * sin_ref[r:r + mm, :]
                o_ref[r:r + mm, hh * LANES:(hh + 1) * LANES] = xn.astype(BF16)


def _cast_weight_tile(w_ref, wb_ref, rc, reorder):
    def body(t, carry):
        r = pl.multiple_of(t * rc, rc)
        wt = w_ref[pl.ds(r, rc), :]
        if reorder:
            wt = _rope_lane_order_cols(wt)
        wb_ref[pl.ds(r, rc), :] = wt.astype(BF16)
        return carry
    lax.fori_loop(0, w_ref.shape[0] // rc, body, 0)


def _proj_x_kernel(mm, rc, n_x, x_ref, c_ref, shift_ref, scale_ref, w_ref, o_ref, h_ref, wb_ref):
    @pl.when(pl.program_id(1) == 0)
    def _():
        _cast_weight_tile(w_ref, wb_ref, rc, False)

    def body(t_ref):
        for r in range(0, t_ref.shape[0], mm):
            xs = t_ref[r:r + mm, :]
            ms = jnp.mean(xs * xs, axis=-1, keepdims=True)
            h = (xs * lax.rsqrt(ms + EPS) * (1.0 + scale_ref[0]) + shift_ref[0]).astype(BF16)
            h_ref[r:r + mm, :] = h
            o_ref[r:r + mm, :] = jnp.dot(h, wb_ref[...], preferred_element_type=F32).astype(BF16)

    is_ctx = pl.program_id(1) >= n_x
    pl.when(jnp.logical_not(is_ctx))(functools.partial(body, x_ref))
    pl.when(is_ctx)(functools.partial(body, c_ref))


def _proj_x(x2d, c2d, mod3, w, *, batch, tm=PROJ_X_TM, tn=PROJ_TN, mm=DOT_ROWS, rc=CAST_ROWS):
    m, d = x2d.shape
    n_x, n_c = m // tm, c2d.shape[0] // tm
    tiles_per_mod = m // batch // tm
    mod_row = lambda i: jnp.minimum(i // tiles_per_mod, batch)
    return pl.pallas_call(
        functools.partial(_proj_x_kernel, mm, rc, n_x),
        grid=(1, n_x + n_c),
        in_specs=[
            pl.BlockSpec((tm, d), lambda j, i: (jnp.minimum(i, n_x - 1), 0)),
            pl.BlockSpec((tm, d), lambda j, i: (jnp.maximum(i - n_x, 0), 0)),
            pl.BlockSpec((1, 1, d), lambda j, i: (mod_row(i), 0, 0)),
            pl.BlockSpec((1, 1, d), lambda j, i: (mod_row(i), 0, 1)),
            pl.BlockSpec((d, tn), lambda j, i: (0, j)),
        ],
        out_specs=[pl.BlockSpec((tm, tn), lambda j, i: (i, j)),
                   pl.BlockSpec((tm, d), lambda j, i: (i, 0))],
        out_shape=[jax.ShapeDtypeStruct(((n_x + n_c) * tm, tn), BF16),
                   jax.ShapeDtypeStruct(((n_x + n_c) * tm, d), BF16)],
        scratch_shapes=[pltpu.VMEM((d, tn), BF16)],
        compiler_params=pltpu.CompilerParams(
            dimension_semantics=("arbitrary", "arbitrary"), vmem_limit_bytes=VMEM_LIMIT),
        name="proj_x",
    )(x2d, c2d, mod3, mod3, w)


def _lookup(j, values):
    out = values[-1]
    for idx in range(len(values) - 2, -1, -1):
        out = jnp.where(j == idx, values[idx], out)
    return out


def _proj(h, w, gain, cos_t, sin_t, *, kind, col_tiles, m, x_rows, seq, tm=PROJ_TM, tn=PROJ_TN, rc=CAST_ROWS):
    d = h.shape[1]
    mm = DOT_ROWS
    x_tiles, seq_tiles = x_rows // tm, seq // tm
    rope_tile = lambda i: jnp.where(i < x_tiles, i % seq_tiles, seq_tiles + i - x_tiles)
    kern = functools.partial(_proj_kernel, kind, mm, rc)
    n = len(col_tiles) * tn
    return pl.pallas_call(
        kern,
        grid=(len(col_tiles), m // tm),
        in_specs=[
            pl.BlockSpec((tm, d), lambda j, i: (i, 0)),
            pl.BlockSpec((d, tn), lambda j, i: (0, _lookup(j, col_tiles))),
            pl.BlockSpec((1, LANES), lambda j, i: (0, 0)),
            pl.BlockSpec((tm, LANES), lambda j, i: (rope_tile(i), 0)),
            pl.BlockSpec((tm, LANES), lambda j, i: (rope_tile(i), 0)),
        ],
        out_specs=pl.BlockSpec((tm, tn), lambda j, i: (i, j)),
        out_shape=jax.ShapeDtypeStruct((m, n), BF16),
        scratch_shapes=[pltpu.VMEM((d, tn), BF16)],
        compiler_params=pltpu.CompilerParams(
            dimension_semantics=("arbitrary", "arbitrary"), vmem_limit_bytes=VMEM_LIMIT),
        name="proj_" + kind,
    )(h, w, gain, cos_t, sin_t)


Q_ROWS = 2
AHEAD = 3
N_SLOTS = AHEAD + 1
HEADS_PER_STEP = 2


class _BandPlan:
    def __init__(self, rows, win_r):
        self.rows, self.win_r = rows, win_r
        band = win_r + Q_ROWS - 1
        self.band = band + band % 2
        self.groups = list(range(0, rows, Q_ROWS))
        self.start = {r0: min(max(r0 - win_r // 2, 0), rows - self.band) for r0 in self.groups}
        self.offsets = sorted({r0 - s0 for r0, s0 in self.start.items()})

    def table(self, r0):
        return self.offsets.index(r0 - self.start[r0])

    def valid_slots(self, r, s0):
        rs = min(max(r - self.win_r // 2, 0), self.rows - self.win_r)
        return tuple(rs <= s0 + i < rs + self.win_r for i in range(self.band))


def _build_bias_tables(rpb_ref, bias_ref, plan, win_c):
    win_r = plan.win_r
    c_io = lax.broadcasted_iota(jnp.int32, (GRID_W, LANES), 0)
    l_io = lax.broadcasted_iota(jnp.int32, (GRID_W, LANES), 1)
    cs = jnp.clip(c_io - win_c // 2, 0, GRID_W - win_c)
    inwin = (l_io >= cs) & (l_io < cs + win_c) & (l_io < GRID_W)
    low = l_io < GRID_W
    neg = jnp.full((GRID_W, LANES), NEG, F32)
    toep = []
    for dr in range(2 * win_r - 1):
        row = jnp.broadcast_to(rpb_ref[dr:dr + 1, :], (GRID_W, LANES))
        t = pltpu.roll(row, LANES - (win_c - 1), 1, stride=1, stride_axis=0)
        toep.append(jnp.where(inwin, t * LOG2E, NEG))
    for tb in range(len(plan.offsets)):
        same = [r0 for r0 in plan.groups if plan.table(r0) == tb]
        r0, s0 = same[0], plan.start[same[0]]
        for rho in range(Q_ROWS):
            r = r0 + rho
            valid = plan.valid_slots(r, s0)
            assert all(plan.valid_slots(o + rho, plan.start[o]) == valid for o in same)
            blocks = [toep[s0 + i - r + win_r - 1] if valid[i] else neg for i in range(plan.band)]
            for p in range(plan.band // 2):
                tile = jnp.where(low, blocks[2 * p], pltpu.roll(blocks[2 * p + 1], GRID_W, 1))
                bias_ref[tb, rho * GRID_W:(rho + 1) * GRID_W, p * LANES:(p + 1) * LANES] = tile


def _attn_kernel(plan, win_c,
                 q_ref, k_ref, v_ref, z_ref, kc_ref, vc_ref, rpb_ref, w1_ref, w2_ref, w3_ref,
                 o_ref, w1b_ref, w2b_ref, w3b_ref, bias_ref, s_ref):
    n_heads = q_ref.shape[1] // LANES

    for w_ref, wb_ref in ((w1_ref, w1b_ref), (w2_ref, w2b_ref), (w3_ref, w3b_ref)):
        wb_ref[...] = w_ref[...].astype(BF16)

    @pl.when(pl.program_id(1) == 0)
    def _():
        for hh in range(n_heads):
            _build_bias_tables(rpb_ref.at[hh], bias_ref.at[hh], plan, win_c)

    nq = Q_ROWS * GRID_W
    nk = plan.band * GRID_W
    dyn_zero = jnp.minimum(pl.program_id(0), 0)

    def scores(i, hh, r0):
        q0, k0 = r0 * GRID_W, plan.start[r0] * GRID_W
        hs = slice(hh * LANES, (hh + 1) * LANES)
        qb = q_ref[q0:q0 + nq, hs]
        s_loc = _nt_dot(qb, k_ref[k0:k0 + nk, hs]) + bias_ref[hh, plan.table(r0)]
        s_ctx = _nt_dot(qb, kc_ref[:, hs])
        slot = i % N_SLOTS + dyn_zero
        s_ref[slot, :, :nk] = s_loc
        s_ref[slot, :, nk:] = s_ctx
        return jnp.maximum(jnp.max(s_loc, axis=-1, keepdims=True), jnp.max(s_ctx, axis=-1, keepdims=True))

    def finish(i, hh, r0, m):
        q0, k0 = r0 * GRID_W, plan.start[r0] * GRID_W
        hs = slice(hh * LANES, (hh + 1) * LANES)
        p = jnp.exp2(s_ref[i % N_SLOTS + dyn_zero] - m)
        den = jnp.sum(p, axis=-1, keepdims=True)
        pb = p.astype(BF16)
        o = (jnp.dot(pb[:, :nk], v_ref[k0:k0 + nk, hs], preferred_element_type=F32)
             + jnp.dot(pb[:, nk:], vc_ref[:, hs], preferred_element_type=F32))
        og = (o / den) * z_ref[q0:q0 + nq, hs].astype(F32)
        o_ref[q0:q0 + nq, hs] = og.astype(BF16)

    items = [(hh, r0) for hh in range(n_heads) for r0 in plan.groups]
    pending = [scores(i, *item) for i, item in enumerate(items[:AHEAD])]
    for i, item in enumerate(items):
        if i + AHEAD < len(items):
            pending.append(scores(i + AHEAD, *items[i + AHEAD]))
        finish(i, *item, pending.pop(0))


def _attention(q, k, v, z, rpb_pad, weights, *, batch, seq, ctx_len, heads, z_col0, win_r, win_c):
    plan = _BandPlan(seq // GRID_W, win_r)
    kern = functools.partial(_attn_kernel, plan, win_c)
    nq, nk = Q_ROWS * GRID_W, plan.band * GRID_W
    hps, width = HEADS_PER_STEP, HEADS_PER_STEP * LANES
    assert heads % hps == 0 and z_col0 % width == 0
    steps = heads // hps * batch
    tok = pl.BlockSpec((seq, width), lambda h, b: (b, h))
    ctx0 = batch * seq // ctx_len
    ctx = pl.BlockSpec((ctx_len, width), lambda h, b: (ctx0 + b, h))
    w_rows = [pl.BlockSpec((w.shape[0] // steps, w.shape[1]), lambda h, b: (h * batch + b, 0)) for w in weights]
    return pl.pallas_call(
        kern,
        grid=(heads // hps, batch),
        in_specs=[tok, tok, tok, pl.BlockSpec((seq, width), lambda h, b: (b, z_col0 // width + h)), ctx, ctx,
                  pl.BlockSpec((hps,) + rpb_pad.shape[1:], lambda h, b: (h, 0, 0))] + w_rows,
        out_specs=[tok] + w_rows,
        out_shape=[jax.ShapeDtypeStruct((batch * seq, heads * LANES), BF16)]
        + [jax.ShapeDtypeStruct(w.shape, BF16) for w in weights],
        scratch_shapes=[pltpu.VMEM((hps, len(plan.offsets), nq, nk), F32),
                        pltpu.VMEM((N_SLOTS, nq, nk + ctx_len), F32)],
        compiler_params=pltpu.CompilerParams(
            dimension_semantics=("arbitrary", "arbitrary"), vmem_limit_bytes=VMEM_LIMIT),
        name="attn",
    )(q, k, v, z, k, v, rpb_pad, *weights)


def _dft_mats(n):
    jk = (np.arange(n)[:, None] * np.arange(n)[None, :]) % n
    ang = 2.0 * np.pi * jk.astype(np.float64) / n
    return np.cos(ang) / np.sqrt(n), np.sin(ang) / np.sqrt(n)


def _fourier_kernel(gd, u_ref, zf_ref, csc_ref, ch_ref, sh_ref, nyq_ref, flip_ref, o_ref, a_ref, b_ref, e_ref):
    seq = u_ref.shape[0]
    half = seq // 2
    tk = ch_ref.shape[0]
    fk = flip_ref.shape[0]
    n_first = half // tk
    s = pl.program_id(1)

    @pl.when(s == 0)
    def _():
        rb = 4 * DOT_ROWS
        for g in range(u_ref.shape[1] // gd):
            for r in range(0, seq, rb):
                t = jnp.dot(u_ref[r:r + rb, g * gd:(g + 1) * gd], csc_ref[...], preferred_element_type=F32)
                a_ref[r:r + rb, g * gd:(g + 1) * gd] = t[:, :gd].astype(BF16)
                b_ref[r:r + rb, g * gd:(g + 1) * gd] = t[:, gd:].astype(BF16)
        e_ref[half:, :] = jnp.zeros((fk, e_ref.shape[1]), BF16)
        e_ref[half:half + nyq_ref.shape[0], :] = jnp.dot(
            nyq_ref[...], a_ref[...], preferred_element_type=F32).astype(BF16)

    rows = pl.ds(pl.multiple_of(s * tk, tk), tk)
    p = jnp.dot(ch_ref[...], a_ref[...], preferred_element_type=F32)
    q = jnp.dot(sh_ref[...], b_ref[...], preferred_element_type=F32)
    o_ref[rows, :] = ((p - q) * zf_ref[rows, :].astype(F32)).astype(BF16)
    e_ref[rows, :] = (p + q).astype(BF16)

    @pl.when(s == n_first - 1)
    def _():
        for t in range(half // fk):
            base = half - (t + 1) * fk
            y = jnp.dot(flip_ref[...], e_ref[base:base + 2 * fk, :], preferred_element_type=F32)
            out = slice(half + t * fk, half + (t + 1) * fk)
            o_ref[out, :] = (y * zf_ref[out, :].astype(F32)).astype(BF16)


def _fourier(u, zf, csc, ch, sh, nyq, flip, *, batch, seq, gd, tk=FOURIER_TK):
    fw = u.shape[1]
    half = seq // 2
    n_first = half // tk
    fk = flip.shape[0]
    kern = functools.partial(_fourier_kernel, gd)
    per_batch = pl.BlockSpec((seq, fw), lambda b, k: (b, 0))
    half_rows = pl.BlockSpec((tk, seq), lambda b, k: (k, 0))
    whole = lambda a: pl.BlockSpec(a.shape, lambda b, k: (0, 0))
    return pl.pallas_call(
        kern,
        grid=(batch, n_first),
        in_specs=[per_batch, per_batch, whole(csc), half_rows, half_rows, whole(nyq), whole(flip)],
        out_specs=per_batch,
        out_shape=jax.ShapeDtypeStruct((batch * seq, fw), BF16),
        scratch_shapes=[pltpu.VMEM((seq, fw), BF16), pltpu.VMEM((seq, fw), BF16),
                        pltpu.VMEM((half + fk, fw), BF16)],
        compiler_params=pltpu.CompilerParams(
            dimension_semantics=("arbitrary", "arbitrary"), vmem_limit_bytes=VMEM_LIMIT),
        name="fourier",
    )(u, zf, csc, ch, sh, nyq, flip)


def _merge_kernel(yg_ref, og_ref, sgf_ref, sga_ref, x_ref, gate_ref, wf_ref, wa_ref, wo_ref, o_ref):
    yf = jnp.dot(yg_ref[...], wf_ref[...], preferred_element_type=F32)
    ya = jnp.dot(og_ref[...], wa_ref[...], preferred_element_type=F32)
    y = sgf_ref[...].astype(F32) * yf + sga_ref[...].astype(F32) * ya
    yo = jnp.dot(y.astype(BF16), wo_ref[...], preferred_element_type=F32)
    o_ref[...] = x_ref[...] + gate_ref[0] * yo


def _merge(yg, og, g, x2d, mod3, wf, wa, wo, *, seq, tm=MERGE_TM):
    m, d = x2d.shape
    tiles_per_seq = seq // tm
    const = lambda shape: pl.BlockSpec(shape, lambda i: (0, 0), pipeline_mode=pl.Buffered(1))
    return pl.pallas_call(
        _merge_kernel,
        grid=(m // tm,),
        in_specs=[
            pl.BlockSpec((tm, yg.shape[1]), lambda i: (i, 0)),
            pl.BlockSpec((tm, d), lambda i: (i, 0)),
            pl.BlockSpec((tm, d), lambda i: (i, 0)),
            pl.BlockSpec((tm, d), lambda i: (i, 1)),
            pl.BlockSpec((tm, d), lambda i: (i, 0)),
            pl.BlockSpec((1, 1, d), lambda i: (i // tiles_per_seq, 0, 2)),
            const(wf.shape), const(wa.shape), const(wo.shape),
        ],
        out_specs=pl.BlockSpec((tm, d), lambda i: (i, 0)),
        out_shape=jax.ShapeDtypeStruct((m, d), F32),
        compiler_params=pltpu.CompilerParams(
            dimension_semantics=("parallel",), vmem_limit_bytes=VMEM_LIMIT),
        name="merge",
    )(yg, og, g, g, x2d, mod3, wf, wa, wo)


def _rope_tables(seq, head_dim):
    n_freq = head_dim // 4
    t = np.arange(seq)
    pos = np.stack([t // GRID_W, t % GRID_W], axis=-1).astype(np.float32)
    inv_freq = (np.float32(ROPE_BASE) ** (-np.arange(n_freq, dtype=np.float32) / np.float32(n_freq)))
    ang = (pos[:, :, None] * inv_freq.astype(np.float32)).astype(np.float64)
    ang = np.broadcast_to(ang[:, None, :, :], (seq, 2, 2, n_freq))
    sign = np.array([-1.0, 1.0])[None, :, None, None]
    return (np.cos(ang).reshape(seq, head_dim).astype(np.float32),
            (np.sin(ang) * sign).reshape(seq, head_dim).astype(np.float32))


def _rope_lane_order(a, n_freq):
    lead = a.shape[:-1]
    return a.reshape(lead + (-1, 2, 2, n_freq)).swapaxes(-3, -2).reshape(a.shape)


def kernel(x, c, ctx, c_ctx, w_mod, b_mod, w_in, q_gain, k_gain, rpb, w_f_out, w_a_out, w_out):
    batch, seq, d = x.shape
    ctx_len = ctx.shape[1]
    depth, heads, n_dr, n_dc = rpb.shape
    assert depth == 1 and w_mod.shape[0] == 1
    head_dim = q_gain.shape[1]
    assert head_dim == LANES and seq % GRID_W == 0
    win_r, win_c = (n_dr + 1) // 2, (n_dc + 1) // 2
    attn_w = heads * head_dim
    fw = w_f_out.shape[1]
    gd = fw // F_GROUPS
    off_zf, off_q = fw, 2 * fw
    off_k, off_v, off_za = off_q + attn_w, off_q + 2 * attn_w, off_q + 3 * attn_w
    off_gf = off_za + attn_w
    off_ga = off_gf + d
    assert w_in.shape[2] == off_ga + d

    c_all = jnp.concatenate([c, c_ctx[None, :], jnp.zeros((16 - batch - 1, d), F32)], axis=0)
    mod = _mod(c_all, w_mod[0], b_mod)
    mod3 = mod.reshape(16, 1, 3 * d)

    x2d = x.reshape(batch * seq, d)
    c2d = ctx.reshape(batch * ctx_len, d)
    x_rows, all_rows = batch * seq, batch * (seq + ctx_len)
    u_f, h = _proj_x(x2d, c2d, mod3, w_in[0], batch=batch)

    n_freq = head_dim // 4
    tn = PROJ_TN
    assert batch * ctx_len == PROJ_TM and fw == tn
    cos_np, sin_np = _rope_tables(seq, head_dim)
    ident = np.ones((batch * ctx_len, head_dim), np.float32)
    cos_t = jnp.asarray(np.concatenate([cos_np, ident]))
    sin_t = jnp.asarray(np.concatenate([sin_np, 0.0 * ident]))
    qg = _rope_lane_order(q_gain, n_freq) * (float(head_dim) ** -0.5 * LOG2E)
    kg = _rope_lane_order(k_gain, n_freq)
    proj = functools.partial(_proj, h, w_in[0], cos_t=cos_t, sin_t=sin_t, x_rows=x_rows, seq=seq, tn=tn)
    tiles = lambda a, b: tuple(range(a // tn, b // tn))

    z = proj(gain=qg, kind="silu", col_tiles=tiles(off_zf, off_q) + tiles(off_za, off_gf), m=x_rows)
    q = proj(gain=qg, kind="qk", col_tiles=tiles(off_q, off_k), m=x_rows)
    k = proj(gain=kg, kind="qk", col_tiles=tiles(off_k, off_v), m=all_rows)
    v = proj(gain=qg, kind="raw", col_tiles=tiles(off_v, off_za), m=all_rows)
    g = proj(gain=qg, kind="sig", col_tiles=tiles(off_gf, off_ga + d), m=x_rows)

    rpb_pad = jnp.pad(rpb[0], ((0, 0), (0, 16 - n_dr), (0, LANES - n_dc)))
    og, wf, wa, wo = _attention(q, k, v, z, rpb_pad, (w_f_out[0], w_a_out[0], w_out[0]), batch=batch, seq=seq,
                                ctx_len=ctx_len, heads=heads, z_col0=fw, win_r=win_r, win_c=win_c)

    cc, sc = _dft_mats(gd)
    cn, sn = _dft_mats(seq)
    const = lambda a: jnp.asarray(a.astype(np.float32)).astype(BF16)
    half, fk = seq // 2, FOURIER_FLIP
    nyq = np.zeros((16, seq))
    nyq[0] = cn[half]
    flip = np.zeros((fk, 2 * fk))
    flip[np.arange(fk), fk - np.arange(fk)] = 1.0
    yg = _fourier(u_f, z, const(np.concatenate([cc, sc], axis=1)), const(cn[:half]), const(sn[:half]),
                  const(nyq), const(flip), batch=batch, seq=seq, gd=gd)

    out = _merge(yg, og, g, x2d, mod3, wf, wa, wo, seq=seq)
    return out.reshape(batch, seq, d)
```

```python
import functools

import numpy as np
import jax
import jax.numpy as jnp
from jax import lax
from jax.experimental import pallas as pl
from jax.experimental.pallas import tpu as pltpu

GRID_W = 64
F_GROUPS = 4
ROPE_BASE = 10000.0
EPS = 1e-6
NEG = -1e30
LOG2E = 1.4426950408889634
LANES = 128
VMEM_LIMIT = 56 * 1024 * 1024

MOD_TN = 512
PROJ_TN = 1024
PROJ_TM = 2048
PROJ_X_TM = 512
DOT_ROWS = 128
CAST_ROWS = 64
FOURIER_TK = 512
FOURIER_FLIP = 256
MERGE_TM = 256

BF16 = jnp.bfloat16
F32 = jnp.float32


def _nt_dot(a, b):
    return lax.dot_general(a, b, (((1,), (1,)), ((), ())), preferred_element_type=F32)


def _mod_kernel(c_ref, w_ref, b_ref, o_ref):
    a = jax.nn.silu(c_ref[...]).astype(BF16)
    o_ref[...] = jnp.dot(a, w_ref[...].astype(BF16), preferred_element_type=F32) + b_ref[...]


def _mod(c_all, w_mod, b_mod, tn=MOD_TN):
    m, d = c_all.shape
    n = w_mod.shape[1]
    return pl.pallas_call(
        _mod_kernel,
        grid=(n // tn,),
        in_specs=[pl.BlockSpec((m, d), lambda j: (0, 0)),
                  pl.BlockSpec((d, tn), lambda j: (0, j)),
                  pl.BlockSpec((1, tn), lambda j: (0, j))],
        out_specs=pl.BlockSpec((m, tn), lambda j: (0, j)),
        out_shape=jax.ShapeDtypeStruct((m, n), F32),
        name="mod",
    )(c_all, w_mod, b_mod)


def _rope_lane_order_cols(w):
    n = w.shape[1]
    quarter = (lax.broadcasted_iota(jnp.int32, w.shape, 1) % LANES) // (LANES // 4)
    up = pltpu.roll(w, n - LANES // 4, 1)
    down = pltpu.roll(w, LANES // 4, 1)
    return jnp.where(quarter == 1, up, jnp.where(quarter == 2, down, w))


def _proj_kernel(kind, mm, rc, h_ref, w_ref, gain_ref, cos_ref, sin_ref, o_ref, wb_ref):
    tm = h_ref.shape[0]
    tn = w_ref.shape[1]

    @pl.when(pl.program_id(1) == 0)
    def _():
        _cast_weight_tile(w_ref, wb_ref, rc, kind == "qk")

    for r in range(0, tm, mm):
        a = jnp.dot(h_ref[r:r + mm, :], wb_ref[...], preferred_element_type=F32)
        if kind == "raw":
            o_ref[r:r + mm, :] = a.astype(BF16)
        elif kind == "silu":
            o_ref[r:r + mm, :] = jax.nn.silu(a).astype(BF16)
        elif kind == "sig":
            o_ref[r:r + mm, :] = jax.nn.sigmoid(a).astype(BF16)
        else:
            gain = gain_ref[...]
            for hh in range(tn // LANES):
                xh = a[:, hh * LANES:(hh + 1) * LANES]
                ms = jnp.mean(xh * xh, axis=-1, keepdims=True)
                xn = xh * lax.rsqrt(ms + EPS) * gain
                xn = xn * cos_ref[r:r + mm, :] + pltpu.roll(xn, LANES // 2, 1) * sin_ref[r:r + mm, :]
                o_ref[r:r + mm, hh * LANES:(hh + 1) * LANES] = xn.astype(BF16)


def _cast_weight_tile(w_ref, wb_ref, rc, reorder):
    def body(t, carry):
        r = pl.multiple_of(t * rc, rc)
        wt = w_ref[pl.ds(r, rc), :]
        if reorder:
            wt = _rope_lane_order_cols(wt)
        wb_ref[pl.ds(r, rc), :] = wt.astype(BF16)
        return carry
    lax.fori_loop(0, w_ref.shape[0] // rc, body, 0)


def _proj_x_kernel(mm, rc, n_x, x_ref, c_ref, shift_ref, scale_ref, w_ref, o_ref, h_ref, wb_ref):
    @pl.when(pl.program_id(1) == 0)
    def _():
        _cast_weight_tile(w_ref, wb_ref, rc, False)

    def body(t_ref):
        for r in range(0, t_ref.shape[0], mm):
            xs = t_ref[r:r + mm, :]
            ms = jnp.mean(xs * xs, axis=-1, keepdims=True)
            h = (xs * lax.rsqrt(ms + EPS) * (1.0 + scale_ref[0]) + shift_ref[0]).astype(BF16)
            h_ref[r:r + mm, :] = h
            o_ref[r:r + mm, :] = jnp.dot(h, wb_ref[...], preferred_element_type=F32).astype(BF16)

    is_ctx = pl.program_id(1) >= n_x
    pl.when(jnp.logical_not(is_ctx))(functools.partial(body, x_ref))
    pl.when(is_ctx)(functools.partial(body, c_ref))


def _proj_x(x2d, c2d, mod3, w, *, batch, tm=PROJ_X_TM, tn=PROJ_TN, mm=DOT_ROWS, rc=CAST_ROWS):
    m, d = x2d.shape
    n_x, n_c = m // tm, c2d.shape[0] // tm
    tiles_per_mod = m // batch // tm
    mod_row = lambda i: jnp.minimum(i // tiles_per_mod, batch)
    return pl.pallas_call(
        functools.partial(_proj_x_kernel, mm, rc, n_x),
        grid=(1, n_x + n_c),
        in_specs=[
            pl.BlockSpec((tm, d), lambda j, i: (jnp.minimum(i, n_x - 1), 0)),
            pl.BlockSpec((tm, d), lambda j, i: (jnp.maximum(i - n_x, 0), 0)),
            pl.BlockSpec((1, 1, d), lambda j, i: (mod_row(i), 0, 0)),
            pl.BlockSpec((1, 1, d), lambda j, i: (mod_row(i), 0, 1)),
            pl.BlockSpec((d, tn), lambda j, i: (0, j)),
        ],
        out_specs=[pl.BlockSpec((tm, tn), lambda j, i: (i, j)),
                   pl.BlockSpec((tm, d), lambda j, i: (i, 0))],
        out_shape=[jax.ShapeDtypeStruct(((n_x + n_c) * tm, tn), BF16),
                   jax.ShapeDtypeStruct(((n_x + n_c) * tm, d), BF16)],
        scratch_shapes=[pltpu.VMEM((d, tn), BF16)],
        compiler_params=pltpu.CompilerParams(
            dimension_semantics=("arbitrary", "arbitrary"), vmem_limit_bytes=VMEM_LIMIT),
        name="proj_x",
    )(x2d, c2d, mod3, mod3, w)


def _lookup(j, values):
    out = values[-1]
    for idx in range(len(values) - 2, -1, -1):
        out = jnp.where(j == idx, values[idx], out)
    return out


def _proj(h, w, gain, cos_t, sin_t, *, kind, col_tiles, m, x_rows, seq, tm=PROJ_TM, tn=PROJ_TN, rc=CAST_ROWS):
    d = h.shape[1]
    mm = DOT_ROWS
    x_tiles, seq_tiles = x_rows // tm, seq // tm
    rope_tile = lambda i: jnp.where(i < x_tiles, i % seq_tiles, seq_tiles + i - x_tiles)
    kern = functools.partial(_proj_kernel, kind, mm, rc)
    n = len(col_tiles) * tn
    return pl.pallas_call(
        kern,
        grid=(len(col_tiles), m // tm),
        in_specs=[
            pl.BlockSpec((tm, d), lambda j, i: (i, 0)),
            pl.BlockSpec((d, tn), lambda j, i: (0, _lookup(j, col_tiles))),
            pl.BlockSpec((1, LANES), lambda j, i: (0, 0)),
            pl.BlockSpec((tm, LANES), lambda j, i: (rope_tile(i), 0)),
            pl.BlockSpec((tm, LANES), lambda j, i: (rope_tile(i), 0)),
        ],
        out_specs=pl.BlockSpec((tm, tn), lambda j, i: (i, j)),
        out_shape=jax.ShapeDtypeStruct((m, n), BF16),
        scratch_shapes=[pltpu.VMEM((d, tn), BF16)],
        compiler_params=pltpu.CompilerParams(
            dimension_semantics=("arbitrary", "arbitrary"), vmem_limit_bytes=VMEM_LIMIT),
        name="proj_" + kind,
    )(h, w, gain, cos_t, sin_t)


Q_ROWS = 2
AHEAD = 3
N_SLOTS = AHEAD + 1
HEADS_PER_STEP = 2


class _BandPlan:
    def __init__(self, rows, win_r):
        self.rows, self.win_r = rows, win_r
        band = win_r + Q_ROWS - 1
        self.band = band + band % 2
        self.groups = list(range(0, rows, Q_ROWS))
        self.start = {r0: min(max(r0 - win_r // 2, 0), rows - self.band) for r0 in self.groups}
        self.offsets = sorted({r0 - s0 for r0, s0 in self.start.items()})

    def table(self, r0):
        return self.offsets.index(r0 - self.start[r0])

    def valid_slots(self, r, s0):
        rs = min(max(r - self.win_r // 2, 0), self.rows - self.win_r)
        return tuple(rs <= s0 + i < rs + self.win_r for i in range(self.band))


def _build_bias_tables(rpb_ref, bias_ref, plan, win_c):
    win_r = plan.win_r
    c_io = lax.broadcasted_iota(jnp.int32, (GRID_W, LANES), 0)
    l_io = lax.broadcasted_iota(jnp.int32, (GRID_W, LANES), 1)
    cs = jnp.clip(c_io - win_c // 2, 0, GRID_W - win_c)
    inwin = (l_io >= cs) & (l_io < cs + win_c) & (l_io < GRID_W)
    low = l_io < GRID_W
    neg = jnp.full((GRID_W, LANES), NEG, F32)
    toep = []
    for dr in range(2 * win_r - 1):
        row = jnp.broadcast_to(rpb_ref[dr:dr + 1, :], (GRID_W, LANES))
        t = pltpu.roll(row, LANES - (win_c - 1), 1, stride=1, stride_axis=0)
        toep.append(jnp.where(inwin, t * LOG2E, NEG))
    for tb in range(len(plan.offsets)):
        same = [r0 for r0 in plan.groups if plan.table(r0) == tb]
        r0, s0 = same[0], plan.start[same[0]]
        for rho in range(Q_ROWS):
            r = r0 + rho
            valid = plan.valid_slots(r, s0)
            assert all(plan.valid_slots(o + rho, plan.start[o]) == valid for o in same)
            blocks = [toep[s0 + i - r + win_r - 1] if valid[i] else neg for i in range(plan.band)]
            for p in range(plan.band // 2):
                tile = jnp.where(low, blocks[2 * p], pltpu.roll(blocks[2 * p + 1], GRID_W, 1))
                bias_ref[tb, rho * GRID_W:(rho + 1) * GRID_W, p * LANES:(p + 1) * LANES] = tile


def _attn_kernel(plan, win_c,
                 q_ref, k_ref, v_ref, z_ref, kc_ref, vc_ref, rpb_ref, w1_ref, w2_ref, w3_ref,
                 o_ref, w1b_ref, w2b_ref, w3b_ref, bias_ref, s_ref):
    n_heads = q_ref.shape[1] // LANES

    for w_ref, wb_ref in ((w1_ref, w1b_ref), (w2_ref, w2b_ref), (w3_ref, w3b_ref)):
        wb_ref[...] = w_ref[...].astype(BF16)

    @pl.when(pl.program_id(1) == 0)
    def _():
        for hh in range(n_heads):
            _build_bias_tables(rpb_ref.at[hh], bias_ref.at[hh], plan, win_c)

    nq = Q_ROWS * GRID_W
    nk = plan.band * GRID_W
    dyn_zero = jnp.minimum(pl.program_id(0), 0)

    def scores(i, hh, r0):
        q0, k0 = r0 * GRID_W, plan.start[r0] * GRID_W
        hs = slice(hh * LANES, (hh + 1) * LANES)
        qb = q_ref[q0:q0 + nq, hs]
        s_loc = _nt_dot(qb, k_ref[k0:k0 + nk, hs]) + bias_ref[hh, plan.table(r0)]
        s_ctx = _nt_dot(qb, kc_ref[:, hs])
        slot = i % N_SLOTS + dyn_zero
        s_ref[slot, :, :nk] = s_loc
        s_ref[slot, :, nk:] = s_ctx
        return jnp.maximum(jnp.max(s_loc, axis=-1, keepdims=True), jnp.max(s_ctx, axis=-1, keepdims=True))

    def finish(i, hh, r0, m):
        q0, k0 = r0 * GRID_W, plan.start[r0] * GRID_W
        hs = slice(hh * LANES, (hh + 1) * LANES)
        pb = jnp.exp2(s_ref[i % N_SLOTS + dyn_zero] - m).astype(BF16)
        den = jnp.sum(pb.astype(F32), axis=-1, keepdims=True)
        o = (jnp.dot(pb[:, :nk], v_ref[k0:k0 + nk, hs], preferred_element_type=F32)
             + jnp.dot(pb[:, nk:], vc_ref[:, hs], preferred_element_type=F32))
        og = (o / den) * z_ref[q0:q0 + nq, hs].astype(F32)
        o_ref[q0:q0 + nq, hs] = og.astype(BF16)

    items = [(hh, r0) for hh in range(n_heads) for r0 in plan.groups]
    pending = [scores(i, *item) for i, item in enumerate(items[:AHEAD])]
    for i, item in enumerate(items):
        if i + AHEAD < len(items):
            pending.append(scores(i + AHEAD, *items[i + AHEAD]))
        finish(i, *item, pending.pop(0))


def _attention(q, k, v, z, rpb_pad, weights, *, batch, seq, ctx_len, heads, z_col0, win_r, win_c):
    plan = _BandPlan(seq // GRID_W, win_r)
    kern = functools.partial(_attn_kernel, plan, win_c)
    nq, nk = Q_ROWS * GRID_W, plan.band * GRID_W
    hps, width = HEADS_PER_STEP, HEADS_PER_STEP * LANES
    assert heads % hps == 0 and z_col0 % width == 0
    steps = heads // hps * batch
    tok = pl.BlockSpec((seq, width), lambda h, b: (b, h))
    ctx0 = batch * seq // ctx_len
    ctx = pl.BlockSpec((ctx_len, width), lambda h, b: (ctx0 + b, h))
    w_rows = [pl.BlockSpec((w.shape[0] // steps, w.shape[1]), lambda h, b: (h * batch + b, 0)) for w in weights]
    return pl.pallas_call(
        kern,
        grid=(heads // hps, batch),
        in_specs=[tok, tok, tok, pl.BlockSpec((seq, width), lambda h, b: (b, z_col0 // width + h)), ctx, ctx,
                  pl.BlockSpec((hps,) + rpb_pad.shape[1:], lambda h, b: (h, 0, 0))] + w_rows,
        out_specs=[tok] + w_rows,
        out_shape=[jax.ShapeDtypeStruct((batch * seq, heads * LANES), BF16)]
        + [jax.ShapeDtypeStruct(w.shape, BF16) for w in weights],
        scratch_shapes=[pltpu.VMEM((hps, len(plan.offsets), nq, nk), F32),
                        pltpu.VMEM((N_SLOTS, nq, nk + ctx_len), F32)],
        compiler_params=pltpu.CompilerParams(
            dimension_semantics=("arbitrary", "arbitrary"), vmem_limit_bytes=VMEM_LIMIT),
        name="attn",
    )(q, k, v, z, k, v, rpb_pad, *weights)


def _dft_mats(n):
    jk = (np.arange(n)[:, None] * np.arange(n)[None, :]) % n
    ang = 2.0 * np.pi * jk.astype(np.float64) / n
    return np.cos(ang) / np.sqrt(n), np.sin(ang) / np.sqrt(n)


def _fourier_kernel(gd, u_ref, zf_ref, csc_ref, ch_ref, sh_ref, nyq_ref, flip_ref, o_ref, a_ref, b_ref, e_ref):
    seq = u_ref.shape[0]
    half = seq // 2
    tk = ch_ref.shape[0]
    fk = flip_ref.shape[0]
    n_first = half // tk
    s = pl.program_id(1)

    @pl.when(s == 0)
    def _():
        rb = 4 * DOT_ROWS
        for g in range(u_ref.shape[1] // gd):
            for r in range(0, seq, rb):
                t = jnp.dot(u_ref[r:r + rb, g * gd:(g + 1) * gd], csc_ref[...], preferred_element_type=F32)
                a_ref[r:r + rb, g * gd:(g + 1) * gd] = t[:, :gd].astype(BF16)
                b_ref[r:r + rb, g * gd:(g + 1) * gd] = t[:, gd:].astype(BF16)
        e_ref[half:, :] = jnp.zeros((fk, e_ref.shape[1]), BF16)
        e_ref[half:half + nyq_ref.shape[0], :] = jnp.dot(
            nyq_ref[...], a_ref[...], preferred_element_type=F32).astype(BF16)

    rows = pl.ds(pl.multiple_of(s * tk, tk), tk)
    p = jnp.dot(ch_ref[...], a_ref[...], preferred_element_type=F32)
    q = jnp.dot(sh_ref[...], b_ref[...], preferred_element_type=F32)
    o_ref[rows, :] = ((p - q) * zf_ref[rows, :].astype(F32)).astype(BF16)
    e_ref[rows, :] = (p + q).astype(BF16)

    @pl.when(s == n_first - 1)
    def _():
        for t in range(half // fk):
            base = half - (t + 1) * fk
            y = jnp.dot(flip_ref[...], e_ref[base:base + 2 * fk, :], preferred_element_type=F32)
            out = slice(half + t * fk, half + (t + 1) * fk)
            o_ref[out, :] = (y * zf_ref[out, :].astype(F32)).astype(BF16)


def _fourier(u, zf, csc, ch, sh, nyq, flip, *, batch, seq, gd, tk=FOURIER_TK):
    fw = u.shape[1]
    half = seq // 2
    n_first = half // tk
    fk = flip.shape[0]
    kern = functools.partial(_fourier_kernel, gd)
    per_batch = pl.BlockSpec((seq, fw), lambda b, k: (b, 0))
    half_rows = pl.BlockSpec((tk, seq), lambda b, k: (k, 0))
    whole = lambda a: pl.BlockSpec(a.shape, lambda b, k: (0, 0))
    return pl.pallas_call(
        kern,
        grid=(batch, n_first),
        in_specs=[per_batch, per_batch, whole(csc), half_rows, half_rows, whole(nyq), whole(flip)],
        out_specs=per_batch,
        out_shape=jax.ShapeDtypeStruct((batch * seq, fw), BF16),
        scratch_shapes=[pltpu.VMEM((seq, fw), BF16), pltpu.VMEM((seq, fw), BF16),
                        pltpu.VMEM((half + fk, fw), BF16)],
        compiler_params=pltpu.CompilerParams(
            dimension_semantics=("arbitrary", "arbitrary"), vmem_limit_bytes=VMEM_LIMIT),
        name="fourier",
    )(u, zf, csc, ch, sh, nyq, flip)


def _merge_kernel(yg_ref, og_ref, sgf_ref, sga_ref, x_ref, gate_ref, wf_ref, wa_ref, wo_ref, o_ref):
    yf = jnp.dot(yg_ref[...], wf_ref[...], preferred_element_type=F32)
    ya = jnp.dot(og_ref[...], wa_ref[...], preferred_element_type=F32)
    y = sgf_ref[...].astype(F32) * yf + sga_ref[...].astype(F32) * ya
    yo = jnp.dot(y.astype(BF16), wo_ref[...], preferred_element_type=F32)
    o_ref[...] = x_ref[...] + gate_ref[0] * yo


def _merge(yg, og, g, x2d, mod3, wf, wa, wo, *, seq, tm=MERGE_TM):
    m, d = x2d.shape
    tiles_per_seq = seq // tm
    const = lambda shape: pl.BlockSpec(shape, lambda i: (0, 0), pipeline_mode=pl.Buffered(1))
    return pl.pallas_call(
        _merge_kernel,
        grid=(m // tm,),
        in_specs=[
            pl.BlockSpec((tm, yg.shape[1]), lambda i: (i, 0)),
            pl.BlockSpec((tm, d), lambda i: (i, 0)),
            pl.BlockSpec((tm, d), lambda i: (i, 0)),
            pl.BlockSpec((tm, d), lambda i: (i, 1)),
            pl.BlockSpec((tm, d), lambda i: (i, 0)),
            pl.BlockSpec((1, 1, d), lambda i: (i // tiles_per_seq, 0, 2)),
            const(wf.shape), const(wa.shape), const(wo.shape),
        ],
        out_specs=pl.BlockSpec((tm, d), lambda i: (i, 0)),
        out_shape=jax.ShapeDtypeStruct((m, d), F32),
        compiler_params=pltpu.CompilerParams(
            dimension_semantics=("parallel",), vmem_limit_bytes=VMEM_LIMIT),
        name="merge",
    )(yg, og, g, g, x2d, mod3, wf, wa, wo)


def _rope_tables(seq, head_dim):
    n_freq = head_dim // 4
    t = np.arange(seq)
    pos = np.stack([t // GRID_W, t % GRID_W], axis=-1).astype(np.float32)
    inv_freq = (np.float32(ROPE_BASE) ** (-np.arange(n_freq, dtype=np.float32) / np.float32(n_freq)))
    ang = (pos[:, :, None] * inv_freq.astype(np.float32)).astype(np.float64)
    ang = np.broadcast_to(ang[:, None, :, :], (seq, 2, 2, n_freq))
    sign = np.array([-1.0, 1.0])[None, :, None, None]
    return (np.cos(ang).reshape(seq, head_dim).astype(np.float32),
            (np.sin(ang) * sign).reshape(seq, head_dim).astype(np.float32))


def _rope_lane_order(a, n_freq):
    lead = a.shape[:-1]
    return a.reshape(lead + (-1, 2, 2, n_freq)).swapaxes(-3, -2).reshape(a.shape)


def kernel(x, c, ctx, c_ctx, w_mod, b_mod, w_in, q_gain, k_gain, rpb, w_f_out, w_a_out, w_out):
    batch, seq, d = x.shape
    ctx_len = ctx.shape[1]
    depth, heads, n_dr, n_dc = rpb.shape
    assert depth == 1 and w_mod.shape[0] == 1
    head_dim = q_gain.shape[1]
    assert head_dim == LANES and seq % GRID_W == 0
    win_r, win_c = (n_dr + 1) // 2, (n_dc + 1) // 2
    attn_w = heads * head_dim
    fw = w_f_out.shape[1]
    gd = fw // F_GROUPS
    off_zf, off_q = fw, 2 * fw
    off_k, off_v, off_za = off_q + attn_w, off_q + 2 * attn_w, off_q + 3 * attn_w
    off_gf = off_za + attn_w
    off_ga = off_gf + d
    assert w_in.shape[2] == off_ga + d

    c_all = jnp.concatenate([c, c_ctx[None, :], jnp.zeros((16 - batch - 1, d), F32)], axis=0)
    mod = _mod(c_all, w_mod[0], b_mod)
    mod3 = mod.reshape(16, 1, 3 * d)

    x2d = x.reshape(batch * seq, d)
    c2d = ctx.reshape(batch * ctx_len, d)
    x_rows, all_rows = batch * seq, batch * (seq + ctx_len)
    u_f, h = _proj_x(x2d, c2d, mod3, w_in[0], batch=batch)

    n_freq = head_dim // 4
    tn = PROJ_TN
    assert batch * ctx_len == PROJ_TM and fw == tn
    cos_np, sin_np = _rope_tables(seq, head_dim)
    ident = np.ones((batch * ctx_len, head_dim), np.float32)
    cos_t = jnp.asarray(np.concatenate([cos_np, ident]))
    sin_t = jnp.asarray(np.concatenate([sin_np, 0.0 * ident]))
    qg = _rope_lane_order(q_gain, n_freq) * (float(head_dim) ** -0.5 * LOG2E)
    kg = _rope_lane_order(k_gain, n_freq)
    proj = functools.partial(_proj, h, w_in[0], cos_t=cos_t, sin_t=sin_t, x_rows=x_rows, seq=seq, tn=tn)
    tiles = lambda a, b: tuple(range(a // tn, b // tn))

    z = proj(gain=qg, kind="silu", col_tiles=tiles(off_zf, off_q) + tiles(off_za, off_gf), m=x_rows)
    q = proj(gain=qg, kind="qk", col_tiles=tiles(off_q, off_k), m=x_rows)
    k = proj(gain=kg, kind="qk", col_tiles=tiles(off_k, off_v), m=all_rows)
    v = proj(gain=qg, kind="raw", col_tiles=tiles(off_v, off_za), m=all_rows)
    g = proj(gain=qg, kind="sig", col_tiles=tiles(off_gf, off_ga + d), m=x_rows)

    rpb_pad = jnp.pad(rpb[0], ((0, 0), (0, 16 - n_dr), (0, LANES - n_dc)))
    og, wf, wa, wo = _attention(q, k, v, z, rpb_pad, (w_f_out[0], w_a_out[0], w_out[0]), batch=batch, seq=seq,
                                ctx_len=ctx_len, heads=heads, z_col0=fw, win_r=win_r, win_c=win_c)

    cc, sc = _dft_mats(gd)
    cn, sn = _dft_mats(seq)
    const = lambda a: jnp.asarray(a.astype(np.float32)).astype(BF16)
    half, fk = seq // 2, FOURIER_FLIP
    nyq = np.zeros((16, seq))
    nyq[0] = cn[half]
    flip = np.zeros((fk, 2 * fk))
    flip[np.arange(fk), fk - np.arange(fk)] = 1.0
    yg = _fourier(u_f, z, const(np.concatenate([cc, sc], axis=1)), const(cn[:half]), const(sn[:half]),
                  const(nyq), const(flip), batch=batch, seq=seq, gd=gd)

    out = _merge(yg, og, g, x2d, mod3, wf, wa, wo, seq=seq)
    return out.reshape(batch, seq, d)
```

```python
import functools

import numpy as np
import jax
import jax.numpy as jnp
from jax import lax
from jax.experimental import pallas as pl
from jax.experimental.pallas import tpu as pltpu

GRID_W = 64
F_GROUPS = 4
ROPE_BASE = 10000.0
EPS = 1e-6
NEG = -1e30
LOG2E = 1.4426950408889634
LANES = 128
VMEM_LIMIT = 56 * 1024 * 1024

MOD_TN = 1024
PROJ_TN = 1024
PROJ_TM = 2048
PROJ_X_TM = 512
DOT_ROWS = 128
CAST_ROWS = 64
FOURIER_TK = 512
FOURIER_FLIP = 256
MERGE_TM = 256

BF16 = jnp.bfloat16
F32 = jnp.float32


def _nt_dot(a, b):
    return lax.dot_general(a, b, (((1,), (1,)), ((), ())), preferred_element_type=F32)


def _mod_kernel(c_ref, w_ref, b_ref, o_ref):
    a = jax.nn.silu(c_ref[...]).astype(BF16)
    o_ref[...] = jnp.dot(a, w_ref[...].astype(BF16), preferred_element_type=F32) + b_ref[...]


def _mod(c_all, w_mod, b_mod, tn=MOD_TN):
    m, d = c_all.shape
    n = w_mod.shape[1]
    return pl.pallas_call(
        _mod_kernel,
        grid=(n // tn,),
        in_specs=[pl.BlockSpec((m, d), lambda j: (0, 0)),
                  pl.BlockSpec((d, tn), lambda j: (0, j)),
                  pl.BlockSpec((1, tn), lambda j: (0, j))],
        out_specs=pl.BlockSpec((m, tn), lambda j: (0, j)),
        out_shape=jax.ShapeDtypeStruct((m, n), F32),
        name="mod",
    )(c_all, w_mod, b_mod)


def _rope_lane_order_cols(w):
    n = w.shape[1]
    quarter = (lax.broadcasted_iota(jnp.int32, w.shape, 1) % LANES) // (LANES // 4)
    up = pltpu.roll(w, n - LANES // 4, 1)
    down = pltpu.roll(w, LANES // 4, 1)
    return jnp.where(quarter == 1, up, jnp.where(quarter == 2, down, w))


def _proj_kernel(kind, mm, rc, h_ref, w_ref, gain_ref, cos_ref, sin_ref, o_ref, wb_ref):
    tm = h_ref.shape[0]
    tn = w_ref.shape[1]

    @pl.when(pl.program_id(1) == 0)
    def _():
        _cast_weight_tile(w_ref, wb_ref, rc, kind == "qk")

    for r in range(0, tm, mm):
        a = jnp.dot(h_ref[r:r + mm, :], wb_ref[...], preferred_element_type=F32)
        if kind == "raw":
            o_ref[r:r + mm, :] = a.astype(BF16)
        elif kind == "silu":
            o_ref[r:r + mm, :] = jax.nn.silu(a).astype(BF16)
        elif kind == "sig":
            o_ref[r:r + mm, :] = jax.nn.sigmoid(a).astype(BF16)
        else:
            gain = gain_ref[...]
            for hh in range(tn // LANES):
                xh = a[:, hh * LANES:(hh + 1) * LANES]
                ms = jnp.mean(xh * xh, axis=-1, keepdims=True)
                xn = xh * lax.rsqrt(ms + EPS) * gain
                xn = xn * cos_ref[r:r + mm, :] + pltpu.roll(xn, LANES // 2, 1) * sin_ref[r:r + mm, :]
                o_ref[r:r + mm, hh * LANES:(hh + 1) * LANES] = xn.astype(BF16)


def _cast_weight_tile(w_ref, wb_ref, rc, reorder):
    def body(t, carry):
        r = pl.multiple_of(t * rc, rc)
        wt = w_ref[pl.ds(r, rc), :]
        if reorder:
            wt = _rope_lane_order_cols(wt)
        wb_ref[pl.ds(r, rc), :] = wt.astype(BF16)
        return carry
    lax.fori_loop(0, w_ref.shape[0] // rc, body, 0)


def _proj_x_kernel(mm, rc, n_x, x_ref, c_ref, shift_ref, scale_ref, w_ref, o_ref, h_ref, wb_ref):
    @pl.when(pl.program_id(1) == 0)
    def _():
        _cast_weight_tile(w_ref, wb_ref, rc, False)

    def body(t_ref):
        for r in range(0, t_ref.shape[0], mm):
            xs = t_ref[r:r + mm, :]
            ms = jnp.mean(xs * xs, axis=-1, keepdims=True)
            h = (xs * lax.rsqrt(ms + EPS) * (1.0 + scale_ref[0]) + shift_ref[0]).astype(BF16)
            h_ref[r:r + mm, :] = h
            o_ref[r:r + mm, :] = jnp.dot(h, wb_ref[...], preferred_element_type=F32).astype(BF16)

    is_ctx = pl.program_id(1) >= n_x
    pl.when(jnp.logical_not(is_ctx))(functools.partial(body, x_ref))
    pl.when(is_ctx)(functools.partial(body, c_ref))


def _proj_x(x2d, c2d, mod3, w, *, batch, tm=PROJ_X_TM, tn=PROJ_TN, mm=DOT_ROWS, rc=CAST_ROWS):
    m, d = x2d.shape
    n_x, n_c = m // tm, c2d.shape[0] // tm
    tiles_per_mod = m // batch // tm
    mod_row = lambda i: jnp.minimum(i // tiles_per_mod, batch)
    return pl.pallas_call(
        functools.partial(_proj_x_kernel, mm, rc, n_x),
        grid=(1, n_x + n_c),
        in_specs=[
            pl.BlockSpec((tm, d), lambda j, i: (jnp.minimum(i, n_x - 1), 0)),
            pl.BlockSpec((tm, d), lambda j, i: (jnp.maximum(i - n_x, 0), 0)),
            pl.BlockSpec((1, 1, d), lambda j, i: (mod_row(i), 0, 0)),
            pl.BlockSpec((1, 1, d), lambda j, i: (mod_row(i), 0, 1)),
            pl.BlockSpec((d, tn), lambda j, i: (0, j)),
        ],
        out_specs=[pl.BlockSpec((tm, tn), lambda j, i: (i, j)),
                   pl.BlockSpec((tm, d), lambda j, i: (i, 0))],
        out_shape=[jax.ShapeDtypeStruct(((n_x + n_c) * tm, tn), BF16),
                   jax.ShapeDtypeStruct(((n_x + n_c) * tm, d), BF16)],
        scratch_shapes=[pltpu.VMEM((d, tn), BF16)],
        compiler_params=pltpu.CompilerParams(
            dimension_semantics=("arbitrary", "arbitrary"), vmem_limit_bytes=VMEM_LIMIT),
        name="proj_x",
    )(x2d, c2d, mod3, mod3, w)


def _lookup(j, values):
    out = values[-1]
    for idx in range(len(values) - 2, -1, -1):
        out = jnp.where(j == idx, values[idx], out)
    return out


def _proj(h, w, gain, cos_t, sin_t, *, kind, col_tiles, m, x_rows, seq, tm=PROJ_TM, tn=PROJ_TN, rc=CAST_ROWS):
    d = h.shape[1]
    mm = DOT_ROWS
    x_tiles, seq_tiles = x_rows // tm, seq // tm
    rope_tile = lambda i: jnp.where(i < x_tiles, i % seq_tiles, seq_tiles + i - x_tiles)
    kern = functools.partial(_proj_kernel, kind, mm, rc)
    n = len(col_tiles) * tn
    return pl.pallas_call(
        kern,
        grid=(len(col_tiles), m // tm),
        in_specs=[
            pl.BlockSpec((tm, d), lambda j, i: (i, 0)),
            pl.BlockSpec((d, tn), lambda j, i: (0, _lookup(j, col_tiles))),
            pl.BlockSpec((1, LANES), lambda j, i: (0, 0)),
            pl.BlockSpec((tm, LANES), lambda j, i: (rope_tile(i), 0)),
            pl.BlockSpec((tm, LANES), lambda j, i: (rope_tile(i), 0)),
        ],
        out_specs=pl.BlockSpec((tm, tn), lambda j, i: (i, j)),
        out_shape=jax.ShapeDtypeStruct((m, n), BF16),
        scratch_shapes=[pltpu.VMEM((d, tn), BF16)],
        compiler_params=pltpu.CompilerParams(
            dimension_semantics=("arbitrary", "arbitrary"), vmem_limit_bytes=VMEM_LIMIT),
        name="proj_" + kind,
    )(h, w, gain, cos_t, sin_t)


Q_ROWS = 2
AHEAD = 3
N_SLOTS = AHEAD + 1
HEADS_PER_STEP = 4


class _BandPlan:
    def __init__(self, rows, win_r):
        self.rows, self.win_r = rows, win_r
        band = win_r + Q_ROWS - 1
        self.band = band + band % 2
        self.groups = list(range(0, rows, Q_ROWS))
        self.start = {r0: min(max(r0 - win_r // 2, 0), rows - self.band) for r0 in self.groups}
        self.offsets = sorted({r0 - s0 for r0, s0 in self.start.items()})

    def table(self, r0):
        return self.offsets.index(r0 - self.start[r0])

    def valid_slots(self, r, s0):
        rs = min(max(r - self.win_r // 2, 0), self.rows - self.win_r)
        return tuple(rs <= s0 + i < rs + self.win_r for i in range(self.band))


def _build_bias_tables(rpb_ref, bias_ref, plan, win_c):
    win_r = plan.win_r
    c_io = lax.broadcasted_iota(jnp.int32, (GRID_W, LANES), 0)
    l_io = lax.broadcasted_iota(jnp.int32, (GRID_W, LANES), 1)
    cs = jnp.clip(c_io - win_c // 2, 0, GRID_W - win_c)
    inwin = (l_io >= cs) & (l_io < cs + win_c) & (l_io < GRID_W)
    low = l_io < GRID_W
    neg = jnp.full((GRID_W, LANES), NEG, F32)
    toep = []
    for dr in range(2 * win_r - 1):
        row = jnp.broadcast_to(rpb_ref[dr:dr + 1, :], (GRID_W, LANES))
        t = pltpu.roll(row, LANES - (win_c - 1), 1, stride=1, stride_axis=0)
        toep.append(jnp.where(inwin, t * LOG2E, NEG))
    for tb in range(len(plan.offsets)):
        same = [r0 for r0 in plan.groups if plan.table(r0) == tb]
        r0, s0 = same[0], plan.start[same[0]]
        for rho in range(Q_ROWS):
            r = r0 + rho
            valid = plan.valid_slots(r, s0)
            assert all(plan.valid_slots(o + rho, plan.start[o]) == valid for o in same)
            blocks = [toep[s0 + i - r + win_r - 1] if valid[i] else neg for i in range(plan.band)]
            for p in range(plan.band // 2):
                tile = jnp.where(low, blocks[2 * p], pltpu.roll(blocks[2 * p + 1], GRID_W, 1))
                bias_ref[tb, rho * GRID_W:(rho + 1) * GRID_W, p * LANES:(p + 1) * LANES] = tile


def _attn_kernel(plan, win_c,
                 q_ref, k_ref, v_ref, z_ref, kc_ref, vc_ref, rpb_ref, w1_ref, w2_ref, w3_ref,
                 o_ref, w1b_ref, w2b_ref, w3b_ref, bias_ref, s_ref):
    n_heads = q_ref.shape[1] // LANES

    for w_ref, wb_ref in ((w1_ref, w1b_ref), (w2_ref, w2b_ref), (w3_ref, w3b_ref)):
        wb_ref[...] = w_ref[...].astype(BF16)

    @pl.when(pl.program_id(1) == 0)
    def _():
        for hh in range(n_heads):
            _build_bias_tables(rpb_ref.at[hh], bias_ref.at[hh], plan, win_c)

    nq = Q_ROWS * GRID_W
    nk = plan.band * GRID_W
    dyn_zero = jnp.minimum(pl.program_id(0), 0)

    def scores(i, hh, r0):
        q0, k0 = r0 * GRID_W, plan.start[r0] * GRID_W
        hs = slice(hh * LANES, (hh + 1) * LANES)
        qb = q_ref[q0:q0 + nq, hs]
        s_loc = _nt_dot(qb, k_ref[k0:k0 + nk, hs]) + bias_ref[hh, plan.table(r0)]
        s_ctx = _nt_dot(qb, kc_ref[:, hs])
        slot = i % N_SLOTS + dyn_zero
        s_ref[slot, :, :nk] = s_loc
        s_ref[slot, :, nk:] = s_ctx
        return jnp.maximum(jnp.max(s_loc, axis=-1, keepdims=True), jnp.max(s_ctx, axis=-1, keepdims=True))

    def finish(i, hh, r0, m):
        q0, k0 = r0 * GRID_W, plan.start[r0] * GRID_W
        hs = slice(hh * LANES, (hh + 1) * LANES)
        pb = jnp.exp2(s_ref[i % N_SLOTS + dyn_zero] - m).astype(BF16)
        den = jnp.sum(pb.astype(F32), axis=-1, keepdims=True)
        o = (jnp.dot(pb[:, :nk], v_ref[k0:k0 + nk, hs], preferred_element_type=F32)
             + jnp.dot(pb[:, nk:], vc_ref[:, hs], preferred_element_type=F32))
        og = (o / den) * z_ref[q0:q0 + nq, hs].astype(F32)
        o_ref[q0:q0 + nq, hs] = og.astype(BF16)

    items = [(hh, r0) for hh in range(n_heads) for r0 in plan.groups]
    pending = [scores(i, *item) for i, item in enumerate(items[:AHEAD])]
    for i, item in enumerate(items):
        if i + AHEAD < len(items):
            pending.append(scores(i + AHEAD, *items[i + AHEAD]))
        finish(i, *item, pending.pop(0))


def _attention(q, k, v, z, rpb_pad, weights, *, batch, seq, ctx_len, heads, z_col0, win_r, win_c):
    plan = _BandPlan(seq // GRID_W, win_r)
    kern = functools.partial(_attn_kernel, plan, win_c)
    nq, nk = Q_ROWS * GRID_W, plan.band * GRID_W
    hps, width = HEADS_PER_STEP, HEADS_PER_STEP * LANES
    assert heads % hps == 0 and z_col0 % width == 0
    steps = heads // hps * batch
    tok = pl.BlockSpec((seq, width), lambda h, b: (b, h))
    ctx0 = batch * seq // ctx_len
    ctx = pl.BlockSpec((ctx_len, width), lambda h, b: (ctx0 + b, h))
    w_rows = [pl.BlockSpec((w.shape[0] // steps, w.shape[1]), lambda h, b: (h * batch + b, 0)) for w in weights]
    return pl.pallas_call(
        kern,
        grid=(heads // hps, batch),
        in_specs=[tok, tok, tok, pl.BlockSpec((seq, width), lambda h, b: (b, z_col0 // width + h)), ctx, ctx,
                  pl.BlockSpec((hps,) + rpb_pad.shape[1:], lambda h, b: (h, 0, 0))] + w_rows,
        out_specs=[tok] + w_rows,
        out_shape=[jax.ShapeDtypeStruct((batch * seq, heads * LANES), BF16)]
        + [jax.ShapeDtypeStruct(w.shape, BF16) for w in weights],
        scratch_shapes=[pltpu.VMEM((hps, len(plan.offsets), nq, nk), F32),
                        pltpu.VMEM((N_SLOTS, nq, nk + ctx_len), F32)],
        compiler_params=pltpu.CompilerParams(
            dimension_semantics=("arbitrary", "arbitrary"), vmem_limit_bytes=VMEM_LIMIT),
        name="attn",
    )(q, k, v, z, k, v, rpb_pad, *weights)


def _dft_mats(n):
    jk = (np.arange(n)[:, None] * np.arange(n)[None, :]) % n
    ang = 2.0 * np.pi * jk.astype(np.float64) / n
    return np.cos(ang) / np.sqrt(n), np.sin(ang) / np.sqrt(n)


def _fourier_kernel(gd, u_ref, zf_ref, csc_ref, ch_ref, sh_ref, nyq_ref, flip_ref, o_ref, a_ref, b_ref, e_ref):
    seq = u_ref.shape[0]
    half = seq // 2
    tk = ch_ref.shape[0]
    fk = flip_ref.shape[0]
    n_first = half // tk
    s = pl.program_id(1)

    @pl.when(s == 0)
    def _():
        rb = 4 * DOT_ROWS
        for g in range(u_ref.shape[1] // gd):
            for r in range(0, seq, rb):
                t = jnp.dot(u_ref[r:r + rb, g * gd:(g + 1) * gd], csc_ref[...], preferred_element_type=F32)
                a_ref[r:r + rb, g * gd:(g + 1) * gd] = t[:, :gd].astype(BF16)
                b_ref[r:r + rb, g * gd:(g + 1) * gd] = t[:, gd:].astype(BF16)
        e_ref[half:, :] = jnp.zeros((fk, e_ref.shape[1]), BF16)
        e_ref[half:half + nyq_ref.shape[0], :] = jnp.dot(
            nyq_ref[...], a_ref[...], preferred_element_type=F32).astype(BF16)

    rows = pl.ds(pl.multiple_of(s * tk, tk), tk)
    p = jnp.dot(ch_ref[...], a_ref[...], preferred_element_type=F32)
    q = jnp.dot(sh_ref[...], b_ref[...], preferred_element_type=F32)
    o_ref[rows, :] = ((p - q) * zf_ref[rows, :].astype(F32)).astype(BF16)
    e_ref[rows, :] = (p + q).astype(BF16)

    @pl.when(s == n_first - 1)
    def _():
        for t in range(half // fk):
            base = half - (t + 1) * fk
            y = jnp.dot(flip_ref[...], e_ref[base:base + 2 * fk, :], preferred_element_type=F32)
            out = slice(half + t * fk, half + (t + 1) * fk)
            o_ref[out, :] = (y * zf_ref[out, :].astype(F32)).astype(BF16)


def _fourier(u, zf, csc, ch, sh, nyq, flip, *, batch, seq, gd, tk=FOURIER_TK):
    fw = u.shape[1]
    half = seq // 2
    n_first = half // tk
    fk = flip.shape[0]
    kern = functools.partial(_fourier_kernel, gd)
    per_batch = pl.BlockSpec((seq, fw), lambda b, k: (b, 0))
    half_rows = pl.BlockSpec((tk, seq), lambda b, k: (k, 0))
    whole = lambda a: pl.BlockSpec(a.shape, lambda b, k: (0, 0))
    return pl.pallas_call(
        kern,
        grid=(batch, n_first),
        in_specs=[per_batch, per_batch, whole(csc), half_rows, half_rows, whole(nyq), whole(flip)],
        out_specs=per_batch,
        out_shape=jax.ShapeDtypeStruct((batch * seq, fw), BF16),
        scratch_shapes=[pltpu.VMEM((seq, fw), BF16), pltpu.VMEM((seq, fw), BF16),
                        pltpu.VMEM((half + fk, fw), BF16)],
        compiler_params=pltpu.CompilerParams(
            dimension_semantics=("arbitrary", "arbitrary"), vmem_limit_bytes=VMEM_LIMIT),
        name="fourier",
    )(u, zf, csc, ch, sh, nyq, flip)


def _merge_kernel(yg_ref, og_ref, sgf_ref, sga_ref, x_ref, gate_ref, wf_ref, wa_ref, wo_ref, o_ref):
    yf = jnp.dot(yg_ref[...], wf_ref[...], preferred_element_type=F32)
    ya = jnp.dot(og_ref[...], wa_ref[...], preferred_element_type=F32)
    y = sgf_ref[...].astype(F32) * yf + sga_ref[...].astype(F32) * ya
    yo = jnp.dot(y.astype(BF16), wo_ref[...], preferred_element_type=F32)
    o_ref[...] = x_ref[...] + gate_ref[0] * yo


def _merge(yg, og, g, x2d, mod3, wf, wa, wo, *, seq, tm=MERGE_TM):
    m, d = x2d.shape
    tiles_per_seq = seq // tm
    const = lambda shape: pl.BlockSpec(shape, lambda i: (0, 0), pipeline_mode=pl.Buffered(1))
    return pl.pallas_call(
        _merge_kernel,
        grid=(m // tm,),
        in_specs=[
            pl.BlockSpec((tm, yg.shape[1]), lambda i: (i, 0)),
            pl.BlockSpec((tm, d), lambda i: (i, 0)),
            pl.BlockSpec((tm, d), lambda i: (i, 0)),
            pl.BlockSpec((tm, d), lambda i: (i, 1)),
            pl.BlockSpec((tm, d), lambda i: (i, 0)),
            pl.BlockSpec((1, 1, d), lambda i: (i // tiles_per_seq, 0, 2)),
            const(wf.shape), const(wa.shape), const(wo.shape),
        ],
        out_specs=pl.BlockSpec((tm, d), lambda i: (i, 0)),
        out_shape=jax.ShapeDtypeStruct((m, d), F32),
        compiler_params=pltpu.CompilerParams(
            dimension_semantics=("parallel",), vmem_limit_bytes=VMEM_LIMIT),
        name="merge",
    )(yg, og, g, g, x2d, mod3, wf, wa, wo)


def _rope_tables(seq, head_dim):
    n_freq = head_dim // 4
    t = np.arange(seq)
    pos = np.stack([t // GRID_W, t % GRID_W], axis=-1).astype(np.float32)
    inv_freq = (np.float32(ROPE_BASE) ** (-np.arange(n_freq, dtype=np.float32) / np.float32(n_freq)))
    ang = (pos[:, :, None] * inv_freq.astype(np.float32)).astype(np.float64)
    ang = np.broadcast_to(ang[:, None, :, :], (seq, 2, 2, n_freq))
    sign = np.array([-1.0, 1.0])[None, :, None, None]
    return (np.cos(ang).reshape(seq, head_dim).astype(np.float32),
            (np.sin(ang) * sign).reshape(seq, head_dim).astype(np.float32))


def _rope_lane_order(a, n_freq):
    lead = a.shape[:-1]
    return a.reshape(lead + (-1, 2, 2, n_freq)).swapaxes(-3, -2).reshape(a.shape)


def kernel(x, c, ctx, c_ctx, w_mod, b_mod, w_in, q_gain, k_gain, rpb, w_f_out, w_a_out, w_out):
    batch, seq, d = x.shape
    ctx_len = ctx.shape[1]
    depth, heads, n_dr, n_dc = rpb.shape
    assert depth == 1 and w_mod.shape[0] == 1
    head_dim = q_gain.shape[1]
    assert head_dim == LANES and seq % GRID_W == 0
    win_r, win_c = (n_dr + 1) // 2, (n_dc + 1) // 2
    attn_w = heads * head_dim
    fw = w_f_out.shape[1]
    gd = fw // F_GROUPS
    off_zf, off_q = fw, 2 * fw
    off_k, off_v, off_za = off_q + attn_w, off_q + 2 * attn_w, off_q + 3 * attn_w
    off_gf = off_za + attn_w
    off_ga = off_gf + d
    assert w_in.shape[2] == off_ga + d

    c_all = jnp.concatenate([c, c_ctx[None, :], jnp.zeros((16 - batch - 1, d), F32)], axis=0)
    mod = _mod(c_all, w_mod[0], b_mod)
    mod3 = mod.reshape(16, 1, 3 * d)

    x2d = x.reshape(batch * seq, d)
    c2d = ctx.reshape(batch * ctx_len, d)
    x_rows, all_rows = batch * seq, batch * (seq + ctx_len)
    u_f, h = _proj_x(x2d, c2d, mod3, w_in[0], batch=batch)

    n_freq = head_dim // 4
    tn = PROJ_TN
    assert batch * ctx_len == PROJ_TM and fw == tn
    cos_np, sin_np = _rope_tables(seq, head_dim)
    ident = np.ones((batch * ctx_len, head_dim), np.float32)
    cos_t = jnp.asarray(np.concatenate([cos_np, ident]))
    sin_t = jnp.asarray(np.concatenate([sin_np, 0.0 * ident]))
    qg = _rope_lane_order(q_gain, n_freq) * (float(head_dim) ** -0.5 * LOG2E)
    kg = _rope_lane_order(k_gain, n_freq)
    proj = functools.partial(_proj, h, w_in[0], cos_t=cos_t, sin_t=sin_t, x_rows=x_rows, seq=seq, tn=tn)
    tiles = lambda a, b: tuple(range(a // tn, b // tn))

    z = proj(gain=qg, kind="silu", col_tiles=tiles(off_zf, off_q) + tiles(off_za, off_gf), m=x_rows)
    q = proj(gain=qg, kind="qk", col_tiles=tiles(off_q, off_k), m=x_rows)
    k = proj(gain=kg, kind="qk", col_tiles=tiles(off_k, off_v), m=all_rows)
    v = proj(gain=qg, kind="raw", col_tiles=tiles(off_v, off_za), m=all_rows)
    g = proj(gain=qg, kind="sig", col_tiles=tiles(off_gf, off_ga + d), m=x_rows)

    rpb_pad = jnp.pad(rpb[0], ((0, 0), (0, 16 - n_dr), (0, LANES - n_dc)))
    og, wf, wa, wo = _attention(q, k, v, z, rpb_pad, (w_f_out[0], w_a_out[0], w_out[0]), batch=batch, seq=seq,
                                ctx_len=ctx_len, heads=heads, z_col0=fw, win_r=win_r, win_c=win_c)

    cc, sc = _dft_mats(gd)
    cn, sn = _dft_mats(seq)
    const = lambda a: jnp.asarray(a.astype(np.float32)).astype(BF16)
    half, fk = seq // 2, FOURIER_FLIP
    nyq = np.zeros((16, seq))
    nyq[0] = cn[half]
    flip = np.zeros((fk, 2 * fk))
    flip[np.arange(fk), fk - np.arange(fk)] = 1.0
    yg = _fourier(u_f, z, const(np.concatenate([cc, sc], axis=1)), const(cn[:half]), const(sn[:half]),
                  const(nyq), const(flip), batch=batch, seq=seq, gd=gd)

    out = _merge(yg, og, g, x2d, mod3, wf, wa, wo, seq=seq)
    return out.reshape(batch, seq, d)
```

```python
import functools

import numpy as np
import jax
import jax.numpy as jnp
from jax import lax
from jax.experimental import pallas as pl
from jax.experimental.pallas import tpu as pltpu

GRID_W = 64
F_GROUPS = 4
ROPE_BASE = 10000.0
EPS = 1e-6
NEG = -1e30
LOG2E = 1.4426950408889634
LANES = 128
VMEM_LIMIT = 56 * 1024 * 1024

MOD_TN = 512
PROJ_TN = 1024
PROJ_TM = 2048
PROJ_X_TM = 512
DOT_ROWS = 128
CAST_ROWS = 64
FOURIER_TK = 512
FOURIER_FLIP = 256
MERGE_TM = 512
MERGE_ROWS = 256

BF16 = jnp.bfloat16
F32 = jnp.float32


def _nt_dot(a, b):
    return lax.dot_general(a, b, (((1,), (1,)), ((), ())), preferred_element_type=F32)


def _mod_kernel(c_ref, w_ref, b_ref, o_ref):
    a = jax.nn.silu(c_ref[...]).astype(BF16)
    o_ref[...] = jnp.dot(a, w_ref[...].astype(BF16), preferred_element_type=F32) + b_ref[...]


def _mod(c_all, w_mod, b_mod, tn=MOD_TN):
    m, d = c_all.shape
    n = w_mod.shape[1]
    return pl.pallas_call(
        _mod_kernel,
        grid=(n // tn,),
        in_specs=[pl.BlockSpec((m, d), lambda j: (0, 0)),
                  pl.BlockSpec((d, tn), lambda j: (0, j)),
                  pl.BlockSpec((1, tn), lambda j: (0, j))],
        out_specs=pl.BlockSpec((m, tn), lambda j: (0, j)),
        out_shape=jax.ShapeDtypeStruct((m, n), F32),
        name="mod",
    )(c_all, w_mod, b_mod)


def _rope_lane_order_cols(w):
    n = w.shape[1]
    quarter = (lax.broadcasted_iota(jnp.int32, w.shape, 1) % LANES) // (LANES // 4)
    up = pltpu.roll(w, n - LANES // 4, 1)
    down = pltpu.roll(w, LANES // 4, 1)
    return jnp.where(quarter == 1, up, jnp.where(quarter == 2, down, w))


def _proj_kernel(kind, mm, rc, h_ref, w_ref, gain_ref, cos_ref, sin_ref, o_ref, wb_ref):
    tm = h_ref.shape[0]
    tn = w_ref.shape[1]

    @pl.when(pl.program_id(1) == 0)
    def _():
        _cast_weight_tile(w_ref, wb_ref, rc, kind == "qk")

    for r in range(0, tm, mm):
        a = jnp.dot(h_ref[r:r + mm, :], wb_ref[...], preferred_element_type=F32)
        if kind == "raw":
            o_ref[r:r + mm, :] = a.astype(BF16)
        elif kind == "silu":
            o_ref[r:r + mm, :] = jax.nn.silu(a).astype(BF16)
        elif kind == "sig":
            o_ref[r:r + mm, :] = jax.nn.sigmoid(a).astype(BF16)
        else:
            gain = gain_ref[...]
            for hh in range(tn // LANES):
                xh = a[:, hh * LANES:(hh + 1) * LANES]
                ms = jnp.mean(xh * xh, axis=-1, keepdims=True)
                xn = xh * lax.rsqrt(ms + EPS) * gain
                xn = xn * cos_ref[r:r + mm, :] + pltpu.roll(xn, LANES // 2, 1) * sin_ref[r:r + mm, :]
                o_ref[r:r + mm, hh * LANES:(hh + 1) * LANES] = xn.astype(BF16)


def _cast_weight_tile(w_ref, wb_ref, rc, reorder):
    def body(t, carry):
        r = pl.multiple_of(t * rc, rc)
        wt = w_ref[pl.ds(r, rc), :]
        if reorder:
            wt = _rope_lane_order_cols(wt)
        wb_ref[pl.ds(r, rc), :] = wt.astype(BF16)
        return carry
    lax.fori_loop(0, w_ref.shape[0] // rc, body, 0)


def _proj_x_kernel(mm, rc, n_x, x_ref, c_ref, shift_ref, scale_ref, w_ref, o_ref, h_ref, wb_ref):
    @pl.when(pl.program_id(1) == 0)
    def _():
        _cast_weight_tile(w_ref, wb_ref, rc, False)

    def body(t_ref):
        for r in range(0, t_ref.shape[0], mm):
            xs = t_ref[r:r + mm, :]
            ms = jnp.mean(xs * xs, axis=-1, keepdims=True)
            h = (xs * lax.rsqrt(ms + EPS) * (1.0 + scale_ref[0]) + shift_ref[0]).astype(BF16)
            h_ref[r:r + mm, :] = h
            o_ref[r:r + mm, :] = jnp.dot(h, wb_ref[...], preferred_element_type=F32).astype(BF16)

    is_ctx = pl.program_id(1) >= n_x
    pl.when(jnp.logical_not(is_ctx))(functools.partial(body, x_ref))
    pl.when(is_ctx)(functools.partial(body, c_ref))


def _proj_x(x2d, c2d, mod3, w, *, batch, tm=PROJ_X_TM, tn=PROJ_TN, mm=DOT_ROWS, rc=CAST_ROWS):
    m, d = x2d.shape
    n_x, n_c = m // tm, c2d.shape[0] // tm
    tiles_per_mod = m // batch // tm
    mod_row = lambda i: jnp.minimum(i // tiles_per_mod, batch)
    return pl.pallas_call(
        functools.partial(_proj_x_kernel, mm, rc, n_x),
        grid=(1, n_x + n_c),
        in_specs=[
            pl.BlockSpec((tm, d), lambda j, i: (jnp.minimum(i, n_x - 1), 0)),
            pl.BlockSpec((tm, d), lambda j, i: (jnp.maximum(i - n_x, 0), 0)),
            pl.BlockSpec((1, 1, d), lambda j, i: (mod_row(i), 0, 0)),
            pl.BlockSpec((1, 1, d), lambda j, i: (mod_row(i), 0, 1)),
            pl.BlockSpec((d, tn), lambda j, i: (0, j)),
        ],
        out_specs=[pl.BlockSpec((tm, tn), lambda j, i: (i, j)),
                   pl.BlockSpec((tm, d), lambda j, i: (i, 0))],
        out_shape=[jax.ShapeDtypeStruct(((n_x + n_c) * tm, tn), BF16),
                   jax.ShapeDtypeStruct(((n_x + n_c) * tm, d), BF16)],
        scratch_shapes=[pltpu.VMEM((d, tn), BF16)],
        compiler_params=pltpu.CompilerParams(
            dimension_semantics=("arbitrary", "arbitrary"), vmem_limit_bytes=VMEM_LIMIT),
        name="proj_x",
    )(x2d, c2d, mod3, mod3, w)


def _lookup(j, values):
    out = values[-1]
    for idx in range(len(values) - 2, -1, -1):
        out = jnp.where(j == idx, values[idx], out)
    return out


def _proj(h, w, gain, cos_t, sin_t, *, kind, col_tiles, m, x_rows, seq, tm=PROJ_TM, tn=PROJ_TN, rc=CAST_ROWS):
    d = h.shape[1]
    mm = DOT_ROWS
    x_tiles, seq_tiles = x_rows // tm, seq // tm
    rope_tile = lambda i: jnp.where(i < x_tiles, i % seq_tiles, seq_tiles + i - x_tiles)
    kern = functools.partial(_proj_kernel, kind, mm, rc)
    n = len(col_tiles) * tn
    return pl.pallas_call(
        kern,
        grid=(len(col_tiles), m // tm),
        in_specs=[
            pl.BlockSpec((tm, d), lambda j, i: (i, 0)),
            pl.BlockSpec((d, tn), lambda j, i: (0, _lookup(j, col_tiles))),
            pl.BlockSpec((1, LANES), lambda j, i: (0, 0)),
            pl.BlockSpec((tm, LANES), lambda j, i: (rope_tile(i), 0)),
            pl.BlockSpec((tm, LANES), lambda j, i: (rope_tile(i), 0)),
        ],
        out_specs=pl.BlockSpec((tm, tn), lambda j, i: (i, j)),
        out_shape=jax.ShapeDtypeStruct((m, n), BF16),
        scratch_shapes=[pltpu.VMEM((d, tn), BF16)],
        compiler_params=pltpu.CompilerParams(
            dimension_semantics=("arbitrary", "arbitrary"), vmem_limit_bytes=VMEM_LIMIT),
        name="proj_" + kind,
    )(h, w, gain, cos_t, sin_t)


Q_ROWS = 2
AHEAD = 3
N_SLOTS = AHEAD + 1
HEADS_PER_STEP = 2


class _BandPlan:
    def __init__(self, rows, win_r):
        self.rows, self.win_r = rows, win_r
        band = win_r + Q_ROWS - 1
        self.band = band + band % 2
        self.groups = list(range(0, rows, Q_ROWS))
        self.start = {r0: min(max(r0 - win_r // 2, 0), rows - self.band) for r0 in self.groups}
        self.offsets = sorted({r0 - s0 for r0, s0 in self.start.items()})

    def table(self, r0):
        return self.offsets.index(r0 - self.start[r0])

    def valid_slots(self, r, s0):
        rs = min(max(r - self.win_r // 2, 0), self.rows - self.win_r)
        return tuple(rs <= s0 + i < rs + self.win_r for i in range(self.band))


def _build_bias_tables(rpb_ref, bias_ref, plan, win_c):
    win_r = plan.win_r
    c_io = lax.broadcasted_iota(jnp.int32, (GRID_W, LANES), 0)
    l_io = lax.broadcasted_iota(jnp.int32, (GRID_W, LANES), 1)
    cs = jnp.clip(c_io - win_c // 2, 0, GRID_W - win_c)
    inwin = (l_io >= cs) & (l_io < cs + win_c) & (l_io < GRID_W)
    low = l_io < GRID_W
    neg = jnp.full((GRID_W, LANES), NEG, F32)
    toep = []
    for dr in range(2 * win_r - 1):
        row = jnp.broadcast_to(rpb_ref[dr:dr + 1, :], (GRID_W, LANES))
        t = pltpu.roll(row, LANES - (win_c - 1), 1, stride=1, stride_axis=0)
        toep.append(jnp.where(inwin, t * LOG2E, NEG))
    for tb in range(len(plan.offsets)):
        same = [r0 for r0 in plan.groups if plan.table(r0) == tb]
        r0, s0 = same[0], plan.start[same[0]]
        for rho in range(Q_ROWS):
            r = r0 + rho
            valid = plan.valid_slots(r, s0)
            assert all(plan.valid_slots(o + rho, plan.start[o]) == valid for o in same)
            blocks = [toep[s0 + i - r + win_r - 1] if valid[i] else neg for i in range(plan.band)]
            for p in range(plan.band // 2):
                tile = jnp.where(low, blocks[2 * p], pltpu.roll(blocks[2 * p + 1], GRID_W, 1))
                bias_ref[tb, rho * GRID_W:(rho + 1) * GRID_W, p * LANES:(p + 1) * LANES] = tile


def _attn_kernel(plan, win_c,
                 q_ref, k_ref, v_ref, z_ref, kc_ref, vc_ref, rpb_ref, w1_ref, w2_ref, w3_ref,
                 o_ref, w1b_ref, w2b_ref, w3b_ref, bias_ref, s_ref):
    n_heads = q_ref.shape[1] // LANES

    for w_ref, wb_ref in ((w1_ref, w1b_ref), (w2_ref, w2b_ref), (w3_ref, w3b_ref)):
        wb_ref[...] = w_ref[...].astype(BF16)

    @pl.when(pl.program_id(1) == 0)
    def _():
        for hh in range(n_heads):
            _build_bias_tables(rpb_ref.at[hh], bias_ref.at[hh], plan, win_c)

    nq = Q_ROWS * GRID_W
    nk = plan.band * GRID_W
    dyn_zero = jnp.minimum(pl.program_id(0), 0)

    def scores(i, hh, r0):
        q0, k0 = r0 * GRID_W, plan.start[r0] * GRID_W
        hs = slice(hh * LANES, (hh + 1) * LANES)
        qb = q_ref[q0:q0 + nq, hs]
        s_loc = _nt_dot(qb, k_ref[k0:k0 + nk, hs]) + bias_ref[hh, plan.table(r0)]
        s_ctx = _nt_dot(qb, kc_ref[:, hs])
        slot = i % N_SLOTS + dyn_zero
        s_ref[slot, :, :nk] = s_loc
        s_ref[slot, :, nk:] = s_ctx
        return jnp.maximum(jnp.max(s_loc, axis=-1, keepdims=True), jnp.max(s_ctx, axis=-1, keepdims=True))

    def finish(i, hh, r0, m):
        q0, k0 = r0 * GRID_W, plan.start[r0] * GRID_W
        hs = slice(hh * LANES, (hh + 1) * LANES)
        pb = jnp.exp2(s_ref[i % N_SLOTS + dyn_zero] - m).astype(BF16)
        den = jnp.sum(pb.astype(F32), axis=-1, keepdims=True)
        o = (jnp.dot(pb[:, :nk], v_ref[k0:k0 + nk, hs], preferred_element_type=F32)
             + jnp.dot(pb[:, nk:], vc_ref[:, hs], preferred_element_type=F32))
        og = (o / den) * z_ref[q0:q0 + nq, hs].astype(F32)
        o_ref[q0:q0 + nq, hs] = og.astype(BF16)

    items = [(hh, r0) for hh in range(n_heads) for r0 in plan.groups]
    pending = [scores(i, *item) for i, item in enumerate(items[:AHEAD])]
    for i, item in enumerate(items):
        if i + AHEAD < len(items):
            pending.append(scores(i + AHEAD, *items[i + AHEAD]))
        finish(i, *item, pending.pop(0))


def _attention(q, k, v, z, rpb_pad, weights, *, batch, seq, ctx_len, heads, z_col0, win_r, win_c):
    plan = _BandPlan(seq // GRID_W, win_r)
    kern = functools.partial(_attn_kernel, plan, win_c)
    nq, nk = Q_ROWS * GRID_W, plan.band * GRID_W
    hps, width = HEADS_PER_STEP, HEADS_PER_STEP * LANES
    assert heads % hps == 0 and z_col0 % width == 0
    steps = heads // hps * batch
    tok = pl.BlockSpec((seq, width), lambda h, b: (b, h))
    ctx0 = batch * seq // ctx_len
    ctx = pl.BlockSpec((ctx_len, width), lambda h, b: (ctx0 + b, h))
    w_rows = [pl.BlockSpec((w.shape[0] // steps, w.shape[1]), lambda h, b: (h * batch + b, 0)) for w in weights]
    return pl.pallas_call(
        kern,
        grid=(heads // hps, batch),
        in_specs=[tok, tok, tok, pl.BlockSpec((seq, width), lambda h, b: (b, z_col0 // width + h)), ctx, ctx,
                  pl.BlockSpec((hps,) + rpb_pad.shape[1:], lambda h, b: (h, 0, 0))] + w_rows,
        out_specs=[tok] + w_rows,
        out_shape=[jax.ShapeDtypeStruct((batch * seq, heads * LANES), BF16)]
        + [jax.ShapeDtypeStruct(w.shape, BF16) for w in weights],
        scratch_shapes=[pltpu.VMEM((hps, len(plan.offsets), nq, nk), F32),
                        pltpu.VMEM((N_SLOTS, nq, nk + ctx_len), F32)],
        compiler_params=pltpu.CompilerParams(
            dimension_semantics=("arbitrary", "arbitrary"), vmem_limit_bytes=VMEM_LIMIT),
        name="attn",
    )(q, k, v, z, k, v, rpb_pad, *weights)


def _dft_mats(n):
    jk = (np.arange(n)[:, None] * np.arange(n)[None, :]) % n
    ang = 2.0 * np.pi * jk.astype(np.float64) / n
    return np.cos(ang) / np.sqrt(n), np.sin(ang) / np.sqrt(n)


def _fourier_kernel(gd, u_ref, zf_ref, csc_ref, ch_ref, sh_ref, nyq_ref, flip_ref, o_ref, a_ref, b_ref, e_ref):
    seq = u_ref.shape[0]
    half = seq // 2
    tk = ch_ref.shape[0]
    fk = flip_ref.shape[0]
    n_first = half // tk
    s = pl.program_id(1)

    @pl.when(s == 0)
    def _():
        rb = 4 * DOT_ROWS
        for g in range(u_ref.shape[1] // gd):
            for r in range(0, seq, rb):
                t = jnp.dot(u_ref[r:r + rb, g * gd:(g + 1) * gd], csc_ref[...], preferred_element_type=F32)
                a_ref[r:r + rb, g * gd:(g + 1) * gd] = t[:, :gd].astype(BF16)
                b_ref[r:r + rb, g * gd:(g + 1) * gd] = t[:, gd:].astype(BF16)
        e_ref[half:, :] = jnp.zeros((fk, e_ref.shape[1]), BF16)
        e_ref[half:half + nyq_ref.shape[0], :] = jnp.dot(
            nyq_ref[...], a_ref[...], preferred_element_type=F32).astype(BF16)

    rows = pl.ds(pl.multiple_of(s * tk, tk), tk)
    p = jnp.dot(ch_ref[...], a_ref[...], preferred_element_type=F32)
    q = jnp.dot(sh_ref[...], b_ref[...], preferred_element_type=F32)
    o_ref[rows, :] = ((p - q) * zf_ref[rows, :].astype(F32)).astype(BF16)
    e_ref[rows, :] = (p + q).astype(BF16)

    @pl.when(s == n_first - 1)
    def _():
        for t in range(half // fk):
            base = half - (t + 1) * fk
            y = jnp.dot(flip_ref[...], e_ref[base:base + 2 * fk, :], preferred_element_type=F32)
            out = slice(half + t * fk, half + (t + 1) * fk)
            o_ref[out, :] = (y * zf_ref[out, :].astype(F32)).astype(BF16)


def _fourier(u, zf, csc, ch, sh, nyq, flip, *, batch, seq, gd, tk=FOURIER_TK):
    fw = u.shape[1]
    half = seq // 2
    n_first = half // tk
    fk = flip.shape[0]
    kern = functools.partial(_fourier_kernel, gd)
    per_batch = pl.BlockSpec((seq, fw), lambda b, k: (b, 0))
    half_rows = pl.BlockSpec((tk, seq), lambda b, k: (k, 0))
    whole = lambda a: pl.BlockSpec(a.shape, lambda b, k: (0, 0))
    return pl.pallas_call(
        kern,
        grid=(batch, n_first),
        in_specs=[per_batch, per_batch, whole(csc), half_rows, half_rows, whole(nyq), whole(flip)],
        out_specs=per_batch,
        out_shape=jax.ShapeDtypeStruct((batch * seq, fw), BF16),
        scratch_shapes=[pltpu.VMEM((seq, fw), BF16), pltpu.VMEM((seq, fw), BF16),
                        pltpu.VMEM((half + fk, fw), BF16)],
        compiler_params=pltpu.CompilerParams(
            dimension_semantics=("arbitrary", "arbitrary"), vmem_limit_bytes=VMEM_LIMIT),
        name="fourier",
    )(u, zf, csc, ch, sh, nyq, flip)


def _merge_kernel(yg_ref, og_ref, sgf_ref, sga_ref, x_ref, gate_ref, wf_ref, wa_ref, wo_ref, o_ref):
    for r0 in range(0, x_ref.shape[0], MERGE_ROWS):
        r = slice(r0, r0 + MERGE_ROWS)
        yf = jnp.dot(yg_ref[r, :], wf_ref[...], preferred_element_type=F32)
        ya = jnp.dot(og_ref[r, :], wa_ref[...], preferred_element_type=F32)
        y = sgf_ref[r, :].astype(F32) * yf + sga_ref[r, :].astype(F32) * ya
        yo = jnp.dot(y.astype(BF16), wo_ref[...], preferred_element_type=F32)
        o_ref[r, :] = x_ref[r, :] + gate_ref[0] * yo


def _merge(yg, og, g, x2d, mod3, wf, wa, wo, *, seq, tm=MERGE_TM):
    m, d = x2d.shape
    tiles_per_seq = seq // tm
    const = lambda shape: pl.BlockSpec(shape, lambda i: (0, 0), pipeline_mode=pl.Buffered(1))
    return pl.pallas_call(
        _merge_kernel,
        grid=(m // tm,),
        in_specs=[
            pl.BlockSpec((tm, yg.shape[1]), lambda i: (i, 0)),
            pl.BlockSpec((tm, d), lambda i: (i, 0)),
            pl.BlockSpec((tm, d), lambda i: (i, 0)),
            pl.BlockSpec((tm, d), lambda i: (i, 1)),
            pl.BlockSpec((tm, d), lambda i: (i, 0)),
            pl.BlockSpec((1, 1, d), lambda i: (i // tiles_per_seq, 0, 2)),
            const(wf.shape), const(wa.shape), const(wo.shape),
        ],
        out_specs=pl.BlockSpec((tm, d), lambda i: (i, 0)),
        out_shape=jax.ShapeDtypeStruct((m, d), F32),
        compiler_params=pltpu.CompilerParams(
            dimension_semantics=("parallel",), vmem_limit_bytes=VMEM_LIMIT),
        name="merge",
    )(yg, og, g, g, x2d, mod3, wf, wa, wo)


def _rope_tables(seq, head_dim):
    n_freq = head_dim // 4
    t = np.arange(seq)
    pos = np.stack([t // GRID_W, t % GRID_W], axis=-1).astype(np.float32)
    inv_freq = (np.float32(ROPE_BASE) ** (-np.arange(n_freq, dtype=np.float32) / np.float32(n_freq)))
    ang = (pos[:, :, None] * inv_freq.astype(np.float32)).astype(np.float64)
    ang = np.broadcast_to(ang[:, None, :, :], (seq, 2, 2, n_freq))
    sign = np.array([-1.0, 1.0])[None, :, None, None]
    return (np.cos(ang).reshape(seq, head_dim).astype(np.float32),
            (np.sin(ang) * sign).reshape(seq, head_dim).astype(np.float32))


def _rope_lane_order(a, n_freq):
    lead = a.shape[:-1]
    return a.reshape(lead + (-1, 2, 2, n_freq)).swapaxes(-3, -2).reshape(a.shape)


def kernel(x, c, ctx, c_ctx, w_mod, b_mod, w_in, q_gain, k_gain, rpb, w_f_out, w_a_out, w_out):
    batch, seq, d = x.shape
    ctx_len = ctx.shape[1]
    depth, heads, n_dr, n_dc = rpb.shape
    assert depth == 1 and w_mod.shape[0] == 1
    head_dim = q_gain.shape[1]
    assert head_dim == LANES and seq % GRID_W == 0
    win_r, win_c = (n_dr + 1) // 2, (n_dc + 1) // 2
    attn_w = heads * head_dim
    fw = w_f_out.shape[1]
    gd = fw // F_GROUPS
    off_zf, off_q = fw, 2 * fw
    off_k, off_v, off_za = off_q + attn_w, off_q + 2 * attn_w, off_q + 3 * attn_w
    off_gf = off_za + attn_w
    off_ga = off_gf + d
    assert w_in.shape[2] == off_ga + d

    c_all = jnp.concatenate([c, c_ctx[None, :], jnp.zeros((16 - batch - 1, d), F32)], axis=0)
    mod = _mod(c_all, w_mod[0], b_mod)
    mod3 = mod.reshape(16, 1, 3 * d)

    x2d = x.reshape(batch * seq, d)
    c2d = ctx.reshape(batch * ctx_len, d)
    x_rows, all_rows = batch * seq, batch * (seq + ctx_len)
    u_f, h = _proj_x(x2d, c2d, mod3, w_in[0], batch=batch)

    n_freq = head_dim // 4
    tn = PROJ_TN
    assert batch * ctx_len == PROJ_TM and fw == tn
    cos_np, sin_np = _rope_tables(seq, head_dim)
    ident = np.ones((batch * ctx_len, head_dim), np.float32)
    cos_t = jnp.asarray(np.concatenate([cos_np, ident]))
    sin_t = jnp.asarray(np.concatenate([sin_np, 0.0 * ident]))
    qg = _rope_lane_order(q_gain, n_freq) * (float(head_dim) ** -0.5 * LOG2E)
    kg = _rope_lane_order(k_gain, n_freq)
    proj = functools.partial(_proj, h, w_in[0], cos_t=cos_t, sin_t=sin_t, x_rows=x_rows, seq=seq, tn=tn)
    tiles = lambda a, b: tuple(range(a // tn, b // tn))

    z = proj(gain=qg, kind="silu", col_tiles=tiles(off_zf, off_q) + tiles(off_za, off_gf), m=x_rows)
    q = proj(gain=qg, kind="qk", col_tiles=tiles(off_q, off_k), m=x_rows)
    k = proj(gain=kg, kind="qk", col_tiles=tiles(off_k, off_v), m=all_rows)
    v = proj(gain=qg, kind="raw", col_tiles=tiles(off_v, off_za), m=all_rows)
    g = proj(gain=qg, kind="sig", col_tiles=tiles(off_gf, off_ga + d), m=x_rows)

    rpb_pad = jnp.pad(rpb[0], ((0, 0), (0, 16 - n_dr), (0, LANES - n_dc)))
    og, wf, wa, wo = _attention(q, k, v, z, rpb_pad, (w_f_out[0], w_a_out[0], w_out[0]), batch=batch, seq=seq,
                                ctx_len=ctx_len, heads=heads, z_col0=fw, win_r=win_r, win_c=win_c)

    cc, sc = _dft_mats(gd)
    cn, sn = _dft_mats(seq)
    const = lambda a: jnp.asarray(a.astype(np.float32)).astype(BF16)
    half, fk = seq // 2, FOURIER_FLIP
    nyq = np.zeros((16, seq))
    nyq[0] = cn[half]
    flip = np.zeros((fk, 2 * fk))
    flip[np.arange(fk), fk - np.arange(fk)] = 1.0
    yg = _fourier(u_f, z, const(np.concatenate([cc, sc], axis=1)), const(cn[:half]), const(sn[:half]),
                  const(nyq), const(flip), batch=batch, seq=seq, gd=gd)

    out = _merge(yg, og, g, x2d, mod3, wf, wa, wo, seq=seq)
    return out.reshape(batch, seq, d)
```

```python
import functools

import numpy as np
import jax
import jax.numpy as jnp
from jax import lax
from jax.experimental import pallas as pl
from jax.experimental.pallas import tpu as pltpu

GRID_W = 64
F_GROUPS = 4
ROPE_BASE = 10000.0
EPS = 1e-6
NEG = -1e30
LOG2E = 1.4426950408889634
LANES = 128
VMEM_LIMIT = 56 * 1024 * 1024

MOD_TN = 512
PROJ_TN = 1024
PROJ_TM = 2048
PROJ_X_TM = 512
DOT_ROWS = 128
CAST_ROWS = 64
FOURIER_TK = 512
FOURIER_FLIP = 256
MERGE_TM = 512
MERGE_ROWS = 256

BF16 = jnp.bfloat16
F32 = jnp.float32


def _nt_dot(a, b):
    return lax.dot_general(a, b, (((1,), (1,)), ((), ())), preferred_element_type=F32)


def _mod_kernel(c_ref, w_ref, b_ref, o_ref):
    a = jax.nn.silu(c_ref[...]).astype(BF16)
    o_ref[...] = jnp.dot(a, w_ref[...].astype(BF16), preferred_element_type=F32) + b_ref[...]


def _mod(c_all, w_mod, b_mod, tn=MOD_TN):
    m, d = c_all.shape
    n = w_mod.shape[1]
    return pl.pallas_call(
        _mod_kernel,
        grid=(n // tn,),
        in_specs=[pl.BlockSpec((m, d), lambda j: (0, 0)),
                  pl.BlockSpec((d, tn), lambda j: (0, j)),
                  pl.BlockSpec((1, tn), lambda j: (0, j))],
        out_specs=pl.BlockSpec((m, tn), lambda j: (0, j)),
        out_shape=jax.ShapeDtypeStruct((m, n), F32),
        name="mod",
    )(c_all, w_mod, b_mod)


def _rope_lane_order_cols(w):
    n = w.shape[1]
    quarter = (lax.broadcasted_iota(jnp.int32, w.shape, 1) % LANES) // (LANES // 4)
    up = pltpu.roll(w, n - LANES // 4, 1)
    down = pltpu.roll(w, LANES // 4, 1)
    return jnp.where(quarter == 1, up, jnp.where(quarter == 2, down, w))


def _proj_kernel(kind, mm, rc, h_ref, w_ref, gain_ref, cos_ref, sin_ref, o_ref, wb_ref):
    tm = h_ref.shape[0]
    tn = w_ref.shape[1]

    @pl.when(pl.program_id(1) == 0)
    def _():
        _cast_weight_tile(w_ref, wb_ref, rc, kind == "qk")

    for r in range(0, tm, mm):
        a = jnp.dot(h_ref[r:r + mm, :], wb_ref[...], preferred_element_type=F32)
        if kind == "raw":
            o_ref[r:r + mm, :] = a.astype(BF16)
        elif kind == "silu":
            o_ref[r:r + mm, :] = jax.nn.silu(a).astype(BF16)
        elif kind == "sig":
            o_ref[r:r + mm, :] = jax.nn.sigmoid(a).astype(BF16)
        else:
            gain = gain_ref[...]
            for hh in range(tn // LANES):
                xh = a[:, hh * LANES:(hh + 1) * LANES]
                ms = jnp.mean(xh * xh, axis=-1, keepdims=True)
                xn = xh * lax.rsqrt(ms + EPS) * gain
                xn = xn * cos_ref[r:r + mm, :] + pltpu.roll(xn, LANES // 2, 1) * sin_ref[r:r + mm, :]
                o_ref[r:r + mm, hh * LANES:(hh + 1) * LANES] = xn.astype(BF16)


def _cast_weight_tile(w_ref, wb_ref, rc, reorder):
    def body(t, carry):
        r = pl.multiple_of(t * rc, rc)
        wt = w_ref[pl.ds(r, rc), :]
        if reorder:
            wt = _rope_lane_order_cols(wt)
        wb_ref[pl.ds(r, rc), :] = wt.astype(BF16)
        return carry
    lax.fori_loop(0, w_ref.shape[0] // rc, body, 0)


def _proj_x_kernel(mm, rc, n_x, x_ref, c_ref, shift_ref, scale_ref, w_ref, u_ref, z_ref, h_ref, wb_ref):
    tn = u_ref.shape[1]

    @pl.when(pl.program_id(0) == 0)
    def _():
        _cast_weight_tile(w_ref, wb_ref, rc, False)

    def body(t_ref):
        for r in range(0, t_ref.shape[0], mm):
            xs = t_ref[r:r + mm, :]
            ms = jnp.mean(xs * xs, axis=-1, keepdims=True)
            h = (xs * lax.rsqrt(ms + EPS) * (1.0 + scale_ref[0]) + shift_ref[0]).astype(BF16)
            h_ref[r:r + mm, :] = h
            u_ref[r:r + mm, :] = jnp.dot(h, wb_ref[:, :tn], preferred_element_type=F32).astype(BF16)
            z_ref[r:r + mm, :] = jax.nn.silu(
                jnp.dot(h, wb_ref[:, tn:], preferred_element_type=F32)).astype(BF16)

    is_ctx = pl.program_id(0) >= n_x
    pl.when(jnp.logical_not(is_ctx))(functools.partial(body, x_ref))
    pl.when(is_ctx)(functools.partial(body, c_ref))


def _proj_x(x2d, c2d, mod3, w, *, batch, tm=PROJ_X_TM, tn=PROJ_TN, mm=DOT_ROWS, rc=CAST_ROWS):
    m, d = x2d.shape
    n_x, n_c = m // tm, c2d.shape[0] // tm
    rows = (n_x + n_c) * tm
    tiles_per_mod = m // batch // tm
    mod_row = lambda i: jnp.minimum(i // tiles_per_mod, batch)
    row_tile = lambda width: pl.BlockSpec((tm, width), lambda i: (i, 0))
    return pl.pallas_call(
        functools.partial(_proj_x_kernel, mm, rc, n_x),
        grid=(n_x + n_c,),
        in_specs=[
            pl.BlockSpec((tm, d), lambda i: (jnp.minimum(i, n_x - 1), 0)),
            pl.BlockSpec((tm, d), lambda i: (jnp.maximum(i - n_x, 0), 0), pipeline_mode=pl.Buffered(1)),
            pl.BlockSpec((1, 1, d), lambda i: (mod_row(i), 0, 0)),
            pl.BlockSpec((1, 1, d), lambda i: (mod_row(i), 0, 1)),
            pl.BlockSpec((d, 2 * tn), lambda i: (0, 0), pipeline_mode=pl.Buffered(1)),
        ],
        out_specs=[row_tile(tn), row_tile(tn), row_tile(d)],
        out_shape=[jax.ShapeDtypeStruct((rows, tn), BF16), jax.ShapeDtypeStruct((rows, tn), BF16),
                   jax.ShapeDtypeStruct((rows, d), BF16)],
        scratch_shapes=[pltpu.VMEM((d, 2 * tn), BF16)],
        compiler_params=pltpu.CompilerParams(
            dimension_semantics=("arbitrary",), vmem_limit_bytes=VMEM_LIMIT),
        name="proj_x",
    )(x2d, c2d, mod3, mod3, w)


def _lookup(j, values):
    out = values[-1]
    for idx in range(len(values) - 2, -1, -1):
        out = jnp.where(j == idx, values[idx], out)
    return out


def _proj(h, w, gain, cos_t, sin_t, *, kind, col_tiles, m, x_rows, seq, tm=PROJ_TM, tn=PROJ_TN, rc=CAST_ROWS):
    d = h.shape[1]
    mm = DOT_ROWS
    x_tiles, seq_tiles = x_rows // tm, seq // tm
    rope_tile = lambda i: jnp.where(i < x_tiles, i % seq_tiles, seq_tiles + i - x_tiles)
    kern = functools.partial(_proj_kernel, kind, mm, rc)
    n = len(col_tiles) * tn
    return pl.pallas_call(
        kern,
        grid=(len(col_tiles), m // tm),
        in_specs=[
            pl.BlockSpec((tm, d), lambda j, i: (i, 0)),
            pl.BlockSpec((d, tn), lambda j, i: (0, _lookup(j, col_tiles))),
            pl.BlockSpec((1, LANES), lambda j, i: (0, 0)),
            pl.BlockSpec((tm, LANES), lambda j, i: (rope_tile(i), 0)),
            pl.BlockSpec((tm, LANES), lambda j, i: (rope_tile(i), 0)),
        ],
        out_specs=pl.BlockSpec((tm, tn), lambda j, i: (i, j)),
        out_shape=jax.ShapeDtypeStruct((m, n), BF16),
        scratch_shapes=[pltpu.VMEM((d, tn), BF16)],
        compiler_params=pltpu.CompilerParams(
            dimension_semantics=("arbitrary", "arbitrary"), vmem_limit_bytes=VMEM_LIMIT),
        name="proj_" + kind,
    )(h, w, gain, cos_t, sin_t)


Q_ROWS = 2
AHEAD = 3
N_SLOTS = AHEAD + 1
HEADS_PER_STEP = 2


class _BandPlan:
    def __init__(self, rows, win_r):
        self.rows, self.win_r = rows, win_r
        band = win_r + Q_ROWS - 1
        self.band = band + band % 2
        self.groups = list(range(0, rows, Q_ROWS))
        self.start = {r0: min(max(r0 - win_r // 2, 0), rows - self.band) for r0 in self.groups}
        self.offsets = sorted({r0 - s0 for r0, s0 in self.start.items()})

    def table(self, r0):
        return self.offsets.index(r0 - self.start[r0])

    def valid_slots(self, r, s0):
        rs = min(max(r - self.win_r // 2, 0), self.rows - self.win_r)
        return tuple(rs <= s0 + i < rs + self.win_r for i in range(self.band))


def _build_bias_tables(rpb_ref, bias_ref, plan, win_c):
    win_r = plan.win_r
    c_io = lax.broadcasted_iota(jnp.int32, (GRID_W, LANES), 0)
    l_io = lax.broadcasted_iota(jnp.int32, (GRID_W, LANES), 1)
    cs = jnp.clip(c_io - win_c // 2, 0, GRID_W - win_c)
    inwin = (l_io >= cs) & (l_io < cs + win_c) & (l_io < GRID_W)
    low = l_io < GRID_W
    neg = jnp.full((GRID_W, LANES), NEG, F32)
    toep = []
    for dr in range(2 * win_r - 1):
        row = jnp.broadcast_to(rpb_ref[dr:dr + 1, :], (GRID_W, LANES))
        t = pltpu.roll(row, LANES - (win_c - 1), 1, stride=1, stride_axis=0)
        toep.append(jnp.where(inwin, t * LOG2E, NEG))
    for tb in range(len(plan.offsets)):
        same = [r0 for r0 in plan.groups if plan.table(r0) == tb]
        r0, s0 = same[0], plan.start[same[0]]
        for rho in range(Q_ROWS):
            r = r0 + rho
            valid = plan.valid_slots(r, s0)
            assert all(plan.valid_slots(o + rho, plan.start[o]) == valid for o in same)
            blocks = [toep[s0 + i - r + win_r - 1] if valid[i] else neg for i in range(plan.band)]
            for p in range(plan.band // 2):
                tile = jnp.where(low, blocks[2 * p], pltpu.roll(blocks[2 * p + 1], GRID_W, 1))
                bias_ref[tb, rho * GRID_W:(rho + 1) * GRID_W, p * LANES:(p + 1) * LANES] = tile


def _attn_kernel(plan, win_c,
                 q_ref, k_ref, v_ref, z_ref, kc_ref, vc_ref, rpb_ref, w1_ref, w2_ref, w3_ref,
                 o_ref, w1b_ref, w2b_ref, w3b_ref, bias_ref, s_ref):
    n_heads = q_ref.shape[1] // LANES

    for w_ref, wb_ref in ((w1_ref, w1b_ref), (w2_ref, w2b_ref), (w3_ref, w3b_ref)):
        wb_ref[...] = w_ref[...].astype(BF16)

    @pl.when(pl.program_id(1) == 0)
    def _():
        for hh in range(n_heads):
            _build_bias_tables(rpb_ref.at[hh], bias_ref.at[hh], plan, win_c)

    nq = Q_ROWS * GRID_W
    nk = plan.band * GRID_W
    dyn_zero = jnp.minimum(pl.program_id(0), 0)

    def scores(i, hh, r0):
        q0, k0 = r0 * GRID_W, plan.start[r0] * GRID_W
        hs = slice(hh * LANES, (hh + 1) * LANES)
        qb = q_ref[q0:q0 + nq, hs]
        s_loc = _nt_dot(qb, k_ref[k0:k0 + nk, hs]) + bias_ref[hh, plan.table(r0)]
        s_ctx = _nt_dot(qb, kc_ref[:, hs])
        slot = i % N_SLOTS + dyn_zero
        s_ref[slot, :, :nk] = s_loc
        s_ref[slot, :, nk:] = s_ctx
        return jnp.maximum(jnp.max(s_loc, axis=-1, keepdims=True), jnp.max(s_ctx, axis=-1, keepdims=True))

    def finish(i, hh, r0, m):
        q0, k0 = r0 * GRID_W, plan.start[r0] * GRID_W
        hs = slice(hh * LANES, (hh + 1) * LANES)
        pb = jnp.exp2(s_ref[i % N_SLOTS + dyn_zero] - m).astype(BF16)
        den = jnp.sum(pb.astype(F32), axis=-1, keepdims=True)
        o = (jnp.dot(pb[:, :nk], v_ref[k0:k0 + nk, hs], preferred_element_type=F32)
             + jnp.dot(pb[:, nk:], vc_ref[:, hs], preferred_element_type=F32))
        og = (o / den) * z_ref[q0:q0 + nq, hs].astype(F32)
        o_ref[q0:q0 + nq, hs] = og.astype(BF16)

    items = [(hh, r0) for hh in range(n_heads) for r0 in plan.groups]
    pending = [scores(i, *item) for i, item in enumerate(items[:AHEAD])]
    for i, item in enumerate(items):
        if i + AHEAD < len(items):
            pending.append(scores(i + AHEAD, *items[i + AHEAD]))
        finish(i, *item, pending.pop(0))


def _attention(q, k, v, z, rpb_pad, weights, *, batch, seq, ctx_len, heads, z_col0, win_r, win_c):
    plan = _BandPlan(seq // GRID_W, win_r)
    kern = functools.partial(_attn_kernel, plan, win_c)
    nq, nk = Q_ROWS * GRID_W, plan.band * GRID_W
    hps, width = HEADS_PER_STEP, HEADS_PER_STEP * LANES
    assert heads % hps == 0 and z_col0 % width == 0
    steps = heads // hps * batch
    tok = pl.BlockSpec((seq, width), lambda h, b: (b, h))
    ctx0 = batch * seq // ctx_len
    ctx = pl.BlockSpec((ctx_len, width), lambda h, b: (ctx0 + b, h))
    w_rows = [pl.BlockSpec((w.shape[0] // steps, w.shape[1]), lambda h, b: (h * batch + b, 0)) for w in weights]
    return pl.pallas_call(
        kern,
        grid=(heads // hps, batch),
        in_specs=[tok, tok, tok, pl.BlockSpec((seq, width), lambda h, b: (b, z_col0 // width + h)), ctx, ctx,
                  pl.BlockSpec((hps,) + rpb_pad.shape[1:], lambda h, b: (h, 0, 0))] + w_rows,
        out_specs=[tok] + w_rows,
        out_shape=[jax.ShapeDtypeStruct((batch * seq, heads * LANES), BF16)]
        + [jax.ShapeDtypeStruct(w.shape, BF16) for w in weights],
        scratch_shapes=[pltpu.VMEM((hps, len(plan.offsets), nq, nk), F32),
                        pltpu.VMEM((N_SLOTS, nq, nk + ctx_len), F32)],
        compiler_params=pltpu.CompilerParams(
            dimension_semantics=("arbitrary", "arbitrary"), vmem_limit_bytes=VMEM_LIMIT),
        name="attn",
    )(q, k, v, z, k, v, rpb_pad, *weights)


def _dft_mats(n):
    jk = (np.arange(n)[:, None] * np.arange(n)[None, :]) % n
    ang = 2.0 * np.pi * jk.astype(np.float64) / n
    return np.cos(ang) / np.sqrt(n), np.sin(ang) / np.sqrt(n)


def _fourier_kernel(gd, u_ref, zf_ref, csc_ref, ch_ref, sh_ref, nyq_ref, flip_ref, o_ref, a_ref, b_ref, e_ref):
    seq = u_ref.shape[0]
    half = seq // 2
    tk = ch_ref.shape[0]
    fk = flip_ref.shape[0]
    n_first = half // tk
    s = pl.program_id(1)

    @pl.when(s == 0)
    def _():
        rb = 4 * DOT_ROWS
        for g in range(u_ref.shape[1] // gd):
            for r in range(0, seq, rb):
                t = jnp.dot(u_ref[r:r + rb, g * gd:(g + 1) * gd], csc_ref[...], preferred_element_type=F32)
                a_ref[r:r + rb, g * gd:(g + 1) * gd] = t[:, :gd].astype(BF16)
                b_ref[r:r + rb, g * gd:(g + 1) * gd] = t[:, gd:].astype(BF16)
        e_ref[half:, :] = jnp.zeros((fk, e_ref.shape[1]), BF16)
        e_ref[half:half + nyq_ref.shape[0], :] = jnp.dot(
            nyq_ref[...], a_ref[...], preferred_element_type=F32).astype(BF16)

    rows = pl.ds(pl.multiple_of(s * tk, tk), tk)
    p = jnp.dot(ch_ref[...], a_ref[...], preferred_element_type=F32)
    q = jnp.dot(sh_ref[...], b_ref[...], preferred_element_type=F32)
    o_ref[rows, :] = ((p - q) * zf_ref[rows, :].astype(F32)).astype(BF16)
    e_ref[rows, :] = (p + q).astype(BF16)

    @pl.when(s == n_first - 1)
    def _():
        for t in range(half // fk):
            base = half - (t + 1) * fk
            y = jnp.dot(flip_ref[...], e_ref[base:base + 2 * fk, :], preferred_element_type=F32)
            out = slice(half + t * fk, half + (t + 1) * fk)
            o_ref[out, :] = (y * zf_ref[out, :].astype(F32)).astype(BF16)


def _fourier(u, zf, csc, ch, sh, nyq, flip, *, batch, seq, gd, tk=FOURIER_TK):
    fw = u.shape[1]
    half = seq // 2
    n_first = half // tk
    fk = flip.shape[0]
    kern = functools.partial(_fourier_kernel, gd)
    per_batch = pl.BlockSpec((seq, fw), lambda b, k: (b, 0))
    half_rows = pl.BlockSpec((tk, seq), lambda b, k: (k, 0))
    whole = lambda a: pl.BlockSpec(a.shape, lambda b, k: (0, 0))
    return pl.pallas_call(
        kern,
        grid=(batch, n_first),
        in_specs=[per_batch, per_batch, whole(csc), half_rows, half_rows, whole(nyq), whole(flip)],
        out_specs=per_batch,
        out_shape=jax.ShapeDtypeStruct((batch * seq, fw), BF16),
        scratch_shapes=[pltpu.VMEM((seq, fw), BF16), pltpu.VMEM((seq, fw), BF16),
                        pltpu.VMEM((half + fk, fw), BF16)],
        compiler_params=pltpu.CompilerParams(
            dimension_semantics=("arbitrary", "arbitrary"), vmem_limit_bytes=VMEM_LIMIT),
        name="fourier",
    )(u, zf, csc, ch, sh, nyq, flip)


def _merge_kernel(yg_ref, og_ref, sgf_ref, sga_ref, x_ref, gate_ref, wf_ref, wa_ref, wo_ref, o_ref):
    for r0 in range(0, x_ref.shape[0], MERGE_ROWS):
        r = slice(r0, r0 + MERGE_ROWS)
        yf = jnp.dot(yg_ref[r, :], wf_ref[...], preferred_element_type=F32)
        ya = jnp.dot(og_ref[r, :], wa_ref[...], preferred_element_type=F32)
        y = sgf_ref[r, :].astype(F32) * yf + sga_ref[r, :].astype(F32) * ya
        yo = jnp.dot(y.astype(BF16), wo_ref[...], preferred_element_type=F32)
        o_ref[r, :] = x_ref[r, :] + gate_ref[0] * yo


def _merge(yg, og, g, x2d, mod3, wf, wa, wo, *, seq, tm=MERGE_TM):
    m, d = x2d.shape
    tiles_per_seq = seq // tm
    const = lambda shape: pl.BlockSpec(shape, lambda i: (0, 0), pipeline_mode=pl.Buffered(1))
    return pl.pallas_call(
        _merge_kernel,
        grid=(m // tm,),
        in_specs=[
            pl.BlockSpec((tm, yg.shape[1]), lambda i: (i, 0)),
            pl.BlockSpec((tm, d), lambda i: (i, 0)),
            pl.BlockSpec((tm, d), lambda i: (i, 0)),
            pl.BlockSpec((tm, d), lambda i: (i, 1)),
            pl.BlockSpec((tm, d), lambda i: (i, 0)),
            pl.BlockSpec((1, 1, d), lambda i: (i // tiles_per_seq, 0, 2)),
            const(wf.shape), const(wa.shape), const(wo.shape),
        ],
        out_specs=pl.BlockSpec((tm, d), lambda i: (i, 0)),
        out_shape=jax.ShapeDtypeStruct((m, d), F32),
        compiler_params=pltpu.CompilerParams(
            dimension_semantics=("parallel",), vmem_limit_bytes=VMEM_LIMIT),
        name="merge",
    )(yg, og, g, g, x2d, mod3, wf, wa, wo)


def _rope_tables(seq, head_dim):
    n_freq = head_dim // 4
    t = np.arange(seq)
    pos = np.stack([t // GRID_W, t % GRID_W], axis=-1).astype(np.float32)
    inv_freq = (np.float32(ROPE_BASE) ** (-np.arange(n_freq, dtype=np.float32) / np.float32(n_freq)))
    ang = (pos[:, :, None] * inv_freq.astype(np.float32)).astype(np.float64)
    ang = np.broadcast_to(ang[:, None, :, :], (seq, 2, 2, n_freq))
    sign = np.array([-1.0, 1.0])[None, :, None, None]
    return (np.cos(ang).reshape(seq, head_dim).astype(np.float32),
            (np.sin(ang) * sign).reshape(seq, head_dim).astype(np.float32))


def _rope_lane_order(a, n_freq):
    lead = a.shape[:-1]
    return a.reshape(lead + (-1, 2, 2, n_freq)).swapaxes(-3, -2).reshape(a.shape)


def kernel(x, c, ctx, c_ctx, w_mod, b_mod, w_in, q_gain, k_gain, rpb, w_f_out, w_a_out, w_out):
    batch, seq, d = x.shape
    ctx_len = ctx.shape[1]
    depth, heads, n_dr, n_dc = rpb.shape
    assert depth == 1 and w_mod.shape[0] == 1
    head_dim = q_gain.shape[1]
    assert head_dim == LANES and seq % GRID_W == 0
    win_r, win_c = (n_dr + 1) // 2, (n_dc + 1) // 2
    attn_w = heads * head_dim
    fw = w_f_out.shape[1]
    gd = fw // F_GROUPS
    off_zf, off_q = fw, 2 * fw
    off_k, off_v, off_za = off_q + attn_w, off_q + 2 * attn_w, off_q + 3 * attn_w
    off_gf = off_za + attn_w
    off_ga = off_gf + d
    assert w_in.shape[2] == off_ga + d

    c_all = jnp.concatenate([c, c_ctx[None, :], jnp.zeros((16 - batch - 1, d), F32)], axis=0)
    mod = _mod(c_all, w_mod[0], b_mod)
    mod3 = mod.reshape(16, 1, 3 * d)

    x2d = x.reshape(batch * seq, d)
    c2d = ctx.reshape(batch * ctx_len, d)
    x_rows, all_rows = batch * seq, batch * (seq + ctx_len)
    assert off_zf == PROJ_TN and off_q == 2 * PROJ_TN
    u_f, z_f, h = _proj_x(x2d, c2d, mod3, w_in[0], batch=batch)

    n_freq = head_dim // 4
    tn = PROJ_TN
    assert batch * ctx_len == PROJ_TM and fw == tn
    cos_np, sin_np = _rope_tables(seq, head_dim)
    ident = np.ones((batch * ctx_len, head_dim), np.float32)
    cos_t = jnp.asarray(np.concatenate([cos_np, ident]))
    sin_t = jnp.asarray(np.concatenate([sin_np, 0.0 * ident]))
    qg = _rope_lane_order(q_gain, n_freq) * (float(head_dim) ** -0.5 * LOG2E)
    kg = _rope_lane_order(k_gain, n_freq)
    proj = functools.partial(_proj, h, w_in[0], cos_t=cos_t, sin_t=sin_t, x_rows=x_rows, seq=seq, tn=tn)
    tiles = lambda a, b: tuple(range(a // tn, b // tn))

    z_a = proj(gain=qg, kind="silu", col_tiles=tiles(off_za, off_gf), m=x_rows)
    q = proj(gain=qg, kind="qk", col_tiles=tiles(off_q, off_k), m=x_rows)
    k = proj(gain=kg, kind="qk", col_tiles=tiles(off_k, off_v), m=all_rows)
    v = proj(gain=qg, kind="raw", col_tiles=tiles(off_v, off_za), m=all_rows)
    g = proj(gain=qg, kind="sig", col_tiles=tiles(off_gf, off_ga + d), m=x_rows)

    rpb_pad = jnp.pad(rpb[0], ((0, 0), (0, 16 - n_dr), (0, LANES - n_dc)))
    og, wf, wa, wo = _attention(q, k, v, z_a, rpb_pad, (w_f_out[0], w_a_out[0], w_out[0]), batch=batch, seq=seq,
                                ctx_len=ctx_len, heads=heads, z_col0=0, win_r=win_r, win_c=win_c)

    cc, sc = _dft_mats(gd)
    cn, sn = _dft_mats(seq)
    const = lambda a: jnp.asarray(a.astype(np.float32)).astype(BF16)
    half, fk = seq // 2, FOURIER_FLIP
    nyq = np.zeros((16, seq))
    nyq[0] = cn[half]
    flip = np.zeros((fk, 2 * fk))
    flip[np.arange(fk), fk - np.arange(fk)] = 1.0
    yg = _fourier(u_f, z_f, const(np.concatenate([cc, sc], axis=1)), const(cn[:half]), const(sn[:half]),
                  const(nyq), const(flip), batch=batch, seq=seq, gd=gd)

    out = _merge(yg, og, g, x2d, mod3, wf, wa, wo, seq=seq)
    return out.reshape(batch, seq, d)
```

```python
import functools

import numpy as np
import jax
import jax.numpy as jnp
from jax import lax
from jax.experimental import pallas as pl
from jax.experimental.pallas import tpu as pltpu

GRID_W = 64
F_GROUPS = 4
ROPE_BASE = 10000.0
EPS = 1e-6
NEG = -1e30
LOG2E = 1.4426950408889634
LANES = 128
VMEM_LIMIT = 56 * 1024 * 1024

MOD_TN = 512
PROJ_TN = 1024
PROJ_TM = 2048
PROJ_X_TM = 1024
DOT_ROWS = 128
CAST_ROWS = 64
FOURIER_TK = 1024
FOURIER_FLIP = 256
MERGE_TM = 512
MERGE_ROWS = 256

BF16 = jnp.bfloat16
F32 = jnp.float32


def _nt_dot(a, b):
    return lax.dot_general(a, b, (((1,), (1,)), ((), ())), preferred_element_type=F32)


def _mod_kernel(c_ref, w_ref, b_ref, o_ref):
    a = jax.nn.silu(c_ref[...]).astype(BF16)
    o_ref[:, 0, :] = jnp.dot(a, w_ref[...].astype(BF16), preferred_element_type=F32) + b_ref[...]


def _mod(c_all, w_mod, b_mod, tn=MOD_TN):
    m, d = c_all.shape
    n = w_mod.shape[1]
    return pl.pallas_call(
        _mod_kernel,
        grid=(n // tn,),
        in_specs=[pl.BlockSpec((m, d), lambda j: (0, 0)),
                  pl.BlockSpec((d, tn), lambda j: (0, j)),
                  pl.BlockSpec((1, tn), lambda j: (0, j))],
        out_specs=pl.BlockSpec((m, 1, tn), lambda j: (0, 0, j)),
        out_shape=jax.ShapeDtypeStruct((m, 1, n), F32),
        name="mod",
    )(c_all, w_mod, b_mod)


def _rope_lane_order_cols(w):
    n = w.shape[1]
    quarter = (lax.broadcasted_iota(jnp.int32, w.shape, 1) % LANES) // (LANES // 4)
    up = pltpu.roll(w, n - LANES // 4, 1)
    down = pltpu.roll(w, LANES // 4, 1)
    return jnp.where(quarter == 1, up, jnp.where(quarter == 2, down, w))


def _proj_kernel(kinds, mm, rc, h_ref, w_ref, gain_ref, cos_ref, sin_ref, o_ref, wb_ref):
    j = pl.program_id(0)
    for kind in sorted(set(kinds)):
        mine = functools.reduce(jnp.logical_or, [j == t for t, k in enumerate(kinds) if k == kind])
        pl.when(mine)(functools.partial(_proj_tile, kind, mm, rc, h_ref, w_ref, gain_ref, cos_ref, sin_ref,
                                        o_ref, wb_ref))


def _proj_tile(kind, mm, rc, h_ref, w_ref, gain_ref, cos_ref, sin_ref, o_ref, wb_ref):
    tm = h_ref.shape[0]
    tn = w_ref.shape[1]

    @pl.when(pl.program_id(1) == 0)
    def _():
        _cast_weight_tile(w_ref, wb_ref, rc, kind == "qk")

    for r in range(0, tm, mm):
        a = jnp.dot(h_ref[r:r + mm, :], wb_ref[...], preferred_element_type=F32)
        if kind == "raw":
            o_ref[r:r + mm, :] = a.astype(BF16)
        elif kind == "silu":
            o_ref[r:r + mm, :] = jax.nn.silu(a).astype(BF16)
        elif kind == "sig":
            o_ref[r:r + mm, :] = jax.nn.sigmoid(a).astype(BF16)
        else:
            gain = gain_ref[...]
            for hh in range(tn // LANES):
                xh = a[:, hh * LANES:(hh + 1) * LANES]
                ms = jnp.mean(xh * xh, axis=-1, keepdims=True)
                xn = xh * lax.rsqrt(ms + EPS) * gain
                xn = xn * cos_ref[r:r + mm, :] + pltpu.roll(xn, LANES // 2, 1) * sin_ref[r:r + mm, :]
                o_ref[r:r + mm, hh * LANES:(hh + 1) * LANES] = xn.astype(BF16)


def _cast_weight_tile(w_ref, wb_ref, rc, reorder):
    def body(t, carry):
        r = pl.multiple_of(t * rc, rc)
        wt = w_ref[pl.ds(r, rc), :]
        if reorder:
            wt = _rope_lane_order_cols(wt)
        wb_ref[pl.ds(r, rc), :] = wt.astype(BF16)
        return carry
    lax.fori_loop(0, w_ref.shape[0] // rc, body, 0)


def _proj_x_kernel(mm, rc, n_x, x_ref, c_ref, shift_ref, scale_ref, w_ref, o_ref, h_ref, wb_ref):
    @pl.when(pl.program_id(1) == 0)
    def _():
        _cast_weight_tile(w_ref, wb_ref, rc, False)

    def body(t_ref):
        for r in range(0, t_ref.shape[0], mm):
            xs = t_ref[r:r + mm, :]
            ms = jnp.mean(xs * xs, axis=-1, keepdims=True)
            h = (xs * lax.rsqrt(ms + EPS) * (1.0 + scale_ref[0]) + shift_ref[0]).astype(BF16)
            h_ref[r:r + mm, :] = h
            o_ref[r:r + mm, :] = jnp.dot(h, wb_ref[...], preferred_element_type=F32).astype(BF16)

    is_ctx = pl.program_id(1) >= n_x
    pl.when(jnp.logical_not(is_ctx))(functools.partial(body, x_ref))
    pl.when(is_ctx)(functools.partial(body, c_ref))


def _proj_x(x2d, c2d, mod3, w, *, batch, tm=PROJ_X_TM, tn=PROJ_TN, mm=DOT_ROWS, rc=CAST_ROWS):
    m, d = x2d.shape
    n_x, n_c = m // tm, c2d.shape[0] // tm
    tiles_per_mod = m // batch // tm
    mod_row = lambda i: jnp.minimum(i // tiles_per_mod, batch)
    return pl.pallas_call(
        functools.partial(_proj_x_kernel, mm, rc, n_x),
        grid=(1, n_x + n_c),
        in_specs=[
            pl.BlockSpec((tm, d), lambda j, i: (jnp.minimum(i, n_x - 1), 0)),
            pl.BlockSpec((tm, d), lambda j, i: (jnp.maximum(i - n_x, 0), 0), pipeline_mode=pl.Buffered(1)),
            pl.BlockSpec((1, 1, d), lambda j, i: (mod_row(i), 0, 0)),
            pl.BlockSpec((1, 1, d), lambda j, i: (mod_row(i), 0, 1)),
            pl.BlockSpec((d, tn), lambda j, i: (0, j), pipeline_mode=pl.Buffered(1)),
        ],
        out_specs=[pl.BlockSpec((tm, tn), lambda j, i: (i, j)),
                   pl.BlockSpec((tm, d), lambda j, i: (i, 0))],
        out_shape=[jax.ShapeDtypeStruct(((n_x + n_c) * tm, tn), BF16),
                   jax.ShapeDtypeStruct(((n_x + n_c) * tm, d), BF16)],
        scratch_shapes=[pltpu.VMEM((d, tn), BF16)],
        compiler_params=pltpu.CompilerParams(
            dimension_semantics=("arbitrary", "arbitrary"), vmem_limit_bytes=VMEM_LIMIT),
        name="proj_x",
    )(x2d, c2d, mod3, mod3, w)


def _lookup(j, values):
    out = values[-1]
    for idx in range(len(values) - 2, -1, -1):
        out = jnp.where(j == idx, values[idx], out)
    return out


def _proj(h, w, gain, cos_t, sin_t, *, kinds, col_tiles, m, x_rows, seq, tm=PROJ_TM, tn=PROJ_TN, rc=CAST_ROWS):
    d = h.shape[1]
    mm = DOT_ROWS
    x_tiles, seq_tiles = x_rows // tm, seq // tm
    rope_tile = lambda i: jnp.where(i < x_tiles, i % seq_tiles, seq_tiles + i - x_tiles)
    assert len(kinds) == len(col_tiles)
    kern = functools.partial(_proj_kernel, tuple(kinds), mm, rc)
    n = len(col_tiles) * tn
    return pl.pallas_call(
        kern,
        grid=(len(col_tiles), m // tm),
        in_specs=[
            pl.BlockSpec((tm, d), lambda j, i: (i, 0)),
            pl.BlockSpec((d, tn), lambda j, i: (0, _lookup(j, col_tiles))),
            pl.BlockSpec((1, LANES), lambda j, i: (0, 0)),
            pl.BlockSpec((tm, LANES), lambda j, i: (rope_tile(i), 0)),
            pl.BlockSpec((tm, LANES), lambda j, i: (rope_tile(i), 0)),
        ],
        out_specs=pl.BlockSpec((tm, tn), lambda j, i: (i, j)),
        out_shape=jax.ShapeDtypeStruct((m, n), BF16),
        scratch_shapes=[pltpu.VMEM((d, tn), BF16)],
        compiler_params=pltpu.CompilerParams(
            dimension_semantics=("arbitrary", "arbitrary"), vmem_limit_bytes=VMEM_LIMIT),
        name="proj",
    )(h, w, gain, cos_t, sin_t)


Q_ROWS = 2
AHEAD = 3
N_SLOTS = AHEAD + 1
HEADS_PER_STEP = 2


class _BandPlan:
    def __init__(self, rows, win_r):
        self.rows, self.win_r = rows, win_r
        band = win_r + Q_ROWS - 1
        self.band = band + band % 2
        self.groups = list(range(0, rows, Q_ROWS))
        self.start = {r0: min(max(r0 - win_r // 2, 0), rows - self.band) for r0 in self.groups}
        self.offsets = sorted({r0 - s0 for r0, s0 in self.start.items()})

    def table(self, r0):
        return self.offsets.index(r0 - self.start[r0])

    def valid_slots(self, r, s0):
        rs = min(max(r - self.win_r // 2, 0), self.rows - self.win_r)
        return tuple(rs <= s0 + i < rs + self.win_r for i in range(self.band))


def _build_bias_tables(rpb_ref, bias_ref, plan, win_c):
    win_r = plan.win_r
    c_io = lax.broadcasted_iota(jnp.int32, (GRID_W, LANES), 0)
    l_io = lax.broadcasted_iota(jnp.int32, (GRID_W, LANES), 1)
    cs = jnp.clip(c_io - win_c // 2, 0, GRID_W - win_c)
    inwin = (l_io >= cs) & (l_io < cs + win_c) & (l_io < GRID_W)
    low = l_io < GRID_W
    neg = jnp.full((GRID_W, LANES), NEG, F32)
    toep = []
    for dr in range(2 * win_r - 1):
        row = jnp.broadcast_to(rpb_ref[dr:dr + 1, :], (GRID_W, LANES))
        t = pltpu.roll(row, LANES - (win_c - 1), 1, stride=1, stride_axis=0)
        toep.append(jnp.where(inwin, t * LOG2E, NEG))
    for tb in range(len(plan.offsets)):
        same = [r0 for r0 in plan.groups if plan.table(r0) == tb]
        r0, s0 = same[0], plan.start[same[0]]
        for rho in range(Q_ROWS):
            r = r0 + rho
            valid = plan.valid_slots(r, s0)
            assert all(plan.valid_slots(o + rho, plan.start[o]) == valid for o in same)
            blocks = [toep[s0 + i - r + win_r - 1] if valid[i] else neg for i in range(plan.band)]
            for p in range(plan.band // 2):
                tile = jnp.where(low, blocks[2 * p], pltpu.roll(blocks[2 * p + 1], GRID_W, 1))
                bias_ref[tb, rho * GRID_W:(rho + 1) * GRID_W, p * LANES:(p + 1) * LANES] = tile


def _attn_kernel(plan, win_c,
                 q_ref, k_ref, v_ref, z_ref, kc_ref, vc_ref, rpb_ref, w1_ref, w2_ref, w3_ref,
                 o_ref, w1b_ref, w2b_ref, w3b_ref, bias_ref, s_ref):
    n_heads = q_ref.shape[1] // LANES

    for w_ref, wb_ref in ((w1_ref, w1b_ref), (w2_ref, w2b_ref), (w3_ref, w3b_ref)):
        wb_ref[...] = w_ref[...].astype(BF16)

    @pl.when(pl.program_id(1) == 0)
    def _():
        for hh in range(n_heads):
            _build_bias_tables(rpb_ref.at[hh], bias_ref.at[hh], plan, win_c)

    nq = Q_ROWS * GRID_W
    nk = plan.band * GRID_W
    dyn_zero = jnp.minimum(pl.program_id(0), 0)

    def scores(i, hh, r0):
        q0, k0 = r0 * GRID_W, plan.start[r0] * GRID_W
        hs = slice(hh * LANES, (hh + 1) * LANES)
        qb = q_ref[q0:q0 + nq, hs]
        s_loc = _nt_dot(qb, k_ref[k0:k0 + nk, hs]) + bias_ref[hh, plan.table(r0)]
        s_ctx = _nt_dot(qb, kc_ref[:, hs])
        slot = i % N_SLOTS + dyn_zero
        s_ref[slot, :, :nk] = s_loc
        s_ref[slot, :, nk:] = s_ctx
        return jnp.maximum(jnp.max(s_loc, axis=-1, keepdims=True), jnp.max(s_ctx, axis=-1, keepdims=True))

    def finish(i, hh, r0, m):
        q0, k0 = r0 * GRID_W, plan.start[r0] * GRID_W
        hs = slice(hh * LANES, (hh + 1) * LANES)
        pb = jnp.exp2(s_ref[i % N_SLOTS + dyn_zero] - m).astype(BF16)
        den = jnp.sum(pb.astype(F32), axis=-1, keepdims=True)
        o = (jnp.dot(pb[:, :nk], v_ref[k0:k0 + nk, hs], preferred_element_type=F32)
             + jnp.dot(pb[:, nk:], vc_ref[:, hs], preferred_element_type=F32))
        og = (o / den) * z_ref[q0:q0 + nq, hs].astype(F32)
        o_ref[q0:q0 + nq, hs] = og.astype(BF16)

    items = [(hh, r0) for hh in range(n_heads) for r0 in plan.groups]
    pending = [scores(i, *item) for i, item in enumerate(items[:AHEAD])]
    for i, item in enumerate(items):
        if i + AHEAD < len(items):
            pending.append(scores(i + AHEAD, *items[i + AHEAD]))
        finish(i, *item, pending.pop(0))


def _attention(q, k, v, z, rpb_pad, weights, *, batch, seq, ctx_len, heads, q_col0, k_col0, v_col0, z_col0,
               win_r, win_c):
    plan = _BandPlan(seq // GRID_W, win_r)
    kern = functools.partial(_attn_kernel, plan, win_c)
    nq, nk = Q_ROWS * GRID_W, plan.band * GRID_W
    hps, width = HEADS_PER_STEP, HEADS_PER_STEP * LANES
    assert heads % hps == 0 and all(c % width == 0 for c in (q_col0, k_col0, v_col0, z_col0))
    steps = heads // hps * batch
    tok = lambda col0: pl.BlockSpec((seq, width), lambda h, b: (b, col0 // width + h))
    ctx0 = batch * seq // ctx_len
    ctx = lambda col0: pl.BlockSpec((ctx_len, width), lambda h, b: (ctx0 + b, col0 // width + h))
    w_rows = [pl.BlockSpec((w.shape[0] // steps, w.shape[1]), lambda h, b: (h * batch + b, 0)) for w in weights]
    return pl.pallas_call(
        kern,
        grid=(heads // hps, batch),
        in_specs=[tok(q_col0), tok(k_col0), tok(v_col0), tok(z_col0), ctx(k_col0), ctx(v_col0),
                  pl.BlockSpec((hps,) + rpb_pad.shape[1:], lambda h, b: (h, 0, 0))] + w_rows,
        out_specs=[tok(0)] + w_rows,
        out_shape=[jax.ShapeDtypeStruct((batch * seq, heads * LANES), BF16)]
        + [jax.ShapeDtypeStruct(w.shape, BF16) for w in weights],
        scratch_shapes=[pltpu.VMEM((hps, len(plan.offsets), nq, nk), F32),
                        pltpu.VMEM((N_SLOTS, nq, nk + ctx_len), F32)],
        compiler_params=pltpu.CompilerParams(
            dimension_semantics=("arbitrary", "arbitrary"), vmem_limit_bytes=VMEM_LIMIT),
        name="attn",
    )(q, k, v, z, k, v, rpb_pad, *weights)


def _dft_mats(n):
    jk = (np.arange(n)[:, None] * np.arange(n)[None, :]) % n
    ang = 2.0 * np.pi * jk.astype(np.float64) / n
    return np.cos(ang) / np.sqrt(n), np.sin(ang) / np.sqrt(n)


def _fourier_kernel(gd, u_ref, zf_ref, csc_ref, ch_ref, sh_ref, nyq_ref, flip_ref, o_ref, a_ref, b_ref, e_ref):
    seq = u_ref.shape[0]
    half = seq // 2
    tk = ch_ref.shape[0]
    fk = flip_ref.shape[0]
    n_first = half // tk
    s = pl.program_id(1)

    @pl.when(s == 0)
    def _():
        rb = 4 * DOT_ROWS
        for g in range(u_ref.shape[1] // gd):
            for r in range(0, seq, rb):
                t = jnp.dot(u_ref[r:r + rb, g * gd:(g + 1) * gd], csc_ref[...], preferred_element_type=F32)
                a_ref[r:r + rb, g * gd:(g + 1) * gd] = t[:, :gd].astype(BF16)
                b_ref[r:r + rb, g * gd:(g + 1) * gd] = t[:, gd:].astype(BF16)
        e_ref[half:, :] = jnp.zeros((fk, e_ref.shape[1]), BF16)
        e_ref[half:half + nyq_ref.shape[0], :] = jnp.dot(
            nyq_ref[...], a_ref[...], preferred_element_type=F32).astype(BF16)

    rows = pl.ds(pl.multiple_of(s * tk, tk), tk)
    p = jnp.dot(ch_ref[...], a_ref[...], preferred_element_type=F32)
    q = jnp.dot(sh_ref[...], b_ref[...], preferred_element_type=F32)
    o_ref[rows, :] = ((p - q) * zf_ref[rows, :].astype(F32)).astype(BF16)
    e_ref[rows, :] = (p + q).astype(BF16)

    @pl.when(s == n_first - 1)
    def _():
        for t in range(half // fk):
            base = half - (t + 1) * fk
            y = jnp.dot(flip_ref[...], e_ref[base:base + 2 * fk, :], preferred_element_type=F32)
            out = slice(half + t * fk, half + (t + 1) * fk)
            o_ref[out, :] = (y * zf_ref[out, :].astype(F32)).astype(BF16)


def _fourier(u, zf, csc, ch, sh, nyq, flip, *, batch, seq, gd, zf_col0, tk=FOURIER_TK):
    fw = u.shape[1]
    half = seq // 2
    n_first = half // tk
    fk = flip.shape[0]
    kern = functools.partial(_fourier_kernel, gd)
    per_batch = pl.BlockSpec((seq, fw), lambda b, k: (b, 0))
    zf_spec = pl.BlockSpec((seq, fw), lambda b, k: (b, zf_col0 // fw))
    half_rows = pl.BlockSpec((tk, seq), lambda b, k: (k, 0))
    whole = lambda a: pl.BlockSpec(a.shape, lambda b, k: (0, 0))
    return pl.pallas_call(
        kern,
        grid=(batch, n_first),
        in_specs=[per_batch, zf_spec, whole(csc), half_rows, half_rows, whole(nyq), whole(flip)],
        out_specs=per_batch,
        out_shape=jax.ShapeDtypeStruct((batch * seq, fw), BF16),
        scratch_shapes=[pltpu.VMEM((seq, fw), BF16), pltpu.VMEM((seq, fw), BF16),
                        pltpu.VMEM((half + fk, fw), BF16)],
        compiler_params=pltpu.CompilerParams(
            dimension_semantics=("arbitrary", "arbitrary"), vmem_limit_bytes=VMEM_LIMIT),
        name="fourier",
    )(u, zf, csc, ch, sh, nyq, flip)


def _merge_kernel(yg_ref, og_ref, sgf_ref, sga_ref, x_ref, gate_ref, wf_ref, wa_ref, wo_ref, o_ref):
    for r0 in range(0, x_ref.shape[0], MERGE_ROWS):
        r = slice(r0, r0 + MERGE_ROWS)
        yf = jnp.dot(yg_ref[r, :], wf_ref[...], preferred_element_type=F32)
        ya = jnp.dot(og_ref[r, :], wa_ref[...], preferred_element_type=F32)
        y = sgf_ref[r, :].astype(F32) * yf + sga_ref[r, :].astype(F32) * ya
        yo = jnp.dot(y.astype(BF16), wo_ref[...], preferred_element_type=F32)
        o_ref[r, :] = x_ref[r, :] + gate_ref[0] * yo


def _merge(yg, og, g, x2d, mod3, wf, wa, wo, *, seq, tm=MERGE_TM):
    m, d = x2d.shape
    tiles_per_seq = seq // tm
    const = lambda shape: pl.BlockSpec(shape, lambda i: (0, 0), pipeline_mode=pl.Buffered(1))
    return pl.pallas_call(
        _merge_kernel,
        grid=(m // tm,),
        in_specs=[
            pl.BlockSpec((tm, yg.shape[1]), lambda i: (i, 0)),
            pl.BlockSpec((tm, d), lambda i: (i, 0)),
            pl.BlockSpec((tm, d), lambda i: (i, 0)),
            pl.BlockSpec((tm, d), lambda i: (i, 1)),
            pl.BlockSpec((tm, d), lambda i: (i, 0)),
            pl.BlockSpec((1, 1, d), lambda i: (i // tiles_per_seq, 0, 2)),
            const(wf.shape), const(wa.shape), const(wo.shape),
        ],
        out_specs=pl.BlockSpec((tm, d), lambda i: (i, 0)),
        out_shape=jax.ShapeDtypeStruct((m, d), F32),
        compiler_params=pltpu.CompilerParams(
            dimension_semantics=("parallel",), vmem_limit_bytes=VMEM_LIMIT),
        name="merge",
    )(yg, og, g, g, x2d, mod3, wf, wa, wo)


def _rope_tables(seq, head_dim):
    n_freq = head_dim // 4
    t = np.arange(seq)
    pos = np.stack([t // GRID_W, t % GRID_W], axis=-1).astype(np.float32)
    inv_freq = (np.float32(ROPE_BASE) ** (-np.arange(n_freq, dtype=np.float32) / np.float32(n_freq)))
    ang = (pos[:, :, None] * inv_freq.astype(np.float32)).astype(np.float64)
    ang = np.broadcast_to(ang[:, None, :, :], (seq, 2, 2, n_freq))
    sign = np.array([-1.0, 1.0])[None, :, None, None]
    return (np.cos(ang).reshape(seq, head_dim).astype(np.float32),
            (np.sin(ang) * sign).reshape(seq, head_dim).astype(np.float32))


def _rope_lane_order(a, n_freq):
    lead = a.shape[:-1]
    return a.reshape(lead + (-1, 2, 2, n_freq)).swapaxes(-3, -2).reshape(a.shape)


def kernel(x, c, ctx, c_ctx, w_mod, b_mod, w_in, q_gain, k_gain, rpb, w_f_out, w_a_out, w_out):
    batch, seq, d = x.shape
    ctx_len = ctx.shape[1]
    depth, heads, n_dr, n_dc = rpb.shape
    assert depth == 1 and w_mod.shape[0] == 1
    head_dim = q_gain.shape[1]
    assert head_dim == LANES and seq % GRID_W == 0
    win_r, win_c = (n_dr + 1) // 2, (n_dc + 1) // 2
    attn_w = heads * head_dim
    fw = w_f_out.shape[1]
    gd = fw // F_GROUPS
    off_zf, off_q = fw, 2 * fw
    off_k, off_v, off_za = off_q + attn_w, off_q + 2 * attn_w, off_q + 3 * attn_w
    off_gf = off_za + attn_w
    off_ga = off_gf + d
    assert w_in.shape[2] == off_ga + d

    c_all = jnp.concatenate([c, c_ctx[None, :], jnp.zeros((16 - batch - 1, d), F32)], axis=0)
    mod3 = _mod(c_all, w_mod[0], b_mod)

    x2d = x.reshape(batch * seq, d)
    c2d = ctx.reshape(batch * ctx_len, d)
    x_rows, all_rows = batch * seq, batch * (seq + ctx_len)
    u_f, h = _proj_x(x2d, c2d, mod3, w_in[0], batch=batch)

    n_freq = head_dim // 4
    tn = PROJ_TN
    assert batch * ctx_len == PROJ_TM and fw == tn
    cos_np, sin_np = _rope_tables(seq, head_dim)
    ident = np.ones((batch * ctx_len, head_dim), np.float32)
    cos_t = jnp.asarray(np.concatenate([cos_np, ident]))
    sin_t = jnp.asarray(np.concatenate([sin_np, 0.0 * ident]))
    qg = _rope_lane_order(q_gain, n_freq) * (float(head_dim) ** -0.5 * LOG2E)
    kg = _rope_lane_order(k_gain, n_freq)
    proj = functools.partial(_proj, h, w_in[0], cos_t=cos_t, sin_t=sin_t, x_rows=x_rows, seq=seq, tn=tn)
    tiles = lambda a, b: tuple(range(a // tn, b // tn))

    lat_parts = [("sig", off_gf, off_ga + d), ("silu", off_zf, off_q), ("silu", off_za, off_gf), ("qk", off_q, off_k)]
    lat = proj(gain=qg, kinds=sum(((kd,) * len(tiles(a, b)) for kd, a, b in lat_parts), ()),
               col_tiles=sum((tiles(a, b) for _, a, b in lat_parts), ()), m=x_rows)
    zf_col0 = 2 * d
    za_col0, q_col0 = zf_col0 + fw, zf_col0 + fw + attn_w
    kv = proj(gain=kg, kinds=("qk",) * (attn_w // tn) + ("raw",) * (attn_w // tn),
              col_tiles=tiles(off_k, off_v) + tiles(off_v, off_za), m=all_rows)
    g = lat

    rpb_pad = jnp.pad(rpb[0], ((0, 0), (0, 16 - n_dr), (0, LANES - n_dc)))
    og, wf, wa, wo = _attention(lat, kv, kv, lat, rpb_pad, (w_f_out[0], w_a_out[0], w_out[0]), batch=batch,
                                seq=seq, ctx_len=ctx_len, heads=heads, q_col0=q_col0, k_col0=0, v_col0=attn_w,
                                z_col0=za_col0, win_r=win_r, win_c=win_c)

    cc, sc = _dft_mats(gd)
    cn, sn = _dft_mats(seq)
    const = lambda a: jnp.asarray(a.astype(np.float32)).astype(BF16)
    half, fk = seq // 2, FOURIER_FLIP
    nyq = np.zeros((16, seq))
    nyq[0] = cn[half]
    flip = np.zeros((fk, 2 * fk))
    flip[np.arange(fk), fk - np.arange(fk)] = 1.0
    yg = _fourier(u_f, lat, const(np.concatenate([cc, sc], axis=1)), const(cn[:half]), const(sn[:half]),
                  const(nyq), const(flip), batch=batch, seq=seq, gd=gd, zf_col0=zf_col0)

    out = _merge(yg, og, g, x2d, mod3, wf, wa, wo, seq=seq)
    return out.reshape(batch, seq, d)
```
